```python
import jax, jax.numpy as jnp
from jax import lax
import numpy as np

D_MODEL = 2048
BATCH = 4
SEQ = 2048
DEPTH = 1
DEC_BATCH = 128
DEC_SEQ = 1
PAST_LEN = 16384
PAGE_SIZE = 128

H_A = 8
CH_A = D_MODEL // (2 * H_A)
W_A = H_A * CH_A
CHUNK = 128
H_B = 8
DK_B = D_MODEL // (2 * H_B)
DV_B = DK_B
W_B = H_B * DV_B
RET_CHUNK = 128
ROPE_THETA = 10000.0
IN_COLS = 2 * W_A + 4 * W_B
N_MEM = 256
H_X = 4
DH_X = D_MODEL // H_X
D_FF = 4 * D_MODEL
EPS = 1e-6

kernel_name = "hymba_gmlp_retnet_memxattn_step"


def rmsnorm(x, g):
    xf = x.astype(jnp.float32)
    y = xf * lax.rsqrt(jnp.mean(xf * xf, axis=-1, keepdims=True) + EPS)
    return (y * g.astype(jnp.float32)).astype(x.dtype)


def layernorm(x, g):
    xf = x.astype(jnp.float32)
    mu = jnp.mean(xf, axis=-1, keepdims=True)
    var = jnp.mean(jnp.square(xf - mu), axis=-1, keepdims=True)
    return ((xf - mu) * lax.rsqrt(var + EPS) * g.astype(jnp.float32)).astype(x.dtype)


def rotary(x, pos):
    half = x.shape[-1] // 2
    freqs = ROPE_THETA ** (-jnp.arange(half, dtype=jnp.float32) / half)
    ang = pos[:, None] * freqs[None, :]
    cos = jnp.cos(ang)[None, :, None, :]
    sin = jnp.sin(ang)[None, :, None, :]
    xf = x.astype(jnp.float32)
    x1, x2 = xf[..., :half], xf[..., half:]
    return jnp.concatenate([x1 * cos - x2 * sin, x1 * sin + x2 * cos], axis=-1)


def chunk_spatial_mix(v, w_s, b):
    B, S, H, C = v.shape
    c = CHUNK if S >= CHUNK else S
    nc = -(-S // c)
    pad = nc * c - S
    vp = jnp.pad(v, ((0, 0), (0, pad), (0, 0), (0, 0))).reshape(B, nc, c, H, C)
    w = w_s[:, :c, :c] * jnp.tril(jnp.ones((c, c), w_s.dtype))
    out = jnp.einsum('hij,bnjhc->bnihc', w, vp) + jnp.transpose(b[:, :c])[None, None, :, :, None]
    return out.reshape(B, nc * c, H, C)[:, :S]


def retention(q, k, v, log_g, s0):
    B, S, H, dk = q.shape
    dv = v.shape[-1]
    c = RET_CHUNK if S % RET_CHUNK == 0 else S
    nc = S // c
    idx = jnp.arange(c, dtype=jnp.float32)
    diff = idx[:, None] - idx[None, :]
    decay_mask = jnp.where(diff[None] >= 0,
                           jnp.exp(log_g[:, None, None] * jnp.maximum(diff, 0.0)[None]), 0.0)
    q_decay = jnp.transpose(jnp.exp(log_g[:, None] * (idx[None, :] + 1.0)))[None, :, :, None]
    k_decay = jnp.transpose(jnp.exp(log_g[:, None] * (c - 1.0 - idx[None, :])))[None, :, :, None]
    chunk_decay = jnp.exp(log_g * c)[None, :, None, None]

    def to_chunks(t):
        return t.reshape(B, nc, c, H, t.shape[-1]).transpose(1, 0, 2, 3, 4)

    def step(state, inp):
        qc, kc, vc = inp
        scores = jnp.einsum('bihd,bjhd->bhij', qc, kc) * decay_mask[None]
        intra = jnp.einsum('bhij,bjhe->bihe', scores, vc)
        cross = jnp.einsum('bihd,bhde->bihe', qc, state) * q_decay
        new_state = state * chunk_decay + jnp.einsum('bjhd,bjhe->bhde', kc * k_decay, vc)
        return new_state, intra + cross

    s_final, out = lax.scan(step, s0, (to_chunks(q), to_chunks(k), to_chunks(v)))
    out = out.transpose(1, 0, 2, 3, 4).reshape(B, S, H, dv)
    return out, s_final


def mixing_sublayer(xn, pos, s0, w_in, sgu_norm_g, sgu_w_s, sgu_b, ret_gn_g, w_out):
    B, S, _ = xn.shape
    proj = xn @ w_in
    z_u, z_v, q, k, v, g = jnp.split(
        proj, [W_A, 2 * W_A, 2 * W_A + W_B, 2 * W_A + 2 * W_B, 2 * W_A + 3 * W_B], axis=-1)
    z_u = jax.nn.gelu(z_u)
    z_v = layernorm(jax.nn.gelu(z_v), sgu_norm_g)
    v_rows = z_v.reshape(B, S, H_A, CH_A)
    out_a = z_u * chunk_spatial_mix(v_rows, sgu_w_s, sgu_b).reshape(B, S, W_A)
    log_g = jnp.log(1.0 - jnp.exp2(-5.0 - jnp.arange(H_B, dtype=jnp.float32)))
    qh = rotary(q.reshape(B, S, H_B, DK_B), pos)
    kh = rotary(k.reshape(B, S, H_B, DK_B), pos) * (DK_B ** -0.5)
    vh = v.reshape(B, S, H_B, DV_B).astype(jnp.float32)
    ret, s_new = retention(qh, kh, vh, log_g, s0.astype(jnp.float32))
    mu = jnp.mean(ret, axis=-1, keepdims=True)
    var = jnp.mean(jnp.square(ret - mu), axis=-1, keepdims=True)
    ret = ((ret - mu) * lax.rsqrt(var + EPS)).reshape(B, S, W_B) * ret_gn_g.astype(jnp.float32)
    out_b = jax.nn.silu(g) * ret.astype(xn.dtype)
    out = jnp.concatenate([out_a, out_b], axis=-1) @ w_out
    return out, s_new.astype(s0.dtype), v_rows


def memory_kv(mem, mem_norm_g, w_ck, w_cv):
    B = mem.shape[0]
    mn = rmsnorm(mem, mem_norm_g)
    mk = (mn @ w_ck).reshape(B, N_MEM, H_X, DH_X)
    mv = (mn @ w_cv).reshape(B, N_MEM, H_X, DH_X)
    return mk, mv


def cross_attend(xn, mk, mv, w_cq, w_co):
    B, S, _ = xn.shape
    q = (xn @ w_cq).reshape(B, S, H_X, DH_X)
    s = jnp.einsum('bshd,bmhd->bhsm', q, mk).astype(jnp.float32) * (DH_X ** -0.5)
    p = jax.nn.softmax(s, axis=-1).astype(xn.dtype)
    o = jnp.einsum('bhsm,bmhd->bshd', p, mv).reshape(B, S, D_MODEL)
    return o @ w_co


def decoder_layer(x, pos, s0, mk, mv, norm1_g, w_in, sgu_norm_g, sgu_w_s, sgu_b, ret_gn_g,
                  w_out, norm2_g, w_cq, w_co, norm3_g, w_ff1, w_ff2):
    mix, s_new, v_rows = mixing_sublayer(rmsnorm(x, norm1_g), pos, s0, w_in, sgu_norm_g,
                                         sgu_w_s, sgu_b, ret_gn_g, w_out)
    h = x + mix
    h = h + cross_attend(rmsnorm(h, norm2_g), mk, mv, w_cq, w_co)
    h = h + jnp.square(jax.nn.relu(rmsnorm(h, norm3_g) @ w_ff1)) @ w_ff2
    return h, s_new, v_rows


def setup_inputs(seed: int = 0) -> dict:
    key = jax.random.key(seed)
    ks = jax.random.split(key, 24)
    f32 = jnp.float32

    def nrm(k, shape, scale):
        return jax.random.normal(k, shape, f32) * scale

    def gain(k, shape):
        return 1.0 + 0.01 * jax.random.normal(k, shape, f32)

    return {
        "x_prompt": nrm(ks[0], (BATCH, SEQ, D_MODEL), 1.0),
        "x_sample": nrm(ks[1], (DEC_BATCH, DEC_SEQ, D_MODEL), 1.0),
        "mem_prompt": nrm(ks[2], (BATCH, N_MEM, D_MODEL), 1.0),
        "cache_mem_k": nrm(ks[3], (DEPTH, DEC_BATCH, N_MEM, H_X, DH_X), 1.0),
        "cache_mem_v": nrm(ks[4], (DEPTH, DEC_BATCH, N_MEM, H_X, DH_X), 1.0),
        "state_ret": nrm(ks[5], (DEPTH, DEC_BATCH, H_B, DK_B, DV_B), 0.1),
        "norm1_g": gain(ks[6], (DEPTH, D_MODEL)),
        "w_in": nrm(ks[7], (DEPTH, D_MODEL, IN_COLS), D_MODEL ** -0.5),
        "sgu_norm_g": gain(ks[8], (DEPTH, W_A)),
        "sgu_w_s": nrm(ks[9], (DEPTH, H_A, CHUNK, CHUNK), 0.5 * CHUNK ** -0.5),
        "sgu_b": 1.0 + 0.01 * jax.random.normal(ks[10], (DEPTH, H_A, CHUNK), f32),
        "ret_gn_g": gain(ks[11], (DEPTH, W_B)),
        "w_out": nrm(ks[12], (DEPTH, W_A + W_B, D_MODEL), (W_A + W_B) ** -0.5),
        "norm2_g": gain(ks[13], (DEPTH, D_MODEL)),
        "mem_norm_g": gain(ks[14], (DEPTH, D_MODEL)),
        "w_cq": nrm(ks[15], (DEPTH, D_MODEL, D_MODEL), D_MODEL ** -0.5),
        "w_ck": nrm(ks[16], (DEPTH, D_MODEL, D_MODEL), D_MODEL ** -0.5),
        "w_cv": nrm(ks[17], (DEPTH, D_MODEL, D_MODEL), D_MODEL ** -0.5),
        "w_co": nrm(ks[18], (DEPTH, D_MODEL, D_MODEL), D_MODEL ** -0.5),
        "norm3_g": gain(ks[19], (DEPTH, D_MODEL)),
        "w_ff1": nrm(ks[20], (DEPTH, D_MODEL, D_FF), D_MODEL ** -0.5),
        "w_ff2": nrm(ks[21], (DEPTH, D_FF, D_MODEL), D_FF ** -0.5),
        "final_norm_g": gain(ks[22], (D_MODEL,)),
    }


def reference(x_prompt, x_sample, mem_prompt, cache_mem_k, cache_mem_v, state_ret,
              norm1_g, w_in, sgu_norm_g, sgu_w_s, sgu_b, ret_gn_g, w_out, norm2_g,
              mem_norm_g, w_cq, w_ck, w_cv, w_co, norm3_g, w_ff1, w_ff2, final_norm_g):
    pos_prompt = jnp.arange(SEQ, dtype=jnp.float32)
    pos_sample = jnp.arange(DEC_SEQ, dtype=jnp.float32) + PAST_LEN
    h_p, h_s = x_prompt, x_sample
    mk_list, mv_list, sp_list, ss_list, vs_list = [], [], [], [], []
    for l in range(DEPTH):
        mk_p, mv_p = memory_kv(mem_prompt, mem_norm_g[l], w_ck[l], w_cv[l])
        s0_p = jnp.zeros((BATCH, H_B, DK_B, DV_B), state_ret.dtype)
        h_p, s_p, _ = decoder_layer(h_p, pos_prompt, s0_p, mk_p, mv_p, norm1_g[l], w_in[l],
                                    sgu_norm_g[l], sgu_w_s[l], sgu_b[l], ret_gn_g[l], w_out[l],
                                    norm2_g[l], w_cq[l], w_co[l], norm3_g[l], w_ff1[l], w_ff2[l])
        h_s, s_s, v_s = decoder_layer(h_s, pos_sample, state_ret[l], cache_mem_k[l], cache_mem_v[l],
                                      norm1_g[l], w_in[l], sgu_norm_g[l], sgu_w_s[l], sgu_b[l],
                                      ret_gn_g[l], w_out[l], norm2_g[l], w_cq[l], w_co[l],
                                      norm3_g[l], w_ff1[l], w_ff2[l])
        mk_list.append(mk_p)
        mv_list.append(mv_p)
        sp_list.append(s_p)
        ss_list.append(s_s)
        vs_list.append(v_s)
    y_prompt = rmsnorm(h_p, final_norm_g)
    y_sample = rmsnorm(h_s, final_norm_g)
    mem_k_prompt = jnp.stack(mk_list, axis=0)
    mem_v_prompt = jnp.stack(mv_list, axis=0)
    state_ret_prompt = jnp.stack(sp_list, axis=0)
    state_ret_sample = jnp.stack(ss_list, axis=0)
    chunk_v_sample = jnp.stack(vs_list, axis=0)
    return (y_prompt, y_sample, mem_k_prompt, mem_v_prompt, state_ret_prompt, state_ret_sample, chunk_v_sample)
```

```python
import functools
import math

import jax
import jax.numpy as jnp
from jax import lax
from jax.experimental import pallas as pl
from jax.experimental.pallas import tpu as pltpu

F32 = jnp.float32
BF16 = jnp.bfloat16

D_MODEL = 2048
SEQ = 2048
H_A = 8
CH_A = 128
W_A = H_A * CH_A
CHUNK = 128
H_B = 8
DK_B = 128
DV_B = 128
W_B = H_B * DV_B
ROPE_THETA = 10000.0
N_MEM = 256
H_X = 4
DH_X = D_MODEL // H_X
D_FF = 4 * D_MODEL
EPS = 1e-6
PAST_LEN = 16384

SEC = 1024
N_SEC = 6
LOG_G = tuple(math.log(1.0 - 2.0 ** (-5.0 - h)) for h in range(H_B))

V7X_VMEM_BYTES = 64 * 1024 * 1024
VMEM_LIMIT = V7X_VMEM_BYTES - 4 * 1024 * 1024
ROW_CHUNK = 256


def _params(sem):
    return pltpu.CompilerParams(dimension_semantics=sem, vmem_limit_bytes=VMEM_LIMIT)


def _row_loop(n_rows, fn):
    rc = min(n_rows, ROW_CHUNK)
    assert n_rows % rc == 0

    def body(c, carry):
        fn(pl.ds(pl.multiple_of(c * rc, rc), rc))
        return carry

    lax.fori_loop(0, n_rows // rc, body, 0)


def _rms_rows(x_ref, g_ref, xn_ref, n_rows):
    def fn(rows):
        xf = x_ref[rows, :]
        ms = jnp.mean(xf * xf, axis=-1, keepdims=True)
        xn_ref[rows, :] = (xf * lax.rsqrt(ms + EPS) * g_ref[...]).astype(BF16)

    _row_loop(n_rows, fn)


def _inproj_kernel(x_ref, g_ref, w_ref, cc_ref, ss_ref, lng_ref, o_ref, xn_ref, acc_ref, *, tm):
    j = pl.program_id(1)
    odt = o_ref.dtype

    @pl.when(j == 0)
    def _():
        _rms_rows(x_ref, g_ref, xn_ref, tm)

    acc_ref[...] = jnp.dot(xn_ref[...], w_ref[...].astype(BF16), preferred_element_type=F32)

    @pl.when(j == 0)
    def _():
        def fn(rows):
            o_ref[rows, :] = jax.nn.gelu(acc_ref[rows, :]).astype(odt)
        _row_loop(tm, fn)

    @pl.when(j == 1)
    def _():
        def fn(rows):
            z = jax.nn.gelu(acc_ref[rows, :])
            mu = jnp.mean(z, axis=-1, keepdims=True)
            zc = z - mu
            var = jnp.mean(zc * zc, axis=-1, keepdims=True)
            o_ref[rows, :] = (zc * lax.rsqrt(var + EPS) * lng_ref[...]).astype(odt)
        _row_loop(tm, fn)

    @pl.when((j == 2) | (j == 3))
    def _():
        scale = jnp.where(j == 3, DK_B ** -0.5, 1.0).astype(F32)

        def fn(rows):
            cc = cc_ref[rows, :]
            ss = ss_ref[rows, :]
            for h in range(H_B):
                cols = slice(h * DK_B, (h + 1) * DK_B)
                blk = acc_ref[rows, cols]
                rot = pltpu.roll(blk, DK_B // 2, 1)
                o_ref[rows, cols] = ((blk * cc + rot * ss) * scale).astype(odt)
        _row_loop(tm, fn)

    @pl.when(j == 4)
    def _():
        def fn(rows):
            o_ref[rows, :] = acc_ref[rows, :].astype(odt)
        _row_loop(tm, fn)

    @pl.when(j == 5)
    def _():
        def fn(rows):
            o_ref[rows, :] = jax.nn.silu(acc_ref[rows, :]).astype(odt)
        _row_loop(tm, fn)


def _in_proj(x, g, w, cc, ss, lng, *, tm, rope_blocks, out_dtype):
    t = x.shape[0]
    return pl.pallas_call(
        functools.partial(_inproj_kernel, tm=tm),
        grid=(t // tm, N_SEC),
        in_specs=[
            pl.BlockSpec((tm, D_MODEL), lambda i, j: (i, 0)),
            pl.BlockSpec((1, D_MODEL), lambda i, j: (0, 0)),
            pl.BlockSpec((D_MODEL, SEC), lambda i, j: (0, j)),
            pl.BlockSpec((tm, DK_B), lambda i, j: (i % rope_blocks, 0)),
            pl.BlockSpec((tm, DK_B), lambda i, j: (i % rope_blocks, 0)),
            pl.BlockSpec((1, SEC), lambda i, j: (0, 0)),
        ],
        out_specs=pl.BlockSpec((tm, SEC), lambda i, j: (i, j)),
        out_shape=jax.ShapeDtypeStruct((t, N_SEC * SEC), out_dtype),
        scratch_shapes=[pltpu.VMEM((tm, D_MODEL), BF16), pltpu.VMEM((tm, SEC), F32)],
        compiler_params=_params(("arbitrary", "arbitrary")),
        name="in_proj",
    )(x, g, w, cc, ss, lng)


def _nmm_kernel(x_ref, g_ref, w_ref, o_ref, xn_ref, *, tm, act):
    @pl.when(pl.program_id(1) == 0)
    def _():
        _rms_rows(x_ref, g_ref, xn_ref, tm)

    acc = jnp.dot(xn_ref[...], w_ref[...].astype(BF16), preferred_element_type=F32)
    if act == "relu2":
        acc = jnp.square(jnp.maximum(acc, 0.0))
    o_ref[...] = acc.astype(o_ref.dtype)


def _norm_matmul(x, g, w, *, tm, tn, act, out_dtype, name):
    t, k = x.shape
    n = w.shape[1]
    return pl.pallas_call(
        functools.partial(_nmm_kernel, tm=tm, act=act),
        grid=(t // tm, n // tn),
        in_specs=[
            pl.BlockSpec((tm, k), lambda i, j: (i, 0)),
            pl.BlockSpec((1, k), lambda i, j: (0, 0)),
            pl.BlockSpec((k, tn), lambda i, j: (0, j)),
        ],
        out_specs=pl.BlockSpec((tm, tn), lambda i, j: (i, j)),
        out_shape=jax.ShapeDtypeStruct((t, n), out_dtype),
        scratch_shapes=[pltpu.VMEM((tm, k), BF16)],
        compiler_params=_params(("arbitrary", "arbitrary")),
        name=name,
    )(x, g, w)


def _rmm_kernel(x_ref, w_ref, r_ref, o_ref, *, nk):
    part = jnp.dot(x_ref[...], w_ref[...].astype(BF16), preferred_element_type=F32)
    if nk == 1:
        o_ref[...] = r_ref[...] + part
    else:
        k = pl.program_id(2)

        @pl.when(k == 0)
        def _():
            o_ref[...] = r_ref[...] + part

        @pl.when(k > 0)
        def _():
            o_ref[...] += part


def _resid_matmul(x, w, r, *, tm, tn, tk, name):
    t, k = x.shape
    n = w.shape[1]
    nk = k // tk
    return pl.pallas_call(
        functools.partial(_rmm_kernel, nk=nk),
        grid=(t // tm, n // tn, nk),
        in_specs=[
            pl.BlockSpec((tm, tk), lambda i, j, kk: (i, kk)),
            pl.BlockSpec((tk, tn), lambda i, j, kk: (kk, j)),
            pl.BlockSpec((tm, tn), lambda i, j, kk: (i, j)),
        ],
        out_specs=pl.BlockSpec((tm, tn), lambda i, j, kk: (i, j)),
        out_shape=jax.ShapeDtypeStruct((t, n), F32),
        compiler_params=_params(("arbitrary", "arbitrary", "arbitrary")),
        name=name,
    )(x, w, r)


def _rms_kernel(x_ref, g_ref, o_ref, *, tm):
    def fn(rows):
        xf = x_ref[rows, :]
        ms = jnp.mean(xf * xf, axis=-1, keepdims=True)
        o_ref[rows, :] = xf * lax.rsqrt(ms + EPS) * g_ref[...]

    _row_loop(tm, fn)


def _final_norm(x, g, *, tm):
    t, d = x.shape
    return pl.pallas_call(
        functools.partial(_rms_kernel, tm=tm),
        grid=(t // tm,),
        in_specs=[pl.BlockSpec((tm, d), lambda i: (i, 0)), pl.BlockSpec((1, d), lambda i: (0, 0))],
        out_specs=pl.BlockSpec((tm, d), lambda i: (i, 0)),
        out_shape=jax.ShapeDtypeStruct((t, d), F32),
        compiler_params=_params(("arbitrary",)),
        name="final_norm",
    )(x, g)


def _mix_prompt_kernel(u_ref, v_ref, q_ref, k_ref, vb_ref, g_ref, ws_ref, bt_ref, gn_ref,
                       o_ref, so_ref, st_ref, *, n_chunks):
    n = pl.program_id(1)

    @pl.when(n == 0)
    def _():
        st_ref[...] = jnp.zeros_like(st_ref)

    ii = lax.broadcasted_iota(jnp.int32, (CHUNK, CHUNK), 0)
    jj = lax.broadcasted_iota(jnp.int32, (CHUNK, CHUNK), 1)
    causal = ii >= jj
    diff = jnp.maximum((ii - jj).astype(F32), 0.0)
    ridx = lax.broadcasted_iota(jnp.int32, (CHUNK, 1), 0).astype(F32)

    for h in range(H_A):
        cols = slice(h * CH_A, (h + 1) * CH_A)
        w = (ws_ref[h] * causal.astype(F32)).astype(BF16)
        mixed = jnp.dot(w, v_ref[:, cols], preferred_element_type=F32) + bt_ref[:, h:h + 1]
        o_ref[:, cols] = (u_ref[:, cols].astype(F32) * mixed).astype(BF16)

        lg = LOG_G[h]
        qh = q_ref[:, cols]
        kh = k_ref[:, cols]
        vh = vb_ref[:, cols]
        decay_mask = jnp.where(causal, jnp.exp(lg * diff), 0.0)
        q_decay = jnp.exp(lg * (ridx + 1.0))
        k_decay = jnp.exp(lg * (CHUNK - 1.0 - ridx))
        scores = lax.dot_general(qh, kh, (((1,), (1,)), ((), ())), preferred_element_type=F32) * decay_mask
        intra = jnp.dot(scores.astype(BF16), vh, preferred_element_type=F32)
        state = st_ref[h]
        cross = jnp.dot(qh, state.astype(BF16), preferred_element_type=F32) * q_decay
        kd = (kh.astype(F32) * k_decay).astype(BF16)
        st_ref[h] = state * math.exp(lg * CHUNK) + lax.dot_general(
            kd, vh, (((0,), (0,)), ((), ())), preferred_element_type=F32)
        ret = intra + cross
        mu = jnp.mean(ret, axis=-1, keepdims=True)
        rc = ret - mu
        var = jnp.mean(rc * rc, axis=-1, keepdims=True)
        normed = rc * lax.rsqrt(var + EPS) * gn_ref[:, cols]
        o_ref[:, W_A + h * DV_B:W_A + (h + 1) * DV_B] = (g_ref[:, cols].astype(F32) * normed).astype(BF16)

    @pl.when(n == n_chunks - 1)
    def _():
        so_ref[0] = st_ref[...]


def _mix_prompt(proj, ws, bt, gn, *, batch, n_chunks):
    t = proj.shape[0]

    def sec(s):
        return pl.BlockSpec((CHUNK, SEC), lambda b, n, s=s: (b * n_chunks + n, s))

    return pl.pallas_call(
        functools.partial(_mix_prompt_kernel, n_chunks=n_chunks),
        grid=(batch, n_chunks),
        in_specs=[sec(0), sec(1), sec(2), sec(3), sec(4), sec(5),
                  pl.BlockSpec((H_A, CHUNK, CHUNK), lambda b, n: (0, 0, 0)),
                  pl.BlockSpec((CHUNK, H_A), lambda b, n: (0, 0)),
                  pl.BlockSpec((1, W_B), lambda b, n: (0, 0))],
        out_specs=[pl.BlockSpec((CHUNK, W_A + W_B), lambda b, n: (b * n_chunks + n, 0)),
                   pl.BlockSpec((1, H_B, DK_B, DV_B), lambda b, n: (b, 0, 0, 0))],
        out_shape=[jax.ShapeDtypeStruct((t, W_A + W_B), BF16),
                   jax.ShapeDtypeStruct((batch, H_B, DK_B, DV_B), F32)],
        scratch_shapes=[pltpu.VMEM((H_B, DK_B, DV_B), F32)],
        compiler_params=_params(("arbitrary", "arbitrary")),
        name="mix_prompt",
    )(proj, proj, proj, proj, proj, proj, ws, bt, gn)


SAMPLE_ROWS = 8


def _mix_sample_kernel(u_ref, v_ref, q_ref, k_ref, vb_ref, g_ref, w0_ref, b0_ref, gn_ref, s_ref,
                       o_ref, so_ref):
    rid = lax.broadcasted_iota(jnp.int32, (SAMPLE_ROWS, 1), 0)
    o_ref[:, :W_A] = u_ref[...] * (w0_ref[...] * v_ref[...] + b0_ref[...])

    for h in range(H_B):
        cols = slice(h * DK_B, (h + 1) * DK_B)
        g_h = math.exp(LOG_G[h])
        q_blk = q_ref[:, cols]
        k_blk = k_ref[:, cols]
        v_blk = vb_ref[:, cols]
        v_bf = v_blk.astype(BF16)
        intra = jnp.sum(q_blk * k_blk, axis=-1, keepdims=True) * v_blk
        cross = jnp.zeros((SAMPLE_ROWS, DV_B), F32)
        for r in range(SAMPLE_ROWS):
            state = s_ref[r, h]
            q_only_r = jnp.where(rid == r, q_blk, 0.0).astype(BF16)
            cross = cross + jnp.dot(q_only_r, state.astype(BF16), preferred_element_type=F32)
            k_only_r = jnp.where(rid == r, k_blk, 0.0).astype(BF16)
            outer = lax.dot_general(k_only_r, v_bf, (((0,), (0,)), ((), ())), preferred_element_type=F32)
            so_ref[r, h] = state * g_h + outer
        ret = intra + cross * g_h
        mu = jnp.mean(ret, axis=-1, keepdims=True)
        rc = ret - mu
        var = jnp.mean(rc * rc, axis=-1, keepdims=True)
        normed = rc * lax.rsqrt(var + EPS) * gn_ref[:, cols]
        o_ref[:, W_A + h * DV_B:W_A + (h + 1) * DV_B] = g_ref[:, cols] * normed


def _mix_sample(proj, w0, b0, gn, state):
    t = proj.shape[0]
    rb = SAMPLE_ROWS

    def sec(s):
        return pl.BlockSpec((rb, SEC), lambda i, s=s: (i, s))

    return pl.pallas_call(
        _mix_sample_kernel,
        grid=(t // rb,),
        in_specs=[sec(0), sec(1), sec(2), sec(3), sec(4), sec(5),
                  pl.BlockSpec((1, W_A), lambda i: (0, 0)),
                  pl.BlockSpec((1, W_A), lambda i: (0, 0)),
                  pl.BlockSpec((1, W_B), lambda i: (0, 0)),
                  pl.BlockSpec((rb, H_B, DK_B, DV_B), lambda i: (i, 0, 0, 0))],
        out_specs=[pl.BlockSpec((rb, W_A + W_B), lambda i: (i, 0)),
                   pl.BlockSpec((rb, H_B, DK_B, DV_B), lambda i: (i, 0, 0, 0))],
        out_shape=[jax.ShapeDtypeStruct((t, W_A + W_B), F32),
                   jax.ShapeDtypeStruct(state.shape, F32)],
        compiler_params=_params(("arbitrary",)),
        name="mix_sample",
    )(proj, proj, proj, proj, proj, proj, w0, b0, gn, state)


def _softmax_rows(s):
    m = jnp.max(s, axis=-1, keepdims=True)
    e = jnp.exp(s - m)
    return e / jnp.sum(e, axis=-1, keepdims=True)


def _attn_prompt_kernel(q_ref, mk_ref, mv_ref, o_ref):
    for h in range(H_X):
        cols = slice(h * DH_X, (h + 1) * DH_X)
        kh = mk_ref[:, cols].astype(BF16)
        vh = mv_ref[:, cols].astype(BF16)
        s = lax.dot_general(q_ref[:, cols], kh, (((1,), (1,)), ((), ())),
                            preferred_element_type=F32) * (DH_X ** -0.5)
        p = _softmax_rows(s)
        o_ref[:, cols] = jnp.dot(p.astype(BF16), vh, preferred_element_type=F32).astype(BF16)


def _attn_prompt(q, mk, mv, *, batch, tq):
    t = q.shape[0]
    per_b = (t // batch) // tq
    return pl.pallas_call(
        _attn_prompt_kernel,
        grid=(batch, per_b),
        in_specs=[pl.BlockSpec((tq, D_MODEL), lambda b, s: (b * per_b + s, 0)),
                  pl.BlockSpec((N_MEM, D_MODEL), lambda b, s: (b, 0)),
                  pl.BlockSpec((N_MEM, D_MODEL), lambda b, s: (b, 0))],
        out_specs=pl.BlockSpec((tq, D_MODEL), lambda b, s: (b * per_b + s, 0)),
        out_shape=jax.ShapeDtypeStruct((t, D_MODEL), BF16),
        compiler_params=_params(("arbitrary", "arbitrary")),
        name="attn_prompt",
    )(q, mk, mv)


ATTN_SAMPLE_ROWS = 2


def _attn_sample_kernel(q_ref, ck_ref, cv_ref, o_ref):
    i = pl.program_id(0)
    for r in range(ATTN_SAMPLE_ROWS):
        row = i * ATTN_SAMPLE_ROWS + r
        q_rows = jnp.broadcast_to(q_ref[pl.ds(row, 1), :], (8, D_MODEL)).astype(BF16)
        for h in range(H_X):
            cols = slice(h * DH_X, (h + 1) * DH_X)
            kh = ck_ref[r, :, cols].astype(BF16)
            vh = cv_ref[r, :, cols].astype(BF16)
            s = lax.dot_general(q_rows[:, cols], kh, (((1,), (1,)), ((), ())),
                                preferred_element_type=F32) * (DH_X ** -0.5)
            p = _softmax_rows(s)
            o_h = jnp.dot(p.astype(BF16), vh, preferred_element_type=F32)
            o_ref[pl.ds(row, 1), cols] = o_h[0:1, :]


def _attn_sample(q, ck, cv):
    t = q.shape[0]
    rb = ATTN_SAMPLE_ROWS
    return pl.pallas_call(
        _attn_sample_kernel,
        grid=(t // rb,),
        in_specs=[pl.BlockSpec((t, D_MODEL), lambda i: (0, 0)),
                  pl.BlockSpec((rb, N_MEM, D_MODEL), lambda i: (i, 0, 0)),
                  pl.BlockSpec((rb, N_MEM, D_MODEL), lambda i: (i, 0, 0))],
        out_specs=pl.BlockSpec((t, D_MODEL), lambda i: (0, 0)),
        out_shape=jax.ShapeDtypeStruct((t, D_MODEL), F32),
        compiler_params=_params(("arbitrary",)),
        name="attn_sample",
    )(q, ck, cv)


def _rope_tables(pos):
    half = DK_B // 2
    freqs = ROPE_THETA ** (-jnp.arange(half, dtype=F32) / half)
    ang = pos[:, None] * freqs[None, :]
    cos = jnp.cos(ang)
    sin = jnp.sin(ang)
    return jnp.concatenate([cos, cos], axis=-1), jnp.concatenate([-sin, sin], axis=-1)


def _layer_tail(x, mix, attend, w_out, g2, w_cq, w_co, g3, w_ff1, w_ff2, gf, *, tm, q_dtype):
    t = x.shape[0]
    h1 = _resid_matmul(mix.astype(BF16), w_out, x, tm=tm, tn=1024, tk=D_MODEL, name="out_proj")
    q = _norm_matmul(h1, g2, w_cq, tm=tm, tn=1024, act=None, out_dtype=q_dtype, name="q_proj")
    att = attend(q)
    h2 = _resid_matmul(att.astype(BF16), w_co, h1, tm=tm, tn=1024, tk=D_MODEL, name="co_proj")
    hid = _norm_matmul(h2, g3, w_ff1, tm=tm, tn=1024, act="relu2", out_dtype=BF16, name="ff1")
    y = _resid_matmul(hid, w_ff2, h2, tm=tm, tn=1024, tk=2048, name="ff2")
    return _final_norm(y, gf, tm=min(t, 512))


def kernel(x_prompt, x_sample, mem_prompt, cache_mem_k, cache_mem_v, state_ret, norm1_g, w_in, sgu_norm_g, sgu_w_s, sgu_b, ret_gn_g, w_out, norm2_g, mem_norm_g, w_cq, w_ck, w_cv, w_co, norm3_g, w_ff1, w_ff2, final_norm_g):
    batch, seq, d = x_prompt.shape
    dec_batch = x_sample.shape[0]
    n_chunks = seq // CHUNK
    tp = batch * seq

    g1 = norm1_g[0][None, :]
    g2 = norm2_g[0][None, :]
    g3 = norm3_g[0][None, :]
    gm = mem_norm_g[0][None, :]
    gf = final_norm_g[None, :]
    lng = sgu_norm_g[0][None, :]
    gn = ret_gn_g[0][None, :]
    win, wout, wcq, wck, wcv, wco, wf1, wf2 = (w_in[0], w_out[0], w_cq[0], w_ck[0], w_cv[0], w_co[0],
                                                 w_ff1[0], w_ff2[0])
    ws = sgu_w_s[0]
    sb = sgu_b[0]

    tm = 1024
    cc_p, ss_p = _rope_tables(jnp.arange(seq, dtype=F32))
    xp = x_prompt.reshape(tp, d)
    proj_p = _in_proj(xp, g1, win, cc_p, ss_p, lng, tm=tm, rope_blocks=seq // tm, out_dtype=BF16)
    mix_p, state_p = _mix_prompt(proj_p, ws, sb.T, gn, batch=batch, n_chunks=n_chunks)

    mem = mem_prompt.reshape(batch * N_MEM, d)
    mk = _norm_matmul(mem, gm, wck, tm=batch * N_MEM, tn=1024, act=None, out_dtype=F32, name="mem_k")
    mv = _norm_matmul(mem, gm, wcv, tm=batch * N_MEM, tn=1024, act=None, out_dtype=F32, name="mem_v")

    y_p = _layer_tail(xp, mix_p, lambda q: _attn_prompt(q, mk, mv, batch=batch, tq=512),
                      wout, g2, wcq, wco, g3, wf1, wf2, gf, tm=tm, q_dtype=BF16)

    ts = dec_batch
    cc_s, ss_s = _rope_tables(jnp.full((ts,), PAST_LEN, dtype=F32))
    xs = x_sample.reshape(ts, d)
    proj_s = _in_proj(xs, g1, win, cc_s, ss_s, lng, tm=ts, rope_blocks=1, out_dtype=F32)
    v_rows = proj_s[:, SEC:2 * SEC]
    w0 = jnp.repeat(ws[:, 0, 0], CH_A)[None, :]
    b0 = jnp.repeat(sb[:, 0], CH_A)[None, :]
    mix_s, state_s = _mix_sample(proj_s, w0, b0, gn, state_ret[0])
    ck = cache_mem_k[0].reshape(ts, N_MEM, d)
    cv = cache_mem_v[0].reshape(ts, N_MEM, d)
    y_s = _layer_tail(xs, mix_s, lambda q: _attn_sample(q, ck, cv),
                      wout, g2, wcq, wco, g3, wf1, wf2, gf, tm=ts, q_dtype=F32)

    return (y_p.reshape(batch, seq, d),
            y_s.reshape(dec_batch, 1, d),
            mk.reshape(1, batch, N_MEM, H_X, DH_X),
            mv.reshape(1, batch, N_MEM, H_X, DH_X),
            state_p[None],
            state_s[None],
            v_rows.reshape(1, dec_batch, 1, H_A, CH_A))
```

```python
import functools
import math

import jax
import jax.numpy as jnp
from jax import lax
from jax.experimental import pallas as pl
from jax.experimental.pallas import tpu as pltpu

F32 = jnp.float32
BF16 = jnp.bfloat16

D_MODEL = 2048
SEQ = 2048
H_A = 8
CH_A = 128
W_A = H_A * CH_A
CHUNK = 128
H_B = 8
DK_B = 128
DV_B = 128
W_B = H_B * DV_B
ROPE_THETA = 10000.0
N_MEM = 256
H_X = 4
DH_X = D_MODEL // H_X
D_FF = 4 * D_MODEL
EPS = 1e-6
PAST_LEN = 16384

SEC = 1024
N_SEC = 6
LOG_G = tuple(math.log(1.0 - 2.0 ** (-5.0 - h)) for h in range(H_B))

V7X_VMEM_BYTES = 64 * 1024 * 1024
VMEM_LIMIT = V7X_VMEM_BYTES - 4 * 1024 * 1024
ROW_CHUNK = 256


def _params(sem):
    return pltpu.CompilerParams(dimension_semantics=sem, vmem_limit_bytes=VMEM_LIMIT)


def _row_loop(n_rows, fn):
    rc = min(n_rows, ROW_CHUNK)
    assert n_rows % rc == 0

    def body(c, carry):
        fn(pl.ds(pl.multiple_of(c * rc, rc), rc))
        return carry

    lax.fori_loop(0, n_rows // rc, body, 0)


def _rms_rows(x_ref, g_ref, xn_ref, n_rows):
    def fn(rows):
        xf = x_ref[rows, :]
        ms = jnp.mean(xf * xf, axis=-1, keepdims=True)
        xn_ref[rows, :] = (xf * lax.rsqrt(ms + EPS) * g_ref[...]).astype(BF16)

    _row_loop(n_rows, fn)


def _inproj_kernel(x_ref, g_ref, w_ref, cc_ref, ss_ref, lng_ref, o_ref, xn_ref, *, tm):
    j = pl.program_id(1)
    odt = o_ref.dtype

    @pl.when(j == 0)
    def _():
        _rms_rows(x_ref, g_ref, xn_ref, tm)

    def section(pred, epilogue):
        @pl.when(pred)
        def _():
            acc = jnp.dot(xn_ref[...], w_ref[...].astype(BF16), preferred_element_type=F32)
            epilogue(acc)

    def store(fn):
        def epilogue(acc):
            o_ref[...] = fn(acc).astype(odt)
        return epilogue

    def layernorm(acc):
        z = jax.nn.gelu(acc)
        mu = jnp.mean(z, axis=-1, keepdims=True)
        zc = z - mu
        var = jnp.mean(zc * zc, axis=-1, keepdims=True)
        return zc * lax.rsqrt(var + EPS) * lng_ref[...]

    def rotary(acc):
        scale = jnp.where(j == 3, DK_B ** -0.5, 1.0).astype(F32)
        cc = cc_ref[...]
        ss = ss_ref[...]
        for h in range(H_B):
            cols = slice(h * DK_B, (h + 1) * DK_B)
            blk = acc[:, cols]
            rot = pltpu.roll(blk, DK_B // 2, 1)
            o_ref[:, cols] = ((blk * cc + rot * ss) * scale).astype(odt)

    section(j == 0, store(jax.nn.gelu))
    section(j == 1, store(layernorm))
    section((j == 2) | (j == 3), rotary)
    section(j == 4, store(lambda acc: acc))
    section(j == 5, store(jax.nn.silu))


def _in_proj(x, g, w, cc, ss, lng, *, tm, rope_blocks, out_dtype):
    t = x.shape[0]
    return pl.pallas_call(
        functools.partial(_inproj_kernel, tm=tm),
        grid=(t // tm, N_SEC),
        in_specs=[
            pl.BlockSpec((tm, D_MODEL), lambda i, j: (i, 0)),
            pl.BlockSpec((1, D_MODEL), lambda i, j: (0, 0)),
            pl.BlockSpec((D_MODEL, SEC), lambda i, j: (0, j)),
            pl.BlockSpec((tm, DK_B), lambda i, j: (i % rope_blocks, 0)),
            pl.BlockSpec((tm, DK_B), lambda i, j: (i % rope_blocks, 0)),
            pl.BlockSpec((1, SEC), lambda i, j: (0, 0)),
        ],
        out_specs=pl.BlockSpec((tm, SEC), lambda i, j: (i, j)),
        out_shape=jax.ShapeDtypeStruct((t, N_SEC * SEC), out_dtype),
        scratch_shapes=[pltpu.VMEM((tm, D_MODEL), BF16)],
        compiler_params=_params(("arbitrary", "arbitrary")),
        name="in_proj",
    )(x, g, w, cc, ss, lng)


def _nmm_kernel(x_ref, g_ref, w_ref, o_ref, xn_ref, *, tm, act):
    @pl.when(pl.program_id(1) == 0)
    def _():
        _rms_rows(x_ref, g_ref, xn_ref, tm)

    acc = jnp.dot(xn_ref[...], w_ref[...].astype(BF16), preferred_element_type=F32)
    if act == "relu2":
        acc = jnp.square(jnp.maximum(acc, 0.0))
    o_ref[...] = acc.astype(o_ref.dtype)


def _norm_matmul(x, g, w, *, tm, tn, act, out_dtype, name):
    t, k = x.shape
    n = w.shape[1]
    return pl.pallas_call(
        functools.partial(_nmm_kernel, tm=tm, act=act),
        grid=(t // tm, n // tn),
        in_specs=[
            pl.BlockSpec((tm, k), lambda i, j: (i, 0)),
            pl.BlockSpec((1, k), lambda i, j: (0, 0)),
            pl.BlockSpec((k, tn), lambda i, j: (0, j)),
        ],
        out_specs=pl.BlockSpec((tm, tn), lambda i, j: (i, j)),
        out_shape=jax.ShapeDtypeStruct((t, n), out_dtype),
        scratch_shapes=[pltpu.VMEM((tm, k), BF16)],
        compiler_params=_params(("arbitrary", "arbitrary")),
        name=name,
    )(x, g, w)


def _rmm_kernel(x_ref, w_ref, r_ref, o_ref, *, nk):
    part = jnp.dot(x_ref[...], w_ref[...].astype(BF16), preferred_element_type=F32)
    if nk == 1:
        o_ref[...] = r_ref[...] + part
    else:
        k = pl.program_id(2)

        @pl.when(k == 0)
        def _():
            o_ref[...] = r_ref[...] + part

        @pl.when(k > 0)
        def _():
            o_ref[...] += part


def _resid_matmul(x, w, r, *, tm, tn, tk, name):
    t, k = x.shape
    n = w.shape[1]
    nk = k // tk
    return pl.pallas_call(
        functools.partial(_rmm_kernel, nk=nk),
        grid=(t // tm, n // tn, nk),
        in_specs=[
            pl.BlockSpec((tm, tk), lambda i, j, kk: (i, kk)),
            pl.BlockSpec((tk, tn), lambda i, j, kk: (kk, j)),
            pl.BlockSpec((tm, tn), lambda i, j, kk: (i, j)),
        ],
        out_specs=pl.BlockSpec((tm, tn), lambda i, j, kk: (i, j)),
        out_shape=jax.ShapeDtypeStruct((t, n), F32),
        compiler_params=_params(("arbitrary", "arbitrary", "arbitrary")),
        name=name,
    )(x, w, r)


def _rms_kernel(x_ref, g_ref, o_ref, *, tm):
    def fn(rows):
        xf = x_ref[rows, :]
        ms = jnp.mean(xf * xf, axis=-1, keepdims=True)
        o_ref[rows, :] = xf * lax.rsqrt(ms + EPS) * g_ref[...]

    _row_loop(tm, fn)


def _final_norm(x, g, *, tm):
    t, d = x.shape
    return pl.pallas_call(
        functools.partial(_rms_kernel, tm=tm),
        grid=(t // tm,),
        in_specs=[pl.BlockSpec((tm, d), lambda i: (i, 0)), pl.BlockSpec((1, d), lambda i: (0, 0))],
        out_specs=pl.BlockSpec((tm, d), lambda i: (i, 0)),
        out_shape=jax.ShapeDtypeStruct((t, d), F32),
        compiler_params=_params(("arbitrary",)),
        name="final_norm",
    )(x, g)


def _mix_prompt_kernel(u_ref, v_ref, q_ref, k_ref, vb_ref, g_ref, ws_ref, bt_ref, gn_ref,
                       o_ref, so_ref, st_ref, *, n_chunks):
    n = pl.program_id(1)

    @pl.when(n == 0)
    def _():
        st_ref[...] = jnp.zeros_like(st_ref)

    ii = lax.broadcasted_iota(jnp.int32, (CHUNK, CHUNK), 0)
    jj = lax.broadcasted_iota(jnp.int32, (CHUNK, CHUNK), 1)
    causal = ii >= jj
    diff = jnp.maximum((ii - jj).astype(F32), 0.0)
    ridx = lax.broadcasted_iota(jnp.int32, (CHUNK, 1), 0).astype(F32)

    for h in range(H_A):
        cols = slice(h * CH_A, (h + 1) * CH_A)
        w = (ws_ref[h] * causal.astype(F32)).astype(BF16)
        mixed = jnp.dot(w, v_ref[:, cols], preferred_element_type=F32) + bt_ref[:, h:h + 1]
        o_ref[:, cols] = (u_ref[:, cols].astype(F32) * mixed).astype(BF16)

        lg = LOG_G[h]
        qh = q_ref[:, cols]
        kh = k_ref[:, cols]
        vh = vb_ref[:, cols]
        decay_mask = jnp.where(causal, jnp.exp(lg * diff), 0.0)
        q_decay = jnp.exp(lg * (ridx + 1.0))
        k_decay = jnp.exp(lg * (CHUNK - 1.0 - ridx))
        scores = lax.dot_general(qh, kh, (((1,), (1,)), ((), ())), preferred_element_type=F32) * decay_mask
        intra = jnp.dot(scores.astype(BF16), vh, preferred_element_type=F32)
        state = st_ref[h]
        cross = jnp.dot(qh, state.astype(BF16), preferred_element_type=F32) * q_decay
        kd = (kh.astype(F32) * k_decay).astype(BF16)
        st_ref[h] = state * math.exp(lg * CHUNK) + lax.dot_general(
            kd, vh, (((0,), (0,)), ((), ())), preferred_element_type=F32)
        ret = intra + cross
        mu = jnp.mean(ret, axis=-1, keepdims=True)
        rc = ret - mu
        var = jnp.mean(rc * rc, axis=-1, keepdims=True)
        normed = rc * lax.rsqrt(var + EPS) * gn_ref[:, cols]
        o_ref[:, W_A + h * DV_B:W_A + (h + 1) * DV_B] = (g_ref[:, cols].astype(F32) * normed).astype(BF16)

    @pl.when(n == n_chunks - 1)
    def _():
        so_ref[0] = st_ref[...]


def _mix_prompt(proj, ws, bt, gn, *, batch, n_chunks):
    t = proj.shape[0]

    def sec(s):
        return pl.BlockSpec((CHUNK, SEC), lambda b, n, s=s: (b * n_chunks + n, s))

    return pl.pallas_call(
        functools.partial(_mix_prompt_kernel, n_chunks=n_chunks),
        grid=(batch, n_chunks),
        in_specs=[sec(0), sec(1), sec(2), sec(3), sec(4), sec(5),
                  pl.BlockSpec((H_A, CHUNK, CHUNK), lambda b, n: (0, 0, 0)),
                  pl.BlockSpec((CHUNK, H_A), lambda b, n: (0, 0)),
                  pl.BlockSpec((1, W_B), lambda b, n: (0, 0))],
        out_specs=[pl.BlockSpec((CHUNK, W_A + W_B), lambda b, n: (b * n_chunks + n, 0)),
                   pl.BlockSpec((1, H_B, DK_B, DV_B), lambda b, n: (b, 0, 0, 0))],
        out_shape=[jax.ShapeDtypeStruct((t, W_A + W_B), BF16),
                   jax.ShapeDtypeStruct((batch, H_B, DK_B, DV_B), F32)],
        scratch_shapes=[pltpu.VMEM((H_B, DK_B, DV_B), F32)],
        compiler_params=_params(("arbitrary", "arbitrary")),
        name="mix_prompt",
    )(proj, proj, proj, proj, proj, proj, ws, bt, gn)


SAMPLE_ROWS = 8


def _mix_sample_kernel(u_ref, v_ref, q_ref, k_ref, vb_ref, g_ref, w0_ref, b0_ref, gn_ref, s_ref,
                       o_ref, so_ref):
    rid = lax.broadcasted_iota(jnp.int32, (SAMPLE_ROWS, 1), 0)
    o_ref[:, :W_A] = u_ref[...] * (w0_ref[...] * v_ref[...] + b0_ref[...])

    for h in range(H_B):
        cols = slice(h * DK_B, (h + 1) * DK_B)
        g_h = math.exp(LOG_G[h])
        q_blk = q_ref[:, cols]
        k_blk = k_ref[:, cols]
        v_blk = vb_ref[:, cols]
        v_bf = v_blk.astype(BF16)
        intra = jnp.sum(q_blk * k_blk, axis=-1, keepdims=True) * v_blk
        cross = jnp.zeros((SAMPLE_ROWS, DV_B), F32)
        for r in range(SAMPLE_ROWS):
            state = s_ref[r, h]
            q_only_r = jnp.where(rid == r, q_blk, 0.0).astype(BF16)
            cross = cross + jnp.dot(q_only_r, state.astype(BF16), preferred_element_type=F32)
            k_only_r = jnp.where(rid == r, k_blk, 0.0).astype(BF16)
            outer = lax.dot_general(k_only_r, v_bf, (((0,), (0,)), ((), ())), preferred_element_type=F32)
            so_ref[r, h] = state * g_h + outer
        ret = intra + cross * g_h
        mu = jnp.mean(ret, axis=-1, keepdims=True)
        rc = ret - mu
        var = jnp.mean(rc * rc, axis=-1, keepdims=True)
        normed = rc * lax.rsqrt(var + EPS) * gn_ref[:, cols]
        o_ref[:, W_A + h * DV_B:W_A + (h + 1) * DV_B] = g_ref[:, cols] * normed


def _mix_sample(proj, w0, b0, gn, state):
    t = proj.shape[0]
    rb = SAMPLE_ROWS

    def sec(s):
        return pl.BlockSpec((rb, SEC), lambda i, s=s: (i, s))

    return pl.pallas_call(
        _mix_sample_kernel,
        grid=(t // rb,),
        in_specs=[sec(0), sec(1), sec(2), sec(3), sec(4), sec(5),
                  pl.BlockSpec((1, W_A), lambda i: (0, 0)),
                  pl.BlockSpec((1, W_A), lambda i: (0, 0)),
                  pl.BlockSpec((1, W_B), lambda i: (0, 0)),
                  pl.BlockSpec((rb, H_B, DK_B, DV_B), lambda i: (i, 0, 0, 0))],
        out_specs=[pl.BlockSpec((rb, W_A + W_B), lambda i: (i, 0)),
                   pl.BlockSpec((rb, H_B, DK_B, DV_B), lambda i: (i, 0, 0, 0))],
        out_shape=[jax.ShapeDtypeStruct((t, W_A + W_B), F32),
                   jax.ShapeDtypeStruct(state.shape, F32)],
        compiler_params=_params(("arbitrary",)),
        name="mix_sample",
    )(proj, proj, proj, proj, proj, proj, w0, b0, gn, state)


def _softmax_rows(s):
    m = jnp.max(s, axis=-1, keepdims=True)
    e = jnp.exp(s - m)
    return e / jnp.sum(e, axis=-1, keepdims=True)


def _attn_prompt_kernel(q_ref, mk_ref, mv_ref, o_ref):
    for h in range(H_X):
        cols = slice(h * DH_X, (h + 1) * DH_X)
        kh = mk_ref[:, cols].astype(BF16)
        vh = mv_ref[:, cols].astype(BF16)
        s = lax.dot_general(q_ref[:, cols], kh, (((1,), (1,)), ((), ())),
                            preferred_element_type=F32) * (DH_X ** -0.5)
        p = _softmax_rows(s)
        o_ref[:, cols] = jnp.dot(p.astype(BF16), vh, preferred_element_type=F32).astype(BF16)


def _attn_prompt(q, mk, mv, *, batch, tq):
    t = q.shape[0]
    per_b = (t // batch) // tq
    return pl.pallas_call(
        _attn_prompt_kernel,
        grid=(batch, per_b),
        in_specs=[pl.BlockSpec((tq, D_MODEL), lambda b, s: (b * per_b + s, 0)),
                  pl.BlockSpec((N_MEM, D_MODEL), lambda b, s: (b, 0)),
                  pl.BlockSpec((N_MEM, D_MODEL), lambda b, s: (b, 0))],
        out_specs=pl.BlockSpec((tq, D_MODEL), lambda b, s: (b * per_b + s, 0)),
        out_shape=jax.ShapeDtypeStruct((t, D_MODEL), BF16),
        compiler_params=_params(("arbitrary", "arbitrary")),
        name="attn_prompt",
    )(q, mk, mv)


ATTN_SAMPLE_ROWS = 2


def _attn_sample_kernel(q_ref, ck_ref, cv_ref, o_ref):
    i = pl.program_id(0)
    for r in range(ATTN_SAMPLE_ROWS):
        row = i * ATTN_SAMPLE_ROWS + r
        q = q_ref[row]
        s = jnp.sum(ck_ref[0, r] * q[None, :, :], axis=-1) * (DH_X ** -0.5)
        m = jnp.max(s, axis=0, keepdims=True)
        e = jnp.exp(s - m)
        p = e / jnp.sum(e, axis=0, keepdims=True)
        o_ref[row] = jnp.sum(p[:, :, None] * cv_ref[0, r], axis=0)


def _attn_sample(q, ck, cv):
    t = q.shape[0]
    rb = ATTN_SAMPLE_ROWS
    cache_spec = pl.BlockSpec((1, rb, N_MEM, H_X, DH_X), lambda i: (0, i, 0, 0, 0))
    return pl.pallas_call(
        _attn_sample_kernel,
        grid=(t // rb,),
        in_specs=[pl.BlockSpec((t, H_X, DH_X), lambda i: (0, 0, 0)), cache_spec, cache_spec],
        out_specs=pl.BlockSpec((t, H_X, DH_X), lambda i: (0, 0, 0)),
        out_shape=jax.ShapeDtypeStruct((t, H_X, DH_X), F32),
        compiler_params=_params(("arbitrary",)),
        name="attn_sample",
    )(q, ck, cv)


def _rope_tables(pos):
    half = DK_B // 2
    freqs = ROPE_THETA ** (-jnp.arange(half, dtype=F32) / half)
    ang = pos[:, None] * freqs[None, :]
    cos = jnp.cos(ang)
    sin = jnp.sin(ang)
    return jnp.concatenate([cos, cos], axis=-1), jnp.concatenate([-sin, sin], axis=-1)


def _layer_tail(x, mix, attend, w_out, g2, w_cq, w_co, g3, w_ff1, w_ff2, gf, *, tm, q_dtype):
    t = x.shape[0]
    h1 = _resid_matmul(mix.astype(BF16), w_out, x, tm=tm, tn=1024, tk=D_MODEL, name="out_proj")
    q = _norm_matmul(h1, g2, w_cq, tm=tm, tn=1024, act=None, out_dtype=q_dtype, name="q_proj")
    att = attend(q)
    h2 = _resid_matmul(att.astype(BF16), w_co, h1, tm=tm, tn=1024, tk=D_MODEL, name="co_proj")
    hid = _norm_matmul(h2, g3, w_ff1, tm=tm, tn=1024, act="relu2", out_dtype=BF16, name="ff1")
    y = _resid_matmul(hid, w_ff2, h2, tm=tm, tn=1024, tk=2048, name="ff2")
    return _final_norm(y, gf, tm=min(t, 512))


def kernel(x_prompt, x_sample, mem_prompt, cache_mem_k, cache_mem_v, state_ret, norm1_g, w_in, sgu_norm_g, sgu_w_s, sgu_b, ret_gn_g, w_out, norm2_g, mem_norm_g, w_cq, w_ck, w_cv, w_co, norm3_g, w_ff1, w_ff2, final_norm_g):
    batch, seq, d = x_prompt.shape
    dec_batch = x_sample.shape[0]
    n_chunks = seq // CHUNK
    tp = batch * seq

    g1 = norm1_g[0][None, :]
    g2 = norm2_g[0][None, :]
    g3 = norm3_g[0][None, :]
    gm = mem_norm_g[0][None, :]
    gf = final_norm_g[None, :]
    lng = sgu_norm_g[0][None, :]
    gn = ret_gn_g[0][None, :]
    win, wout, wcq, wck, wcv, wco, wf1, wf2 = (w_in[0], w_out[0], w_cq[0], w_ck[0], w_cv[0], w_co[0],
                                                 w_ff1[0], w_ff2[0])
    ws = sgu_w_s[0]
    sb = sgu_b[0]

    tm = 1024
    cc_p, ss_p = _rope_tables(jnp.arange(seq, dtype=F32))
    xp = x_prompt.reshape(tp, d)
    proj_p = _in_proj(xp, g1, win, cc_p, ss_p, lng, tm=tm, rope_blocks=seq // tm, out_dtype=BF16)
    mix_p, state_p = _mix_prompt(proj_p, ws, sb.T, gn, batch=batch, n_chunks=n_chunks)

    mem = mem_prompt.reshape(batch * N_MEM, d)
    mk = _norm_matmul(mem, gm, wck, tm=batch * N_MEM, tn=1024, act=None, out_dtype=F32, name="mem_k")
    mv = _norm_matmul(mem, gm, wcv, tm=batch * N_MEM, tn=1024, act=None, out_dtype=F32, name="mem_v")

    y_p = _layer_tail(xp, mix_p, lambda q: _attn_prompt(q, mk, mv, batch=batch, tq=512),
                      wout, g2, wcq, wco, g3, wf1, wf2, gf, tm=tm, q_dtype=BF16)

    ts = dec_batch
    cc_s, ss_s = _rope_tables(jnp.full((ts,), PAST_LEN, dtype=F32))
    xs = x_sample.reshape(ts, d)
    proj_s = _in_proj(xs, g1, win, cc_s, ss_s, lng, tm=ts, rope_blocks=1, out_dtype=F32)
    v_rows = proj_s[:, SEC:2 * SEC]
    w0 = jnp.repeat(ws[:, 0, 0], CH_A)[None, :]
    b0 = jnp.repeat(sb[:, 0], CH_A)[None, :]
    mix_s, state_s = _mix_sample(proj_s, w0, b0, gn, state_ret[0])
    y_s = _layer_tail(xs, mix_s,
                      lambda q: _attn_sample(q.reshape(ts, H_X, DH_X), cache_mem_k, cache_mem_v).reshape(ts, d),
                      wout, g2, wcq, wco, g3, wf1, wf2, gf, tm=ts, q_dtype=F32)

    return (y_p.reshape(batch, seq, d),
            y_s.reshape(dec_batch, 1, d),
            mk.reshape(1, batch, N_MEM, H_X, DH_X),
            mv.reshape(1, batch, N_MEM, H_X, DH_X),
            state_p[None],
            state_s[None],
            v_rows.reshape(1, dec_batch, 1, H_A, CH_A))
```

```python
import functools
import math

import jax
import jax.numpy as jnp
from jax import lax
from jax.experimental import pallas as pl
from jax.experimental.pallas import tpu as pltpu

F32 = jnp.float32
BF16 = jnp.bfloat16

D_MODEL = 2048
SEQ = 2048
H_A = 8
CH_A = 128
W_A = H_A * CH_A
CHUNK = 128
H_B = 8
DK_B = 128
DV_B = 128
W_B = H_B * DV_B
ROPE_THETA = 10000.0
N_MEM = 256
H_X = 4
DH_X = D_MODEL // H_X
D_FF = 4 * D_MODEL
EPS = 1e-6
PAST_LEN = 16384

SEC = 1024
N_SEC = 6
LOG_G = tuple(math.log(1.0 - 2.0 ** (-5.0 - h)) for h in range(H_B))

V7X_VMEM_BYTES = 64 * 1024 * 1024
VMEM_LIMIT = V7X_VMEM_BYTES - 4 * 1024 * 1024
ROW_CHUNK = 256


def _params(sem):
    return pltpu.CompilerParams(dimension_semantics=sem, vmem_limit_bytes=VMEM_LIMIT)


def _row_loop(n_rows, fn):
    rc = min(n_rows, ROW_CHUNK)
    assert n_rows % rc == 0

    def body(c, carry):
        fn(pl.ds(pl.multiple_of(c * rc, rc), rc))
        return carry

    lax.fori_loop(0, n_rows // rc, body, 0)


def _rms_rows(x_ref, g_ref, xn_ref, n_rows):
    def fn(rows):
        xf = x_ref[rows, :]
        ms = jnp.mean(xf * xf, axis=-1, keepdims=True)
        xn_ref[rows, :] = (xf * lax.rsqrt(ms + EPS) * g_ref[...]).astype(BF16)

    _row_loop(n_rows, fn)


def _inproj_kernel(x_ref, g_ref, w_ref, cc_ref, ss_ref, lng_ref, o_ref, xn_ref, *, tm):
    j = pl.program_id(1)
    odt = o_ref.dtype

    @pl.when(j == 0)
    def _():
        _rms_rows(x_ref, g_ref, xn_ref, tm)

    def section(pred, epilogue):
        @pl.when(pred)
        def _():
            acc = jnp.dot(xn_ref[...], w_ref[...].astype(BF16), preferred_element_type=F32)
            epilogue(acc)

    def store(fn):
        def epilogue(acc):
            o_ref[...] = fn(acc).astype(odt)
        return epilogue

    def layernorm(acc):
        z = jax.nn.gelu(acc)
        mu = jnp.mean(z, axis=-1, keepdims=True)
        zc = z - mu
        var = jnp.mean(zc * zc, axis=-1, keepdims=True)
        return zc * lax.rsqrt(var + EPS) * lng_ref[...]

    def rotary(acc):
        scale = jnp.where(j == 3, DK_B ** -0.5, 1.0).astype(F32)
        cc = cc_ref[...]
        ss = ss_ref[...]
        for h in range(H_B):
            cols = slice(h * DK_B, (h + 1) * DK_B)
            blk = acc[:, cols]
            rot = pltpu.roll(blk, DK_B // 2, 1)
            o_ref[:, cols] = ((blk * cc + rot * ss) * scale).astype(odt)

    section(j == 0, store(jax.nn.gelu))
    section(j == 1, store(layernorm))
    section((j == 2) | (j == 3), rotary)
    section(j == 4, store(lambda acc: acc))
    section(j == 5, store(jax.nn.silu))


def _in_proj(x, g, w, cc, ss, lng, *, tm, rope_blocks, out_dtype):
    t = x.shape[0]
    return pl.pallas_call(
        functools.partial(_inproj_kernel, tm=tm),
        grid=(t // tm, N_SEC),
        in_specs=[
            pl.BlockSpec((tm, D_MODEL), lambda i, j: (i, 0)),
            pl.BlockSpec((1, D_MODEL), lambda i, j: (0, 0)),
            pl.BlockSpec((D_MODEL, SEC), lambda i, j: (0, j)),
            pl.BlockSpec((tm, DK_B), lambda i, j: (i % rope_blocks, 0)),
            pl.BlockSpec((tm, DK_B), lambda i, j: (i % rope_blocks, 0)),
            pl.BlockSpec((1, SEC), lambda i, j: (0, 0)),
        ],
        out_specs=pl.BlockSpec((tm, SEC), lambda i, j: (i, j)),
        out_shape=jax.ShapeDtypeStruct((t, N_SEC * SEC), out_dtype),
        scratch_shapes=[pltpu.VMEM((tm, D_MODEL), BF16)],
        compiler_params=_params(("arbitrary", "arbitrary")),
        name="in_proj",
    )(x, g, w, cc, ss, lng)


def _nmm_kernel(x_ref, g_ref, w_ref, o_ref, xn_ref, *, tm):
    @pl.when(pl.program_id(1) == 0)
    def _():
        _rms_rows(x_ref, g_ref, xn_ref, tm)

    o_ref[...] = jnp.dot(xn_ref[...], w_ref[...].astype(BF16), preferred_element_type=F32)


def _norm_matmul(x, g, w, *, tm, tn, name):
    t, k = x.shape
    n = w.shape[1]
    return pl.pallas_call(
        functools.partial(_nmm_kernel, tm=tm),
        grid=(t // tm, n // tn),
        in_specs=[
            pl.BlockSpec((tm, k), lambda i, j: (i, 0)),
            pl.BlockSpec((1, k), lambda i, j: (0, 0)),
            pl.BlockSpec((k, tn), lambda i, j: (0, j)),
        ],
        out_specs=pl.BlockSpec((tm, tn), lambda i, j: (i, j)),
        out_shape=jax.ShapeDtypeStruct((t, n), F32),
        scratch_shapes=[pltpu.VMEM((tm, k), BF16)],
        compiler_params=_params(("arbitrary", "arbitrary")),
        name=name,
    )(x, g, w)


def _ff1_kernel(x_ref, w_ref, o_ref):
    acc = jnp.dot(x_ref[...], w_ref[...].astype(BF16), preferred_element_type=F32)
    o_ref[...] = jnp.square(jnp.maximum(acc, 0.0)).astype(o_ref.dtype)


def _ff1(x, w, *, tm, tn):
    t, k = x.shape
    n = w.shape[1]
    return pl.pallas_call(
        _ff1_kernel,
        grid=(t // tm, n // tn),
        in_specs=[pl.BlockSpec((tm, k), lambda i, j: (i, 0)),
                  pl.BlockSpec((k, tn), lambda i, j: (0, j))],
        out_specs=pl.BlockSpec((tm, tn), lambda i, j: (i, j)),
        out_shape=jax.ShapeDtypeStruct((t, n), BF16),
        compiler_params=_params(("arbitrary", "arbitrary")),
        name="ff1",
    )(x, w)


def _ff2_kernel(x_ref, w_ref, o_ref):
    @pl.when(pl.program_id(2) == 0)
    def _():
        o_ref[...] = jnp.zeros_like(o_ref)

    o_ref[...] += jnp.dot(x_ref[...], w_ref[...].astype(BF16), preferred_element_type=F32)


def _ff2(x, w, *, tm, tn, tk):
    t, k = x.shape
    n = w.shape[1]
    return pl.pallas_call(
        _ff2_kernel,
        grid=(t // tm, n // tn, k // tk),
        in_specs=[pl.BlockSpec((tm, tk), lambda i, j, kk: (i, kk)),
                  pl.BlockSpec((tk, tn), lambda i, j, kk: (kk, j))],
        out_specs=pl.BlockSpec((tm, tn), lambda i, j, kk: (i, j)),
        out_shape=jax.ShapeDtypeStruct((t, n), F32),
        compiler_params=_params(("arbitrary", "arbitrary", "arbitrary")),
        name="ff2",
    )(x, w)


def _add_norm_kernel(x_ref, r_ref, g_ref, o_ref, *, tm):
    def fn(rows):
        xf = r_ref[rows, :] + x_ref[rows, :]
        ms = jnp.mean(xf * xf, axis=-1, keepdims=True)
        o_ref[rows, :] = xf * lax.rsqrt(ms + EPS) * g_ref[...]

    _row_loop(tm, fn)


def _add_norm(x, r, g, *, tm):
    t, d = x.shape
    row = pl.BlockSpec((tm, d), lambda i: (i, 0))
    return pl.pallas_call(
        functools.partial(_add_norm_kernel, tm=tm),
        grid=(t // tm,),
        in_specs=[row, row, pl.BlockSpec((1, d), lambda i: (0, 0))],
        out_specs=row,
        out_shape=jax.ShapeDtypeStruct((t, d), F32),
        compiler_params=_params(("arbitrary",)),
        name="final_norm",
    )(x, r, g)


W_CHUNK = 256
N_LOAD = D_MODEL // W_CHUNK


def _load_weight_chunk(s, pairs):
    rows = pl.ds(pl.multiple_of(s * W_CHUNK, W_CHUNK), W_CHUNK)
    for src, dst in pairs:
        dst[rows, :] = src[...].astype(BF16)


def _w_chunk_spec():
    return pl.BlockSpec((W_CHUNK, D_MODEL), lambda s: (jnp.minimum(s, N_LOAD - 1), 0))


def _tile_spec(tm, width=D_MODEL):
    return pl.BlockSpec((tm, width), lambda s: (jnp.maximum(s - N_LOAD, 0), 0))


def _rms_bf16(h, g_ref):
    ms = jnp.mean(h * h, axis=-1, keepdims=True)
    return (h * lax.rsqrt(ms + EPS) * g_ref[...]).astype(BF16)


def _outq_kernel(mix_ref, x_ref, wo_ref, wq_ref, g2_ref, h1_ref, q_ref, wo_bf, wq_bf):
    s = pl.program_id(0)

    @pl.when(s < N_LOAD)
    def _():
        _load_weight_chunk(s, ((wo_ref, wo_bf), (wq_ref, wq_bf)))

    @pl.when(s >= N_LOAD)
    def _():
        h1 = x_ref[...] + jnp.dot(mix_ref[...], wo_bf[...], preferred_element_type=F32)
        h1_ref[...] = h1
        q_ref[...] = jnp.dot(_rms_bf16(h1, g2_ref), wq_bf[...],
                             preferred_element_type=F32).astype(q_ref.dtype)


def _out_q(mix, x, w_out, w_cq, g2, *, tm, q_dtype):
    t, d = x.shape
    return pl.pallas_call(
        _outq_kernel,
        grid=(N_LOAD + t // tm,),
        in_specs=[_tile_spec(tm), _tile_spec(tm), _w_chunk_spec(), _w_chunk_spec(),
                  pl.BlockSpec((1, d), lambda s: (0, 0))],
        out_specs=[_tile_spec(tm), _tile_spec(tm)],
        out_shape=[jax.ShapeDtypeStruct((t, d), F32), jax.ShapeDtypeStruct((t, d), q_dtype)],
        scratch_shapes=[pltpu.VMEM((d, d), BF16), pltpu.VMEM((d, d), BF16)],
        compiler_params=_params(("arbitrary",)),
        name="out_q",
    )(mix, x, w_out, w_cq, g2)


def _softmax_rows(s):
    m = jnp.max(s, axis=-1, keepdims=True)
    e = jnp.exp(s - m)
    return e / jnp.sum(e, axis=-1, keepdims=True)


def _attn_co_kernel(*refs, fused_attn):
    if fused_attn:
        q_ref, mk_ref, mv_ref, h1_ref, wc_ref, g3_ref, h2_ref, xn_ref, wc_bf, att_ref = refs
    else:
        att_ref, h1_ref, wc_ref, g3_ref, h2_ref, xn_ref, wc_bf = refs
    s = pl.program_id(0)

    @pl.when(s < N_LOAD)
    def _():
        _load_weight_chunk(s, ((wc_ref, wc_bf),))

    @pl.when(s >= N_LOAD)
    def _():
        if fused_attn:
            for h in range(H_X):
                cols = slice(h * DH_X, (h + 1) * DH_X)
                kh = mk_ref[:, cols].astype(BF16)
                vh = mv_ref[:, cols].astype(BF16)
                sc = lax.dot_general(q_ref[:, cols], kh, (((1,), (1,)), ((), ())),
                                     preferred_element_type=F32) * (DH_X ** -0.5)
                p = _softmax_rows(sc)
                att_ref[:, cols] = jnp.dot(p.astype(BF16), vh, preferred_element_type=F32).astype(BF16)
        h2 = h1_ref[...] + jnp.dot(att_ref[...], wc_bf[...], preferred_element_type=F32)
        h2_ref[...] = h2
        xn_ref[...] = _rms_bf16(h2, g3_ref)


def _attn_co(q_or_att, mem_kv, h1, w_co, g3, *, tm, tiles_per_batch=None):
    t, d = h1.shape
    fused = mem_kv is not None
    in_specs = [_tile_spec(tm)]
    args = [q_or_att]
    scratch = [pltpu.VMEM((d, d), BF16)]
    if fused:
        mem_spec = pl.BlockSpec((N_MEM, d), lambda s: (jnp.maximum(s - N_LOAD, 0) // tiles_per_batch, 0))
        in_specs += [mem_spec, mem_spec]
        args += list(mem_kv)
        scratch.append(pltpu.VMEM((tm, d), BF16))
    in_specs += [_tile_spec(tm), _w_chunk_spec(), pl.BlockSpec((1, d), lambda s: (0, 0))]
    args += [h1, w_co, g3]
    return pl.pallas_call(
        functools.partial(_attn_co_kernel, fused_attn=fused),
        grid=(N_LOAD + t // tm,),
        in_specs=in_specs,
        out_specs=[_tile_spec(tm), _tile_spec(tm)],
        out_shape=[jax.ShapeDtypeStruct((t, d), F32), jax.ShapeDtypeStruct((t, d), BF16)],
        scratch_shapes=scratch,
        compiler_params=_params(("arbitrary",)),
        name="attn_co",
    )(*args)


def _mix_prompt_kernel(u_ref, v_ref, q_ref, k_ref, vb_ref, g_ref, ws_ref, bt_ref, gn_ref,
                       o_ref, so_ref, st_ref, *, n_chunks):
    n = pl.program_id(1)

    @pl.when(n == 0)
    def _():
        st_ref[...] = jnp.zeros_like(st_ref)

    ii = lax.broadcasted_iota(jnp.int32, (CHUNK, CHUNK), 0)
    jj = lax.broadcasted_iota(jnp.int32, (CHUNK, CHUNK), 1)
    causal = ii >= jj
    diff = jnp.maximum((ii - jj).astype(F32), 0.0)
    ridx = lax.broadcasted_iota(jnp.int32, (CHUNK, 1), 0).astype(F32)

    for h in range(H_A):
        cols = slice(h * CH_A, (h + 1) * CH_A)
        w = (ws_ref[h] * causal.astype(F32)).astype(BF16)
        mixed = jnp.dot(w, v_ref[:, cols], preferred_element_type=F32) + bt_ref[:, h:h + 1]
        o_ref[:, cols] = (u_ref[:, cols].astype(F32) * mixed).astype(BF16)

        lg = LOG_G[h]
        qh = q_ref[:, cols]
        kh = k_ref[:, cols]
        vh = vb_ref[:, cols]
        decay_mask = jnp.where(causal, jnp.exp(lg * diff), 0.0)
        q_decay = jnp.exp(lg * (ridx + 1.0))
        k_decay = jnp.exp(lg * (CHUNK - 1.0 - ridx))
        scores = lax.dot_general(qh, kh, (((1,), (1,)), ((), ())), preferred_element_type=F32) * decay_mask
        intra = jnp.dot(scores.astype(BF16), vh, preferred_element_type=F32)
        state = st_ref[h]
        cross = jnp.dot(qh, state.astype(BF16), preferred_element_type=F32) * q_decay
        kd = (kh.astype(F32) * k_decay).astype(BF16)
        st_ref[h] = state * math.exp(lg * CHUNK) + lax.dot_general(
            kd, vh, (((0,), (0,)), ((), ())), preferred_element_type=F32)
        ret = intra + cross
        mu = jnp.mean(ret, axis=-1, keepdims=True)
        rc = ret - mu
        var = jnp.mean(rc * rc, axis=-1, keepdims=True)
        normed = rc * lax.rsqrt(var + EPS) * gn_ref[:, cols]
        o_ref[:, W_A + h * DV_B:W_A + (h + 1) * DV_B] = (g_ref[:, cols].astype(F32) * normed).astype(BF16)

    @pl.when(n == n_chunks - 1)
    def _():
        so_ref[0] = st_ref[...]


def _mix_prompt(proj, ws, bt, gn, *, batch, n_chunks):
    t = proj.shape[0]

    def sec(s):
        return pl.BlockSpec((CHUNK, SEC), lambda b, n, s=s: (b * n_chunks + n, s))

    return pl.pallas_call(
        functools.partial(_mix_prompt_kernel, n_chunks=n_chunks),
        grid=(batch, n_chunks),
        in_specs=[sec(0), sec(1), sec(2), sec(3), sec(4), sec(5),
                  pl.BlockSpec((H_A, CHUNK, CHUNK), lambda b, n: (0, 0, 0)),
                  pl.BlockSpec((CHUNK, H_A), lambda b, n: (0, 0)),
                  pl.BlockSpec((1, W_B), lambda b, n: (0, 0))],
        out_specs=[pl.BlockSpec((CHUNK, W_A + W_B), lambda b, n: (b * n_chunks + n, 0)),
                   pl.BlockSpec((1, H_B, DK_B, DV_B), lambda b, n: (b, 0, 0, 0))],
        out_shape=[jax.ShapeDtypeStruct((t, W_A + W_B), BF16),
                   jax.ShapeDtypeStruct((batch, H_B, DK_B, DV_B), F32)],
        scratch_shapes=[pltpu.VMEM((H_B, DK_B, DV_B), F32)],
        compiler_params=_params(("arbitrary", "arbitrary")),
        name="mix_prompt",
    )(proj, proj, proj, proj, proj, proj, ws, bt, gn)


SAMPLE_ROWS = 8


def _mix_sample_kernel(u_ref, v_ref, q_ref, k_ref, vb_ref, g_ref, w0_ref, b0_ref, gn_ref, s_ref,
                       o_ref, so_ref):
    rid = lax.broadcasted_iota(jnp.int32, (SAMPLE_ROWS, 1), 0)
    o_ref[:, :W_A] = u_ref[...] * (w0_ref[...] * v_ref[...] + b0_ref[...])

    for h in range(H_B):
        cols = slice(h * DK_B, (h + 1) * DK_B)
        g_h = math.exp(LOG_G[h])
        q_blk = q_ref[:, cols]
        k_blk = k_ref[:, cols]
        v_blk = vb_ref[:, cols]
        v_bf = v_blk.astype(BF16)
        intra = jnp.sum(q_blk * k_blk, axis=-1, keepdims=True) * v_blk
        cross = jnp.zeros((SAMPLE_ROWS, DV_B), F32)
        for r in range(SAMPLE_ROWS):
            state = s_ref[r, h]
            q_only_r = jnp.where(rid == r, q_blk, 0.0).astype(BF16)
            cross = cross + jnp.dot(q_only_r, state.astype(BF16), preferred_element_type=F32)
            k_only_r = jnp.where(rid == r, k_blk, 0.0).astype(BF16)
            outer = lax.dot_general(k_only_r, v_bf, (((0,), (0,)), ((), ())), preferred_element_type=F32)
            so_ref[r, h] = state * g_h + outer
        ret = intra + cross * g_h
        mu = jnp.mean(ret, axis=-1, keepdims=True)
        rc = ret - mu
        var = jnp.mean(rc * rc, axis=-1, keepdims=True)
        normed = rc * lax.rsqrt(var + EPS) * gn_ref[:, cols]
        o_ref[:, W_A + h * DV_B:W_A + (h + 1) * DV_B] = g_ref[:, cols] * normed


def _mix_sample(proj, w0, b0, gn, state):
    t = proj.shape[0]
    rb = SAMPLE_ROWS

    def sec(s):
        return pl.BlockSpec((rb, SEC), lambda i, s=s: (i, s))

    return pl.pallas_call(
        _mix_sample_kernel,
        grid=(t // rb,),
        in_specs=[sec(0), sec(1), sec(2), sec(3), sec(4), sec(5),
                  pl.BlockSpec((1, W_A), lambda i: (0, 0)),
                  pl.BlockSpec((1, W_A), lambda i: (0, 0)),
                  pl.BlockSpec((1, W_B), lambda i: (0, 0)),
                  pl.BlockSpec((rb, H_B, DK_B, DV_B), lambda i: (i, 0, 0, 0))],
        out_specs=[pl.BlockSpec((rb, W_A + W_B), lambda i: (i, 0)),
                   pl.BlockSpec((rb, H_B, DK_B, DV_B), lambda i: (i, 0, 0, 0))],
        out_shape=[jax.ShapeDtypeStruct((t, W_A + W_B), F32),
                   jax.ShapeDtypeStruct(state.shape, F32)],
        compiler_params=_params(("arbitrary",)),
        name="mix_sample",
    )(proj, proj, proj, proj, proj, proj, w0, b0, gn, state)


ATTN_SAMPLE_ROWS = 2


def _attn_sample_kernel(q_ref, ck_ref, cv_ref, o_ref):
    i = pl.program_id(0)
    for r in range(ATTN_SAMPLE_ROWS):
        row = i * ATTN_SAMPLE_ROWS + r
        q = q_ref[row]
        s = jnp.sum(ck_ref[0, r] * q[None, :, :], axis=-1) * (DH_X ** -0.5)
        m = jnp.max(s, axis=0, keepdims=True)
        e = jnp.exp(s - m)
        p = e / jnp.sum(e, axis=0, keepdims=True)
        o_ref[row] = jnp.sum(p[:, :, None] * cv_ref[0, r], axis=0)


def _attn_sample(q, ck, cv):
    t = q.shape[0]
    rb = ATTN_SAMPLE_ROWS
    cache_spec = pl.BlockSpec((1, rb, N_MEM, H_X, DH_X), lambda i: (0, i, 0, 0, 0))
    return pl.pallas_call(
        _attn_sample_kernel,
        grid=(t // rb,),
        in_specs=[pl.BlockSpec((t, H_X, DH_X), lambda i: (0, 0, 0)), cache_spec, cache_spec],
        out_specs=pl.BlockSpec((t, H_X, DH_X), lambda i: (0, 0, 0)),
        out_shape=jax.ShapeDtypeStruct((t, H_X, DH_X), F32),
        compiler_params=_params(("arbitrary",)),
        name="attn_sample",
    )(q, ck, cv)


def _rope_tables(pos):
    half = DK_B // 2
    freqs = ROPE_THETA ** (-jnp.arange(half, dtype=F32) / half)
    ang = pos[:, None] * freqs[None, :]
    cos = jnp.cos(ang)
    sin = jnp.sin(ang)
    return jnp.concatenate([cos, cos], axis=-1), jnp.concatenate([-sin, sin], axis=-1)


def _mlp_tail(h2, xn3, w_ff1, w_ff2, gf, *, tm):
    t = h2.shape[0]
    hid = _ff1(xn3, w_ff1, tm=tm, tn=1024)
    part = _ff2(hid, w_ff2, tm=tm, tn=1024, tk=1024)
    return _add_norm(part, h2, gf, tm=min(t, 512))


def kernel(x_prompt, x_sample, mem_prompt, cache_mem_k, cache_mem_v, state_ret, norm1_g, w_in, sgu_norm_g, sgu_w_s, sgu_b, ret_gn_g, w_out, norm2_g, mem_norm_g, w_cq, w_ck, w_cv, w_co, norm3_g, w_ff1, w_ff2, final_norm_g):
    batch, seq, d = x_prompt.shape
    dec_batch = x_sample.shape[0]
    n_chunks = seq // CHUNK
    tp = batch * seq

    g1 = norm1_g[0][None, :]
    g2 = norm2_g[0][None, :]
    g3 = norm3_g[0][None, :]
    gm = mem_norm_g[0][None, :]
    gf = final_norm_g[None, :]
    lng = sgu_norm_g[0][None, :]
    gn = ret_gn_g[0][None, :]
    win, wout, wcq, wck, wcv, wco, wf1, wf2 = (w_in[0], w_out[0], w_cq[0], w_ck[0], w_cv[0], w_co[0],
                                                 w_ff1[0], w_ff2[0])
    ws = sgu_w_s[0]
    sb = sgu_b[0]

    tm = 1024
    cc_p, ss_p = _rope_tables(jnp.arange(seq, dtype=F32))
    xp = x_prompt.reshape(tp, d)
    proj_p = _in_proj(xp, g1, win, cc_p, ss_p, lng, tm=tm, rope_blocks=seq // tm, out_dtype=BF16)
    mix_p, state_p = _mix_prompt(proj_p, ws, sb.T, gn, batch=batch, n_chunks=n_chunks)

    mem = mem_prompt.reshape(batch * N_MEM, d)
    mk = _norm_matmul(mem, gm, wck, tm=batch * N_MEM, tn=1024, name="mem_k")
    mv = _norm_matmul(mem, gm, wcv, tm=batch * N_MEM, tn=1024, name="mem_v")

    tr = 512
    h1_p, q_p = _out_q(mix_p, xp, wout, wcq, g2, tm=tr, q_dtype=BF16)
    h2_p, xn3_p = _attn_co(q_p, (mk, mv), h1_p, wco, g3, tm=tr, tiles_per_batch=seq // tr)
    y_p = _mlp_tail(h2_p, xn3_p, wf1, wf2, gf, tm=2048)

    ts = dec_batch
    cc_s, ss_s = _rope_tables(jnp.full((ts,), PAST_LEN, dtype=F32))
    xs = x_sample.reshape(ts, d)
    proj_s = _in_proj(xs, g1, win, cc_s, ss_s, lng, tm=ts, rope_blocks=1, out_dtype=F32)
    v_rows = proj_s[:, SEC:2 * SEC]
    w0 = jnp.repeat(ws[:, 0, 0], CH_A)[None, :]
    b0 = jnp.repeat(sb[:, 0], CH_A)[None, :]
    mix_s, state_s = _mix_sample(proj_s, w0, b0, gn, state_ret[0])
    h1_s, q_s = _out_q(mix_s.astype(BF16), xs, wout, wcq, g2, tm=ts, q_dtype=F32)
    att_s = _attn_sample(q_s.reshape(ts, H_X, DH_X), cache_mem_k, cache_mem_v).reshape(ts, d)
    h2_s, xn3_s = _attn_co(att_s.astype(BF16), None, h1_s, wco, g3, tm=ts)
    y_s = _mlp_tail(h2_s, xn3_s, wf1, wf2, gf, tm=ts)

    return (y_p.reshape(batch, seq, d),
            y_s.reshape(dec_batch, 1, d),
            mk.reshape(1, batch, N_MEM, H_X, DH_X),
            mv.reshape(1, batch, N_MEM, H_X, DH_X),
            state_p[None],
            state_s[None],
            v_rows.reshape(1, dec_batch, 1, H_A, CH_A))
```

```python
import functools
import math

import jax
import jax.numpy as jnp
from jax import lax
from jax.experimental import pallas as pl
from jax.experimental.pallas import tpu as pltpu

F32 = jnp.float32
BF16 = jnp.bfloat16

D_MODEL = 2048
H_A = 8
CH_A = 128
W_A = H_A * CH_A
CHUNK = 128
H_B = 8
DK_B = 128
DV_B = 128
W_B = H_B * DV_B
ROPE_THETA = 10000.0
N_MEM = 256
H_X = 4
DH_X = D_MODEL // H_X
D_FF = 4 * D_MODEL
EPS = 1e-6
PAST_LEN = 16384

SEC = 1024
N_SEC = 6
LOG_G = tuple(math.log(1.0 - 2.0 ** (-5.0 - h)) for h in range(H_B))

V7X_VMEM_BYTES = 64 * 1024 * 1024
VMEM_LIMIT = V7X_VMEM_BYTES - 4 * 1024 * 1024
ROW_CHUNK = 256


def _params(sem):
    return pltpu.CompilerParams(dimension_semantics=sem, vmem_limit_bytes=VMEM_LIMIT)


def _resident(shape):
    zeros = (0,) * len(shape)
    return pl.BlockSpec(shape, lambda *_: zeros, pipeline_mode=pl.Buffered(1))


def _whole_out(shape):
    zeros = (0,) * len(shape)
    return pl.BlockSpec(shape, lambda *_: zeros)


def _row_loop(n_rows, fn):
    rc = min(n_rows, ROW_CHUNK)
    assert n_rows % rc == 0

    def body(c, carry):
        fn(pl.ds(pl.multiple_of(c * rc, rc), rc))
        return carry

    lax.fori_loop(0, n_rows // rc, body, 0)


def _rms(h, g_ref):
    ms = jnp.mean(h * h, axis=-1, keepdims=True)
    return h * lax.rsqrt(ms + EPS) * g_ref[...]


def _rms_rows(x_ref, g_ref, xn_ref, n_rows):
    def fn(rows):
        xn_ref[rows, :] = _rms(x_ref[rows, :], g_ref).astype(BF16)

    _row_loop(n_rows, fn)


def _inproj_kernel(x_ref, g_ref, w_ref, cc_ref, ss_ref, lng_ref, xs_ref, ccs_ref, sss_ref,
                   o_ref, os_ref, xn_ref, xsn_ref, *, tm, ts):
    i = pl.program_id(0)
    j = pl.program_id(1)

    @pl.when(j == 0)
    def _():
        _rms_rows(x_ref, g_ref, xn_ref, tm)

    @pl.when((j == 0) & (i == 0))
    def _():
        _rms_rows(xs_ref, g_ref, xsn_ref, ts)

    def layernorm(acc, cc, ss):
        z = jax.nn.gelu(acc)
        mu = jnp.mean(z, axis=-1, keepdims=True)
        zc = z - mu
        var = jnp.mean(zc * zc, axis=-1, keepdims=True)
        return zc * lax.rsqrt(var + EPS) * lng_ref[...]

    def rotary(acc, cc, ss):
        scale = jnp.where(j == 3, DK_B ** -0.5, 1.0).astype(F32)
        heads = []
        for h in range(H_B):
            blk = acc[:, h * DK_B:(h + 1) * DK_B]
            rot = pltpu.roll(blk, DK_B // 2, 1)
            heads.append((blk * cc + rot * ss) * scale)
        return jnp.concatenate(heads, axis=-1)

    def section(pred, epilogue):
        @pl.when(pred)
        def _():
            wb = w_ref[...].astype(BF16)
            acc = jnp.dot(xn_ref[...], wb, preferred_element_type=F32)
            o_ref[...] = epilogue(acc, cc_ref[...], ss_ref[...]).astype(o_ref.dtype)

            @pl.when(i == 0)
            def _():
                acc_s = jnp.dot(xsn_ref[...], wb, preferred_element_type=F32)
                os_ref[j] = epilogue(acc_s, ccs_ref[...], sss_ref[...])

    section(j == 0, lambda acc, cc, ss: jax.nn.gelu(acc))
    section(j == 1, layernorm)
    section((j == 2) | (j == 3), rotary)
    section(j == 4, lambda acc, cc, ss: acc)
    section(j == 5, lambda acc, cc, ss: jax.nn.silu(acc))


def _in_proj(x, xs, g, w, rope_p, rope_s, lng, *, tm):
    t = x.shape[0]
    ts = xs.shape[0]
    rope_blocks = rope_p[0].shape[0] // tm
    rope_spec = pl.BlockSpec((tm, DK_B), lambda i, j: (i % rope_blocks, 0))
    return pl.pallas_call(
        functools.partial(_inproj_kernel, tm=tm, ts=ts),
        grid=(t // tm, N_SEC),
        in_specs=[
            pl.BlockSpec((tm, D_MODEL), lambda i, j: (i, 0)),
            _resident((1, D_MODEL)),
            pl.BlockSpec((D_MODEL, SEC), lambda i, j: (0, j)),
            rope_spec, rope_spec,
            _resident((1, SEC)),
            _resident((ts, D_MODEL)), _resident((ts, DK_B)), _resident((ts, DK_B)),
        ],
        out_specs=[pl.BlockSpec((tm, SEC), lambda i, j: (i, j)), _whole_out((N_SEC, ts, SEC))],
        out_shape=[jax.ShapeDtypeStruct((t, N_SEC * SEC), BF16),
                   jax.ShapeDtypeStruct((N_SEC, ts, SEC), F32)],
        scratch_shapes=[pltpu.VMEM((tm, D_MODEL), BF16), pltpu.VMEM((ts, D_MODEL), BF16)],
        compiler_params=_params(("arbitrary", "arbitrary")),
        name="in_proj",
    )(x, g, w, rope_p[0], rope_p[1], lng, xs, rope_s[0], rope_s[1])


def _mem_kernel(x_ref, g_ref, w_ref, o5_ref, o2_ref, wb_ref):
    @pl.when(pl.program_id(0) == 0)
    def _():
        def fn(rows):
            wb_ref[rows, :] = w_ref[rows, :].astype(BF16)
        _row_loop(D_MODEL, fn)

    res = jnp.dot(_rms(x_ref[...], g_ref).astype(BF16), wb_ref[...], preferred_element_type=F32)
    o2_ref[...] = res.astype(BF16)
    for h in range(H_X):
        o5_ref[0, 0, :, h, :] = res[:, h * DH_X:(h + 1) * DH_X]


def _mem_proj(mem, g, w, *, batch, name):
    d = mem.shape[1]
    return pl.pallas_call(
        _mem_kernel,
        grid=(batch,),
        in_specs=[pl.BlockSpec((N_MEM, d), lambda b: (b, 0)), _resident((1, d)), _resident((d, d))],
        out_specs=[pl.BlockSpec((1, 1, N_MEM, H_X, DH_X), lambda b: (0, b, 0, 0, 0)),
                   pl.BlockSpec((N_MEM, d), lambda b: (b, 0))],
        out_shape=[jax.ShapeDtypeStruct((1, batch, N_MEM, H_X, DH_X), F32),
                   jax.ShapeDtypeStruct((batch * N_MEM, d), BF16)],
        scratch_shapes=[pltpu.VMEM((d, d), BF16)],
        compiler_params=_params(("arbitrary",)),
        name=name,
    )(mem, g, w)


def _relu2(acc):
    return jnp.square(jnp.maximum(acc, 0.0))


def _ff1_kernel(x_ref, w_ref, xs_ref, o_ref, os_ref):
    wb = w_ref[...].astype(BF16)
    o_ref[...] = _relu2(jnp.dot(x_ref[...], wb, preferred_element_type=F32)).astype(BF16)

    @pl.when(pl.program_id(0) == 0)
    def _():
        os_ref[pl.program_id(1)] = _relu2(jnp.dot(xs_ref[...], wb, preferred_element_type=F32)).astype(BF16)


def _ff1(x, xs, w, *, tm, tn):
    t, k = x.shape
    ts = xs.shape[0]
    n = w.shape[1]
    return pl.pallas_call(
        _ff1_kernel,
        grid=(t // tm, n // tn),
        in_specs=[pl.BlockSpec((tm, k), lambda i, j: (i, 0)),
                  pl.BlockSpec((k, tn), lambda i, j: (0, j)),
                  _resident((ts, k))],
        out_specs=[pl.BlockSpec((tm, tn), lambda i, j: (i, j)), _whole_out((n // tn, ts, tn))],
        out_shape=[jax.ShapeDtypeStruct((t, n), BF16), jax.ShapeDtypeStruct((n // tn, ts, tn), BF16)],
        compiler_params=_params(("arbitrary", "arbitrary")),
        name="ff1",
    )(x, w, xs)


def _ff2_kernel(x_ref, w_ref, xs_ref, o_ref, os_ref):
    i, j, k = pl.program_id(0), pl.program_id(1), pl.program_id(2)
    wb = w_ref[...].astype(BF16)

    @pl.when(k == 0)
    def _():
        o_ref[...] = jnp.zeros_like(o_ref)

    o_ref[...] += jnp.dot(x_ref[...], wb, preferred_element_type=F32)

    @pl.when(i == 0)
    def _():
        part = jnp.dot(xs_ref[k], wb, preferred_element_type=F32)

        @pl.when(k == 0)
        def _():
            os_ref[j] = part

        @pl.when(k > 0)
        def _():
            os_ref[j] += part


def _ff2(x, xs, w, *, tm, tn, tk):
    t, k = x.shape
    ts = xs.shape[1]
    n = w.shape[1]
    return pl.pallas_call(
        _ff2_kernel,
        grid=(t // tm, n // tn, k // tk),
        in_specs=[pl.BlockSpec((tm, tk), lambda i, j, kk: (i, kk)),
                  pl.BlockSpec((tk, tn), lambda i, j, kk: (kk, j)),
                  _resident((k // tk, ts, tk))],
        out_specs=[pl.BlockSpec((tm, tn), lambda i, j, kk: (i, j)), _whole_out((n // tn, ts, tn))],
        out_shape=[jax.ShapeDtypeStruct((t, n), F32), jax.ShapeDtypeStruct((n // tn, ts, tn), F32)],
        compiler_params=_params(("arbitrary", "arbitrary", "arbitrary")),
        name="ff2",
    )(x, w, xs)


def _add_norm_kernel(x_ref, r_ref, g_ref, o_ref, *, tm):
    def fn(rows):
        o_ref[rows, :] = _rms(r_ref[rows, :] + x_ref[rows, :], g_ref)

    _row_loop(tm, fn)


def _add_norm(x, r, g, *, tm):
    t, d = x.shape
    row = pl.BlockSpec((tm, d), lambda i: (i, 0))
    return pl.pallas_call(
        functools.partial(_add_norm_kernel, tm=tm),
        grid=(t // tm,),
        in_specs=[row, row, _resident((1, d))],
        out_specs=row,
        out_shape=jax.ShapeDtypeStruct((t, d), F32),
        compiler_params=_params(("arbitrary",)),
        name="final_norm",
    )(x, r, g)


W_CHUNK = 256
N_LOAD = D_MODEL // W_CHUNK


def _load_weight_chunk(s, pairs):
    rows = pl.ds(pl.multiple_of(s * W_CHUNK, W_CHUNK), W_CHUNK)
    for src, dst in pairs:
        dst[rows, :] = src[...].astype(BF16)


def _w_chunk_spec():
    return pl.BlockSpec((W_CHUNK, D_MODEL), lambda s: (jnp.minimum(s, N_LOAD - 1), 0))


def _tile_spec(tm, width=D_MODEL):
    return pl.BlockSpec((tm, width), lambda s: (jnp.maximum(s - N_LOAD, 0), 0))


def _outq_kernel(mix_ref, x_ref, wo_ref, wq_ref, g2_ref, mixs_ref, xs_ref,
                 h1_ref, q_ref, h1s_ref, qs_ref, wo_bf, wq_bf):
    s = pl.program_id(0)

    def block(mix_r, x_r, h1_r, q_r):
        h1 = x_r[...] + jnp.dot(mix_r[...], wo_bf[...], preferred_element_type=F32)
        h1_r[...] = h1
        q_r[...] = jnp.dot(_rms(h1, g2_ref).astype(BF16), wq_bf[...],
                           preferred_element_type=F32).astype(q_r.dtype)

    @pl.when(s < N_LOAD)
    def _():
        _load_weight_chunk(s, ((wo_ref, wo_bf), (wq_ref, wq_bf)))

    @pl.when(s >= N_LOAD)
    def _():
        block(mix_ref, x_ref, h1_ref, q_ref)

    @pl.when(s == N_LOAD)
    def _():
        block(mixs_ref, xs_ref, h1s_ref, qs_ref)


def _out_q(mix, x, mix_s, xs, w_out, w_cq, g2, *, tm):
    t, d = x.shape
    ts = xs.shape[0]
    return pl.pallas_call(
        _outq_kernel,
        grid=(N_LOAD + t // tm,),
        in_specs=[_tile_spec(tm), _tile_spec(tm), _w_chunk_spec(), _w_chunk_spec(), _resident((1, d)),
                  _resident((ts, d)), _resident((ts, d))],
        out_specs=[_tile_spec(tm), _tile_spec(tm), _whole_out((ts, d)), _whole_out((ts, d))],
        out_shape=[jax.ShapeDtypeStruct((t, d), F32), jax.ShapeDtypeStruct((t, d), BF16),
                   jax.ShapeDtypeStruct((ts, d), F32), jax.ShapeDtypeStruct((ts, d), F32)],
        scratch_shapes=[pltpu.VMEM((d, d), BF16), pltpu.VMEM((d, d), BF16)],
        compiler_params=_params(("arbitrary",)),
        name="out_q",
    )(mix, x, w_out, w_cq, g2, mix_s, xs)


def _softmax_rows(s):
    m = jnp.max(s, axis=-1, keepdims=True)
    e = jnp.exp(s - m)
    return e / jnp.sum(e, axis=-1, keepdims=True)


def _attn_co_kernel(q_ref, mk_ref, mv_ref, h1_ref, wc_ref, g3_ref, atts_ref, h1s_ref,
                    h2_ref, xn_ref, h2s_ref, xns_ref, wc_bf, att_ref):
    s = pl.program_id(0)

    def co_block(att_r, h1_r, h2_r, xn_r):
        h2 = h1_r[...] + jnp.dot(att_r[...], wc_bf[...], preferred_element_type=F32)
        h2_r[...] = h2
        xn_r[...] = _rms(h2, g3_ref).astype(BF16)

    @pl.when(s < N_LOAD)
    def _():
        _load_weight_chunk(s, ((wc_ref, wc_bf),))

    @pl.when(s >= N_LOAD)
    def _():
        for h in range(H_X):
            cols = slice(h * DH_X, (h + 1) * DH_X)
            sc = lax.dot_general(q_ref[:, cols], mk_ref[:, cols], (((1,), (1,)), ((), ())),
                                 preferred_element_type=F32) * (DH_X ** -0.5)
            p = _softmax_rows(sc)
            att_ref[:, cols] = jnp.dot(p.astype(BF16), mv_ref[:, cols],
                                       preferred_element_type=F32).astype(BF16)
        co_block(att_ref, h1_ref, h2_ref, xn_ref)

    @pl.when(s == N_LOAD)
    def _():
        co_block(atts_ref, h1s_ref, h2s_ref, xns_ref)


def _attn_co(q, mk, mv, h1, att_s, h1_s, w_co, g3, *, tm, tiles_per_batch):
    t, d = h1.shape
    ts = h1_s.shape[0]
    mem_spec = pl.BlockSpec((N_MEM, d), lambda s: (jnp.maximum(s - N_LOAD, 0) // tiles_per_batch, 0))
    return pl.pallas_call(
        _attn_co_kernel,
        grid=(N_LOAD + t // tm,),
        in_specs=[_tile_spec(tm), mem_spec, mem_spec, _tile_spec(tm), _w_chunk_spec(), _resident((1, d)),
                  _resident((ts, d)), _resident((ts, d))],
        out_specs=[_tile_spec(tm), _tile_spec(tm), _whole_out((ts, d)), _whole_out((ts, d))],
        out_shape=[jax.ShapeDtypeStruct((t, d), F32), jax.ShapeDtypeStruct((t, d), BF16),
                   jax.ShapeDtypeStruct((ts, d), F32), jax.ShapeDtypeStruct((ts, d), BF16)],
        scratch_shapes=[pltpu.VMEM((d, d), BF16), pltpu.VMEM((tm, d), BF16)],
        compiler_params=_params(("arbitrary",)),
        name="attn_co",
    )(q, mk, mv, h1, w_co, g3, att_s, h1_s)


def _mix_prompt_kernel(u_ref, v_ref, q_ref, k_ref, vb_ref, g_ref, ws_ref, bt_ref, gn_ref,
                       o_ref, so_ref, st_ref, wt_ref, bias_ref, dm_ref, qd_ref, kd_ref, *, n_chunks):
    b = pl.program_id(0)
    n = pl.program_id(1)

    @pl.when((b == 0) & (n == 0))
    def _():
        ii = lax.broadcasted_iota(jnp.int32, (CHUNK, CHUNK), 0)
        jj = lax.broadcasted_iota(jnp.int32, (CHUNK, CHUNK), 1)
        causal = ii >= jj
        diff = jnp.maximum((ii - jj).astype(F32), 0.0)
        ridx = ii.astype(F32)
        for h in range(H_A):
            lg = LOG_G[h]
            wt_ref[h] = (ws_ref[h] * causal.astype(F32)).astype(BF16)
            bias_ref[h] = jnp.broadcast_to(bt_ref[:, h:h + 1], (CHUNK, CH_A))
            dm_ref[h] = jnp.where(causal, jnp.exp(lg * diff), 0.0)
            qd_ref[h] = jnp.exp(lg * (ridx + 1.0))
            kd_ref[h] = jnp.exp(lg * (CHUNK - 1.0 - ridx))

    @pl.when(n == 0)
    def _():
        st_ref[...] = jnp.zeros_like(st_ref)

    for h in range(H_A):
        cols = slice(h * CH_A, (h + 1) * CH_A)
        mixed = jnp.dot(wt_ref[h], v_ref[:, cols], preferred_element_type=F32) + bias_ref[h]
        o_ref[:, cols] = (u_ref[:, cols].astype(F32) * mixed).astype(BF16)

        qh = q_ref[:, cols]
        kh = k_ref[:, cols]
        vh = vb_ref[:, cols]
        scores = lax.dot_general(qh, kh, (((1,), (1,)), ((), ())), preferred_element_type=F32) * dm_ref[h]
        intra = jnp.dot(scores.astype(BF16), vh, preferred_element_type=F32)
        state = st_ref[h]
        cross = jnp.dot(qh, state.astype(BF16), preferred_element_type=F32) * qd_ref[h]
        kd = (kh.astype(F32) * kd_ref[h]).astype(BF16)
        st_ref[h] = state * math.exp(LOG_G[h] * CHUNK) + lax.dot_general(
            kd, vh, (((0,), (0,)), ((), ())), preferred_element_type=F32)
        ret = intra + cross
        mu = jnp.mean(ret, axis=-1, keepdims=True)
        rc = ret - mu
        var = jnp.mean(rc * rc, axis=-1, keepdims=True)
        normed = rc * lax.rsqrt(var + EPS) * gn_ref[:, cols]
        o_ref[:, W_A + h * DV_B:W_A + (h + 1) * DV_B] = (g_ref[:, cols].astype(F32) * normed).astype(BF16)

    @pl.when(n == n_chunks - 1)
    def _():
        so_ref[0] = st_ref[...]


def _mix_prompt(proj, ws, bt, gn, *, batch, n_chunks):
    t = proj.shape[0]

    def sec(s):
        return pl.BlockSpec((CHUNK, SEC), lambda b, n, s=s: (b * n_chunks + n, s))

    table = pltpu.VMEM((H_B, CHUNK, CHUNK), F32)
    return pl.pallas_call(
        functools.partial(_mix_prompt_kernel, n_chunks=n_chunks),
        grid=(batch, n_chunks),
        in_specs=[sec(0), sec(1), sec(2), sec(3), sec(4), sec(5),
                  _resident((H_A, CHUNK, CHUNK)), _resident((CHUNK, H_A)), _resident((1, W_B))],
        out_specs=[pl.BlockSpec((CHUNK, W_A + W_B), lambda b, n: (b * n_chunks + n, 0)),
                   pl.BlockSpec((1, H_B, DK_B, DV_B), lambda b, n: (b, 0, 0, 0))],
        out_shape=[jax.ShapeDtypeStruct((t, W_A + W_B), BF16),
                   jax.ShapeDtypeStruct((batch, H_B, DK_B, DV_B), F32)],
        scratch_shapes=[pltpu.VMEM((H_B, DK_B, DV_B), F32), pltpu.VMEM((H_A, CHUNK, CHUNK), BF16),
                        table, table, table, table],
        compiler_params=_params(("arbitrary", "arbitrary")),
        name="mix_prompt",
    )(proj, proj, proj, proj, proj, proj, ws, bt, gn)


SAMPLE_ROWS = 8


def _mix_sample_kernel(u_ref, v_ref, q_ref, k_ref, vb_ref, g_ref, w0_ref, b0_ref, gn_ref, s_ref,
                       o_ref, so_ref):
    rid = lax.broadcasted_iota(jnp.int32, (SAMPLE_ROWS, 1), 0)
    o_ref[:, :W_A] = u_ref[...] * (w0_ref[...] * v_ref[...] + b0_ref[...])

    for h in range(H_B):
        cols = slice(h * DK_B, (h + 1) * DK_B)
        g_h = math.exp(LOG_G[h])
        q_blk = q_ref[:, cols]
        k_blk = k_ref[:, cols]
        v_blk = vb_ref[:, cols]
        v_bf = v_blk.astype(BF16)
        intra = jnp.sum(q_blk * k_blk, axis=-1, keepdims=True) * v_blk
        cross = jnp.zeros((SAMPLE_ROWS, DV_B), F32)
        for r in range(SAMPLE_ROWS):
            state = s_ref[r, h]
            q_only_r = jnp.where(rid == r, q_blk, 0.0).astype(BF16)
            cross = cross + jnp.dot(q_only_r, state.astype(BF16), preferred_element_type=F32)
            k_only_r = jnp.where(rid == r, k_blk, 0.0).astype(BF16)
            outer = lax.dot_general(k_only_r, v_bf, (((0,), (0,)), ((), ())), preferred_element_type=F32)
            so_ref[r, h] = state * g_h + outer
        ret = intra + cross * g_h
        mu = jnp.mean(ret, axis=-1, keepdims=True)
        rc = ret - mu
        var = jnp.mean(rc * rc, axis=-1, keepdims=True)
        normed = rc * lax.rsqrt(var + EPS) * gn_ref[:, cols]
        o_ref[:, W_A + h * DV_B:W_A + (h + 1) * DV_B] = g_ref[:, cols] * normed


def _mix_sample(proj, w0, b0, gn, state):
    t = proj.shape[1]
    rb = SAMPLE_ROWS

    def sec(s):
        return pl.BlockSpec((None, rb, SEC), lambda i, s=s: (s, i, 0))

    return pl.pallas_call(
        _mix_sample_kernel,
        grid=(t // rb,),
        in_specs=[sec(0), sec(1), sec(2), sec(3), sec(4), sec(5),
                  _resident((1, W_A)), _resident((1, W_A)), _resident((1, W_B)),
                  pl.BlockSpec((rb, H_B, DK_B, DV_B), lambda i: (i, 0, 0, 0))],
        out_specs=[pl.BlockSpec((rb, W_A + W_B), lambda i: (i, 0)),
                   pl.BlockSpec((rb, H_B, DK_B, DV_B), lambda i: (i, 0, 0, 0))],
        out_shape=[jax.ShapeDtypeStruct((t, W_A + W_B), F32),
                   jax.ShapeDtypeStruct(state.shape, F32)],
        compiler_params=_params(("arbitrary",)),
        name="mix_sample",
    )(proj, proj, proj, proj, proj, proj, w0, b0, gn, state)


ATTN_SAMPLE_ROWS = 2


def _attn_sample_kernel(q_ref, ck_ref, cv_ref, o_ref):
    i = pl.program_id(0)
    for r in range(ATTN_SAMPLE_ROWS):
        row = i * ATTN_SAMPLE_ROWS + r
        q = q_ref[row]
        s = jnp.sum(ck_ref[0, r] * q[None, :, :], axis=-1) * (DH_X ** -0.5)
        m = jnp.max(s, axis=0, keepdims=True)
        e = jnp.exp(s - m)
        p = e / jnp.sum(e, axis=0, keepdims=True)
        o_ref[row] = jnp.sum(p[:, :, None] * cv_ref[0, r], axis=0)


def _attn_sample(q, ck, cv):
    t = q.shape[0]
    rb = ATTN_SAMPLE_ROWS
    cache_spec = pl.BlockSpec((1, rb, N_MEM, H_X, DH_X), lambda i: (0, i, 0, 0, 0))
    return pl.pallas_call(
        _attn_sample_kernel,
        grid=(t // rb,),
        in_specs=[_resident((t, H_X, DH_X)), cache_spec, cache_spec],
        out_specs=_whole_out((t, H_X, DH_X)),
        out_shape=jax.ShapeDtypeStruct((t, H_X, DH_X), F32),
        compiler_params=_params(("arbitrary",)),
        name="attn_sample",
    )(q, ck, cv)


def _rope_tables(pos):
    half = DK_B // 2
    freqs = ROPE_THETA ** (-jnp.arange(half, dtype=F32) / half)
    ang = pos[:, None] * freqs[None, :]
    cos = jnp.cos(ang)
    sin = jnp.sin(ang)
    return jnp.concatenate([cos, cos], axis=-1), jnp.concatenate([-sin, sin], axis=-1)


def kernel(x_prompt, x_sample, mem_prompt, cache_mem_k, cache_mem_v, state_ret, norm1_g, w_in, sgu_norm_g, sgu_w_s, sgu_b, ret_gn_g, w_out, norm2_g, mem_norm_g, w_cq, w_ck, w_cv, w_co, norm3_g, w_ff1, w_ff2, final_norm_g):
    batch, seq, d = x_prompt.shape
    ts = x_sample.shape[0]
    n_chunks = seq // CHUNK
    tp = batch * seq

    g1 = norm1_g[0][None, :]
    g2 = norm2_g[0][None, :]
    g3 = norm3_g[0][None, :]
    gm = mem_norm_g[0][None, :]
    gf = final_norm_g[None, :]
    lng = sgu_norm_g[0][None, :]
    gn = ret_gn_g[0][None, :]
    ws = sgu_w_s[0]
    sb = sgu_b[0]
    xp = x_prompt.reshape(tp, d)
    xs = x_sample.reshape(ts, d)

    rope_p = _rope_tables(jnp.arange(seq, dtype=F32))
    rope_s = _rope_tables(jnp.full((ts,), PAST_LEN, dtype=F32))
    proj_p, proj_s = _in_proj(xp, xs, g1, w_in[0], rope_p, rope_s, lng, tm=1024)
    mix_p, state_p = _mix_prompt(proj_p, ws, sb.T, gn, batch=batch, n_chunks=n_chunks)
    w0 = jnp.repeat(ws[:, 0, 0], CH_A)[None, :]
    b0 = jnp.repeat(sb[:, 0], CH_A)[None, :]
    mix_s, state_s = _mix_sample(proj_s, w0, b0, gn, state_ret[0])

    mem = mem_prompt.reshape(batch * N_MEM, d)
    mk5, mk = _mem_proj(mem, gm, w_ck[0], batch=batch, name="mem_k")
    mv5, mv = _mem_proj(mem, gm, w_cv[0], batch=batch, name="mem_v")
    tr = 512
    h1_p, q_p, h1_s, q_s = _out_q(mix_p, xp, mix_s.astype(BF16), xs, w_out[0], w_cq[0], g2, tm=tr)
    att_s = _attn_sample(q_s.reshape(ts, H_X, DH_X), cache_mem_k, cache_mem_v).reshape(ts, d)
    h2_p, xn3_p, h2_s, xn3_s = _attn_co(q_p, mk, mv, h1_p, att_s.astype(BF16), h1_s, w_co[0], g3,
                                        tm=tr, tiles_per_batch=seq // tr)

    hid_p, hid_s = _ff1(xn3_p, xn3_s, w_ff1[0], tm=2048, tn=1024)
    part_p, part_s = _ff2(hid_p, hid_s, w_ff2[0], tm=2048, tn=1024, tk=1024)
    y_p = _add_norm(part_p, h2_p, gf, tm=512)
    y_s = _add_norm(part_s.transpose(1, 0, 2).reshape(ts, d), h2_s, gf, tm=ts)

    return (y_p.reshape(batch, seq, d),
            y_s.reshape(ts, 1, d),
            mk5,
            mv5,
            state_p[None],
            state_s[None],
            proj_s[1].reshape(1, ts, 1, H_A, CH_A))
```

```python
import functools
import math

import jax
import jax.numpy as jnp
from jax import lax
from jax.experimental import pallas as pl
from jax.experimental.pallas import tpu as pltpu

F32 = jnp.float32
BF16 = jnp.bfloat16

D_MODEL = 2048
H_A = 8
CH_A = 128
W_A = H_A * CH_A
CHUNK = 128
H_B = 8
DK_B = 128
DV_B = 128
W_B = H_B * DV_B
ROPE_THETA = 10000.0
N_MEM = 256
H_X = 4
DH_X = D_MODEL // H_X
D_FF = 4 * D_MODEL
EPS = 1e-6
PAST_LEN = 16384

SEC = 1024
N_SEC = 6
LOG_G = tuple(math.log(1.0 - 2.0 ** (-5.0 - h)) for h in range(H_B))

V7X_VMEM_BYTES = 64 * 1024 * 1024
VMEM_LIMIT = V7X_VMEM_BYTES - 4 * 1024 * 1024
ROW_CHUNK = 256
ROW_SPLIT = 4


def _params(sem):
    return pltpu.CompilerParams(dimension_semantics=sem, vmem_limit_bytes=VMEM_LIMIT)


def _resident(shape):
    zeros = (0,) * len(shape)
    return pl.BlockSpec(shape, lambda *_: zeros, pipeline_mode=pl.Buffered(1))


def _whole_out(shape):
    zeros = (0,) * len(shape)
    return pl.BlockSpec(shape, lambda *_: zeros)


def _row_loop(n_rows, fn):
    rc = min(n_rows, ROW_CHUNK)
    assert n_rows % rc == 0

    def body(c, carry):
        fn(pl.ds(pl.multiple_of(c * rc, rc), rc))
        return carry

    lax.fori_loop(0, n_rows // rc, body, 0)


def _rms(h, g_ref):
    ms = jnp.mean(h * h, axis=-1, keepdims=True)
    return h * lax.rsqrt(ms + EPS) * g_ref[...]


def _rms_rows(x_ref, g_ref, xn_ref, n_rows):
    def fn(rows):
        xn_ref[rows, :] = _rms(x_ref[rows, :], g_ref).astype(BF16)

    _row_loop(n_rows, fn)


def _inproj_kernel(x_ref, g_ref, w_ref, cc_ref, ss_ref, lng_ref, xs_ref, ccs_ref, sss_ref,
                   o_ref, os_ref, xn_ref, xsn_ref, *, tm, ts):
    i = pl.program_id(0)
    j = pl.program_id(1)

    @pl.when(j == 0)
    def _():
        _rms_rows(x_ref, g_ref, xn_ref, tm)

    @pl.when((j == 0) & (i == 0))
    def _():
        _rms_rows(xs_ref, g_ref, xsn_ref, ts)

    def layernorm(acc, cc, ss):
        z = jax.nn.gelu(acc)
        mu = jnp.mean(z, axis=-1, keepdims=True)
        zc = z - mu
        var = jnp.mean(zc * zc, axis=-1, keepdims=True)
        return zc * lax.rsqrt(var + EPS) * lng_ref[...]

    def rotary(acc, cc, ss):
        scale = jnp.where(j == 3, DK_B ** -0.5, 1.0).astype(F32)
        heads = []
        for h in range(H_B):
            blk = acc[:, h * DK_B:(h + 1) * DK_B]
            rot = pltpu.roll(blk, DK_B // 2, 1)
            heads.append((blk * cc + rot * ss) * scale)
        return jnp.concatenate(heads, axis=-1)

    def section(pred, epilogue):
        @pl.when(pred)
        def _():
            wb = w_ref[...].astype(BF16)
            for r0 in range(0, tm, tm // ROW_SPLIT):
                rows = slice(r0, r0 + tm // ROW_SPLIT)
                acc = jnp.dot(xn_ref[rows, :], wb, preferred_element_type=F32)
                o_ref[rows, :] = epilogue(acc, cc_ref[rows, :], ss_ref[rows, :]).astype(o_ref.dtype)

            @pl.when(i == 0)
            def _():
                acc_s = jnp.dot(xsn_ref[...], w_ref[...].astype(BF16), preferred_element_type=F32)
                os_ref[j] = epilogue(acc_s, ccs_ref[...], sss_ref[...])

    section(j == 0, lambda acc, cc, ss: jax.nn.gelu(acc))
    section(j == 1, layernorm)
    section((j == 2) | (j == 3), rotary)
    section(j == 4, lambda acc, cc, ss: acc)
    section(j == 5, lambda acc, cc, ss: jax.nn.silu(acc))


def _in_proj(x, xs, g, w, rope_p, rope_s, lng, *, tm):
    t = x.shape[0]
    ts = xs.shape[0]
    rope_blocks = rope_p[0].shape[0] // tm
    rope_spec = pl.BlockSpec((tm, DK_B), lambda i, j: (i % rope_blocks, 0))
    return pl.pallas_call(
        functools.partial(_inproj_kernel, tm=tm, ts=ts),
        grid=(t // tm, N_SEC),
        in_specs=[
            pl.BlockSpec((tm, D_MODEL), lambda i, j: (i, 0)),
            _resident((1, D_MODEL)),
            pl.BlockSpec((D_MODEL, SEC), lambda i, j: (0, j)),
            rope_spec, rope_spec,
            _resident((1, SEC)),
            _resident((ts, D_MODEL)), _resident((ts, DK_B)), _resident((ts, DK_B)),
        ],
        out_specs=[pl.BlockSpec((tm, SEC), lambda i, j: (i, j)), _whole_out((N_SEC, ts, SEC))],
        out_shape=[jax.ShapeDtypeStruct((t, N_SEC * SEC), BF16),
                   jax.ShapeDtypeStruct((N_SEC, ts, SEC), F32)],
        scratch_shapes=[pltpu.VMEM((tm, D_MODEL), BF16), pltpu.VMEM((ts, D_MODEL), BF16)],
        compiler_params=_params(("arbitrary", "arbitrary")),
        name="in_proj",
    )(x, g, w, rope_p[0], rope_p[1], lng, xs, rope_s[0], rope_s[1])


def _mem_kernel(x_ref, g_ref, w_ref, o5_ref, o2_ref, wb_ref):
    @pl.when(pl.program_id(0) == 0)
    def _():
        def fn(rows):
            wb_ref[rows, :] = w_ref[rows, :].astype(BF16)
        _row_loop(D_MODEL, fn)

    res = jnp.dot(_rms(x_ref[...], g_ref).astype(BF16), wb_ref[...], preferred_element_type=F32)
    o2_ref[...] = res.astype(BF16)
    for h in range(H_X):
        o5_ref[0, 0, :, h, :] = res[:, h * DH_X:(h + 1) * DH_X]


def _mem_proj(mem, g, w, *, batch, name):
    d = mem.shape[1]
    return pl.pallas_call(
        _mem_kernel,
        grid=(batch,),
        in_specs=[pl.BlockSpec((N_MEM, d), lambda b: (b, 0)), _resident((1, d)), _resident((d, d))],
        out_specs=[pl.BlockSpec((1, 1, N_MEM, H_X, DH_X), lambda b: (0, b, 0, 0, 0)),
                   pl.BlockSpec((N_MEM, d), lambda b: (b, 0))],
        out_shape=[jax.ShapeDtypeStruct((1, batch, N_MEM, H_X, DH_X), F32),
                   jax.ShapeDtypeStruct((batch * N_MEM, d), BF16)],
        scratch_shapes=[pltpu.VMEM((d, d), BF16)],
        compiler_params=_params(("arbitrary",)),
        name=name,
    )(mem, g, w)


def _relu2(acc):
    return jnp.square(jnp.maximum(acc, 0.0))


def _ff1_kernel(x_ref, w_ref, xs_ref, o_ref, os_ref):
    o_ref[...] = _relu2(jnp.dot(x_ref[...], w_ref[...].astype(BF16), preferred_element_type=F32)).astype(BF16)

    @pl.when(pl.program_id(0) == 0)
    def _():
        os_ref[pl.program_id(1)] = _relu2(jnp.dot(xs_ref[...], w_ref[...].astype(BF16),
                                                  preferred_element_type=F32)).astype(BF16)


def _ff1(x, xs, w, *, tm, tn):
    t, k = x.shape
    ts = xs.shape[0]
    n = w.shape[1]
    return pl.pallas_call(
        _ff1_kernel,
        grid=(t // tm, n // tn),
        in_specs=[pl.BlockSpec((tm, k), lambda i, j: (i, 0)),
                  pl.BlockSpec((k, tn), lambda i, j: (0, j)),
                  _resident((ts, k))],
        out_specs=[pl.BlockSpec((tm, tn), lambda i, j: (i, j)), _whole_out((n // tn, ts, tn))],
        out_shape=[jax.ShapeDtypeStruct((t, n), BF16), jax.ShapeDtypeStruct((n // tn, ts, tn), BF16)],
        compiler_params=_params(("arbitrary", "arbitrary")),
        name="ff1",
    )(x, w, xs)


def _ff2_kernel(x_ref, w_ref, xs_ref, o_ref, os_ref):
    i, j, k = pl.program_id(0), pl.program_id(1), pl.program_id(2)

    @pl.when(k == 0)
    def _():
        o_ref[...] = jnp.zeros_like(o_ref)

    o_ref[...] += jnp.dot(x_ref[...], w_ref[...].astype(BF16), preferred_element_type=F32)

    @pl.when(i == 0)
    def _():
        part = jnp.dot(xs_ref[k], w_ref[...].astype(BF16), preferred_element_type=F32)

        @pl.when(k == 0)
        def _():
            os_ref[j] = part

        @pl.when(k > 0)
        def _():
            os_ref[j] += part


def _ff2(x, xs, w, *, tm, tn, tk):
    t, k = x.shape
    ts = xs.shape[1]
    n = w.shape[1]
    return pl.pallas_call(
        _ff2_kernel,
        grid=(t // tm, n // tn, k // tk),
        in_specs=[pl.BlockSpec((tm, tk), lambda i, j, kk: (i, kk)),
                  pl.BlockSpec((tk, tn), lambda i, j, kk: (kk, j)),
                  _resident((k // tk, ts, tk))],
        out_specs=[pl.BlockSpec((tm, tn), lambda i, j, kk: (i, j)), _whole_out((n // tn, ts, tn))],
        out_shape=[jax.ShapeDtypeStruct((t, n), F32), jax.ShapeDtypeStruct((n // tn, ts, tn), F32)],
        compiler_params=_params(("arbitrary", "arbitrary", "arbitrary")),
        name="ff2",
    )(x, w, xs)


def _add_norm_kernel(x_ref, r_ref, g_ref, o_ref, *, tm):
    def fn(rows):
        o_ref[rows, :] = _rms(r_ref[rows, :] + x_ref[rows, :], g_ref)

    _row_loop(tm, fn)


def _add_norm(x, r, g, *, tm):
    t, d = x.shape
    row = pl.BlockSpec((tm, d), lambda i: (i, 0))
    return pl.pallas_call(
        functools.partial(_add_norm_kernel, tm=tm),
        grid=(t // tm,),
        in_specs=[row, row, _resident((1, d))],
        out_specs=row,
        out_shape=jax.ShapeDtypeStruct((t, d), F32),
        compiler_params=_params(("arbitrary",)),
        name="final_norm",
    )(x, r, g)


W_CHUNK = 256
N_LOAD = D_MODEL // W_CHUNK


def _load_weight_chunk(s, pairs):
    rows = pl.ds(pl.multiple_of(s * W_CHUNK, W_CHUNK), W_CHUNK)
    for src, dst in pairs:
        dst[rows, :] = src[...].astype(BF16)


def _w_chunk_spec():
    return pl.BlockSpec((W_CHUNK, D_MODEL), lambda s: (jnp.minimum(s, N_LOAD - 1), 0))


def _tile_spec(tm, width=D_MODEL):
    return pl.BlockSpec((tm, width), lambda s: (jnp.maximum(s - N_LOAD, 0), 0))


def _outq_kernel(mix_ref, x_ref, wo_ref, wq_ref, g2_ref, mixs_ref, xs_ref,
                 h1_ref, q_ref, h1s_ref, qs_ref, wo_bf, wq_bf):
    s = pl.program_id(0)

    def block(mix_r, x_r, h1_r, q_r):
        h1 = x_r[...] + jnp.dot(mix_r[...], wo_bf[...], preferred_element_type=F32)
        h1_r[...] = h1
        q_r[...] = jnp.dot(_rms(h1, g2_ref).astype(BF16), wq_bf[...],
                           preferred_element_type=F32).astype(q_r.dtype)

    @pl.when(s < N_LOAD)
    def _():
        _load_weight_chunk(s, ((wo_ref, wo_bf), (wq_ref, wq_bf)))

    @pl.when(s >= N_LOAD)
    def _():
        block(mix_ref, x_ref, h1_ref, q_ref)

    @pl.when(s == N_LOAD)
    def _():
        block(mixs_ref, xs_ref, h1s_ref, qs_ref)


def _out_q(mix, x, mix_s, xs, w_out, w_cq, g2, *, tm):
    t, d = x.shape
    ts = xs.shape[0]
    return pl.pallas_call(
        _outq_kernel,
        grid=(N_LOAD + t // tm,),
        in_specs=[_tile_spec(tm), _tile_spec(tm), _w_chunk_spec(), _w_chunk_spec(), _resident((1, d)),
                  _resident((ts, d)), _resident((ts, d))],
        out_specs=[_tile_spec(tm), _tile_spec(tm), _whole_out((ts, d)), _whole_out((ts, d))],
        out_shape=[jax.ShapeDtypeStruct((t, d), F32), jax.ShapeDtypeStruct((t, d), BF16),
                   jax.ShapeDtypeStruct((ts, d), F32), jax.ShapeDtypeStruct((ts, d), F32)],
        scratch_shapes=[pltpu.VMEM((d, d), BF16), pltpu.VMEM((d, d), BF16)],
        compiler_params=_params(("arbitrary",)),
        name="out_q",
    )(mix, x, w_out, w_cq, g2, mix_s, xs)


def _softmax_rows(s):
    m = jnp.max(s, axis=-1, keepdims=True)
    e = jnp.exp(s - m)
    return e / jnp.sum(e, axis=-1, keepdims=True)


def _attn_co_kernel(q_ref, mk_ref, mv_ref, h1_ref, wc_ref, g3_ref, atts_ref, h1s_ref,
                    h2_ref, xn_ref, h2s_ref, xns_ref, wc_bf, att_ref):
    s = pl.program_id(0)

    def co_block(att_r, h1_r, h2_r, xn_r):
        h2 = h1_r[...] + jnp.dot(att_r[...], wc_bf[...], preferred_element_type=F32)
        h2_r[...] = h2
        xn_r[...] = _rms(h2, g3_ref).astype(BF16)

    @pl.when(s < N_LOAD)
    def _():
        _load_weight_chunk(s, ((wc_ref, wc_bf),))

    @pl.when(s >= N_LOAD)
    def _():
        for h in range(H_X):
            cols = slice(h * DH_X, (h + 1) * DH_X)
            sc = lax.dot_general(q_ref[:, cols], mk_ref[:, cols], (((1,), (1,)), ((), ())),
                                 preferred_element_type=F32) * (DH_X ** -0.5)
            p = _softmax_rows(sc)
            att_ref[:, cols] = jnp.dot(p.astype(BF16), mv_ref[:, cols],
                                       preferred_element_type=F32).astype(BF16)
        co_block(att_ref, h1_ref, h2_ref, xn_ref)

    @pl.when(s == N_LOAD)
    def _():
        co_block(atts_ref, h1s_ref, h2s_ref, xns_ref)


def _attn_co(q, mk, mv, h1, att_s, h1_s, w_co, g3, *, tm, tiles_per_batch):
    t, d = h1.shape
    ts = h1_s.shape[0]
    mem_spec = pl.BlockSpec((N_MEM, d), lambda s: (jnp.maximum(s - N_LOAD, 0) // tiles_per_batch, 0))
    return pl.pallas_call(
        _attn_co_kernel,
        grid=(N_LOAD + t // tm,),
        in_specs=[_tile_spec(tm), mem_spec, mem_spec, _tile_spec(tm), _w_chunk_spec(), _resident((1, d)),
                  _resident((ts, d)), _resident((ts, d))],
        out_specs=[_tile_spec(tm), _tile_spec(tm), _whole_out((ts, d)), _whole_out((ts, d))],
        out_shape=[jax.ShapeDtypeStruct((t, d), F32), jax.ShapeDtypeStruct((t, d), BF16),
                   jax.ShapeDtypeStruct((ts, d), F32), jax.ShapeDtypeStruct((ts, d), BF16)],
        scratch_shapes=[pltpu.VMEM((d, d), BF16), pltpu.VMEM((tm, d), BF16)],
        compiler_params=_params(("arbitrary",)),
        name="attn_co",
    )(q, mk, mv, h1, w_co, g3, att_s, h1_s)


def _mix_prompt_kernel(u_ref, v_ref, q_ref, k_ref, vb_ref, g_ref, ws_ref, bt_ref, gn_ref,
                       o_ref, so_ref, st_ref, wt_ref, bias_ref, dm_ref, qd_ref, kd_ref, *, n_chunks):
    b = pl.program_id(0)
    n = pl.program_id(1)

    @pl.when((b == 0) & (n == 0))
    def _():
        ii = lax.broadcasted_iota(jnp.int32, (CHUNK, CHUNK), 0)
        jj = lax.broadcasted_iota(jnp.int32, (CHUNK, CHUNK), 1)
        causal = ii >= jj
        diff = jnp.maximum((ii - jj).astype(F32), 0.0)
        ridx = ii.astype(F32)
        for h in range(H_A):
            lg = LOG_G[h]
            wt_ref[h] = (ws_ref[h] * causal.astype(F32)).astype(BF16)
            bias_ref[h] = jnp.broadcast_to(bt_ref[:, h:h + 1], (CHUNK, CH_A))
            dm_ref[h] = jnp.where(causal, jnp.exp(lg * diff), 0.0)
            qd_ref[h] = jnp.exp(lg * (ridx + 1.0))
            kd_ref[h] = jnp.exp(lg * (CHUNK - 1.0 - ridx))

    @pl.when(n == 0)
    def _():
        st_ref[...] = jnp.zeros_like(st_ref)

    for h in range(H_A):
        cols = slice(h * CH_A, (h + 1) * CH_A)
        mixed = jnp.dot(wt_ref[h], v_ref[:, cols], preferred_element_type=F32) + bias_ref[h]
        o_ref[:, cols] = (u_ref[:, cols].astype(F32) * mixed).astype(BF16)

        kh = k_ref[:, cols]
        vh = vb_ref[:, cols]
        state = st_ref[h]
        state_bf = state.astype(BF16)
        qh = q_ref[:, cols]
        scores = lax.dot_general(qh, kh, (((1,), (1,)), ((), ())), preferred_element_type=F32) * dm_ref[h]
        intra = jnp.dot(scores.astype(BF16), vh, preferred_element_type=F32)
        cross = jnp.dot(qh, state_bf, preferred_element_type=F32) * qd_ref[h]
        ret = intra + cross
        mu = jnp.mean(ret, axis=-1, keepdims=True)
        rc = ret - mu
        var = jnp.mean(rc * rc, axis=-1, keepdims=True)
        normed = rc * lax.rsqrt(var + EPS) * gn_ref[:, cols]
        o_ref[:, W_A + h * DV_B:W_A + (h + 1) * DV_B] = (g_ref[:, cols].astype(F32) * normed).astype(BF16)
        kd = (kh.astype(F32) * kd_ref[h]).astype(BF16)
        st_ref[h] = state * math.exp(LOG_G[h] * CHUNK) + lax.dot_general(
            kd, vh, (((0,), (0,)), ((), ())), preferred_element_type=F32)

    @pl.when(n == n_chunks - 1)
    def _():
        so_ref[0] = st_ref[...]


def _mix_prompt(proj, ws, bt, gn, *, batch, n_chunks):
    t = proj.shape[0]

    def sec(s):
        return pl.BlockSpec((CHUNK, SEC), lambda b, n, s=s: (b * n_chunks + n, s))

    table = pltpu.VMEM((H_B, CHUNK, CHUNK), F32)
    return pl.pallas_call(
        functools.partial(_mix_prompt_kernel, n_chunks=n_chunks),
        grid=(batch, n_chunks),
        in_specs=[sec(0), sec(1), sec(2), sec(3), sec(4), sec(5),
                  _resident((H_A, CHUNK, CHUNK)), _resident((CHUNK, H_A)), _resident((1, W_B))],
        out_specs=[pl.BlockSpec((CHUNK, W_A + W_B), lambda b, n: (b * n_chunks + n, 0)),
                   pl.BlockSpec((1, H_B, DK_B, DV_B), lambda b, n: (b, 0, 0, 0))],
        out_shape=[jax.ShapeDtypeStruct((t, W_A + W_B), BF16),
                   jax.ShapeDtypeStruct((batch, H_B, DK_B, DV_B), F32)],
        scratch_shapes=[pltpu.VMEM((H_B, DK_B, DV_B), F32), pltpu.VMEM((H_A, CHUNK, CHUNK), BF16),
                        table, table, table, table],
        compiler_params=_params(("arbitrary", "arbitrary")),
        name="mix_prompt",
    )(proj, proj, proj, proj, proj, proj, ws, bt, gn)


SAMPLE_ROWS = 8


def _mix_sample_kernel(u_ref, v_ref, q_ref, k_ref, vb_ref, g_ref, w0_ref, b0_ref, gn_ref, s_ref,
                       o_ref, so_ref):
    rid = lax.broadcasted_iota(jnp.int32, (SAMPLE_ROWS, 1), 0)
    o_ref[:, :W_A] = u_ref[...] * (w0_ref[...] * v_ref[...] + b0_ref[...])

    for h in range(H_B):
        cols = slice(h * DK_B, (h + 1) * DK_B)
        g_h = math.exp(LOG_G[h])
        q_blk = q_ref[:, cols]
        k_blk = k_ref[:, cols]
        v_blk = vb_ref[:, cols]
        v_bf = v_blk.astype(BF16)
        intra = jnp.sum(q_blk * k_blk, axis=-1, keepdims=True) * v_blk
        cross = jnp.zeros((SAMPLE_ROWS, DV_B), F32)
        for r in range(SAMPLE_ROWS):
            state = s_ref[r, h]
            q_only_r = jnp.where(rid == r, q_blk, 0.0).astype(BF16)
            cross = cross + jnp.dot(q_only_r, state.astype(BF16), preferred_element_type=F32)
            k_only_r = jnp.where(rid == r, k_blk, 0.0).astype(BF16)
            outer = lax.dot_general(k_only_r, v_bf, (((0,), (0,)), ((), ())), preferred_element_type=F32)
            so_ref[r, h] = state * g_h + outer
        ret = intra + cross * g_h
        mu = jnp.mean(ret, axis=-1, keepdims=True)
        rc = ret - mu
        var = jnp.mean(rc * rc, axis=-1, keepdims=True)
        normed = rc * lax.rsqrt(var + EPS) * gn_ref[:, cols]
        o_ref[:, W_A + h * DV_B:W_A + (h + 1) * DV_B] = g_ref[:, cols] * normed


def _mix_sample(proj, w0, b0, gn, state):
    t = proj.shape[1]
    rb = SAMPLE_ROWS

    def sec(s):
        return pl.BlockSpec((None, rb, SEC), lambda i, s=s: (s, i, 0))

    return pl.pallas_call(
        _mix_sample_kernel,
        grid=(t // rb,),
        in_specs=[sec(0), sec(1), sec(2), sec(3), sec(4), sec(5),
                  _resident((1, W_A)), _resident((1, W_A)), _resident((1, W_B)),
                  pl.BlockSpec((rb, H_B, DK_B, DV_B), lambda i: (i, 0, 0, 0))],
        out_specs=[pl.BlockSpec((rb, W_A + W_B), lambda i: (i, 0)),
                   pl.BlockSpec((rb, H_B, DK_B, DV_B), lambda i: (i, 0, 0, 0))],
        out_shape=[jax.ShapeDtypeStruct((t, W_A + W_B), F32),
                   jax.ShapeDtypeStruct(state.shape, F32)],
        compiler_params=_params(("arbitrary",)),
        name="mix_sample",
    )(proj, proj, proj, proj, proj, proj, w0, b0, gn, state)


ATTN_SAMPLE_ROWS = 4
SUBLANES = 8


def _attn_sample_kernel(q_ref, ck_ref, cv_ref, o_ref):
    i = pl.program_id(0)
    pairs = N_MEM * H_X
    fold = lambda t: pltpu.roll(t, H_X, 1)
    for r in range(ATTN_SAMPLE_ROWS):
        row = i * ATTN_SAMPLE_ROWS + r
        q = q_ref[row]
        q8 = jnp.concatenate([q] * (SUBLANES // H_X), axis=0)
        k3 = ck_ref[0, r].reshape(pairs // SUBLANES, SUBLANES, DH_X)
        v3 = cv_ref[0, r].reshape(pairs // SUBLANES, SUBLANES, DH_X)
        s = jnp.sum(k3 * q8[None], axis=-1, keepdims=True) * (DH_X ** -0.5)
        m = jnp.max(s, axis=0, keepdims=True)
        m = jnp.maximum(m, fold(m))
        e = jnp.exp(s - m)
        l = jnp.sum(e, axis=0, keepdims=True)
        acc = jnp.sum(e * v3, axis=0, keepdims=True)
        o_ref[row] = ((acc + fold(acc)) / (l + fold(l)))[0, :H_X, :]


def _attn_sample(q, ck, cv):
    t = q.shape[0]
    rb = ATTN_SAMPLE_ROWS
    cache_spec = pl.BlockSpec((1, rb, N_MEM, H_X, DH_X), lambda i: (0, i, 0, 0, 0))
    return pl.pallas_call(
        _attn_sample_kernel,
        grid=(t // rb,),
        in_specs=[_resident((t, H_X, DH_X)), cache_spec, cache_spec],
        out_specs=_whole_out((t, H_X, DH_X)),
        out_shape=jax.ShapeDtypeStruct((t, H_X, DH_X), F32),
        compiler_params=_params(("arbitrary",)),
        name="attn_sample",
    )(q, ck, cv)


def _rope_tables(pos):
    half = DK_B // 2
    freqs = ROPE_THETA ** (-jnp.arange(half, dtype=F32) / half)
    ang = pos[:, None] * freqs[None, :]
    cos = jnp.cos(ang)
    sin = jnp.sin(ang)
    return jnp.concatenate([cos, cos], axis=-1), jnp.concatenate([-sin, sin], axis=-1)


def kernel(x_prompt, x_sample, mem_prompt, cache_mem_k, cache_mem_v, state_ret, norm1_g, w_in, sgu_norm_g, sgu_w_s, sgu_b, ret_gn_g, w_out, norm2_g, mem_norm_g, w_cq, w_ck, w_cv, w_co, norm3_g, w_ff1, w_ff2, final_norm_g):
    batch, seq, d = x_prompt.shape
    ts = x_sample.shape[0]
    n_chunks = seq // CHUNK
    tp = batch * seq

    g1 = norm1_g[0][None, :]
    g2 = norm2_g[0][None, :]
    g3 = norm3_g[0][None, :]
    gm = mem_norm_g[0][None, :]
    gf = final_norm_g[None, :]
    lng = sgu_norm_g[0][None, :]
    gn = ret_gn_g[0][None, :]
    ws = sgu_w_s[0]
    sb = sgu_b[0]
    xp = x_prompt.reshape(tp, d)
    xs = x_sample.reshape(ts, d)

    rope_p = _rope_tables(jnp.arange(seq, dtype=F32))
    rope_s = _rope_tables(jnp.full((ts,), PAST_LEN, dtype=F32))
    proj_p, proj_s = _in_proj(xp, xs, g1, w_in[0], rope_p, rope_s, lng, tm=1024)
    mix_p, state_p = _mix_prompt(proj_p, ws, sb.T, gn, batch=batch, n_chunks=n_chunks)
    w0 = jnp.repeat(ws[:, 0, 0], CH_A)[None, :]
    b0 = jnp.repeat(sb[:, 0], CH_A)[None, :]
    mix_s, state_s = _mix_sample(proj_s, w0, b0, gn, state_ret[0])

    mem = mem_prompt.reshape(batch * N_MEM, d)
    mk5, mk = _mem_proj(mem, gm, w_ck[0], batch=batch, name="mem_k")
    mv5, mv = _mem_proj(mem, gm, w_cv[0], batch=batch, name="mem_v")
    tr = 512
    h1_p, q_p, h1_s, q_s = _out_q(mix_p, xp, mix_s.astype(BF16), xs, w_out[0], w_cq[0], g2, tm=tr)
    att_s = _attn_sample(q_s.reshape(ts, H_X, DH_X), cache_mem_k, cache_mem_v).reshape(ts, d)
    h2_p, xn3_p, h2_s, xn3_s = _attn_co(q_p, mk, mv, h1_p, att_s.astype(BF16), h1_s, w_co[0], g3,
                                        tm=tr, tiles_per_batch=seq // tr)

    hid_p, hid_s = _ff1(xn3_p, xn3_s, w_ff1[0], tm=2048, tn=1024)
    part_p, part_s = _ff2(hid_p, hid_s, w_ff2[0], tm=2048, tn=1024, tk=1024)
    y_p = _add_norm(part_p, h2_p, gf, tm=512)
    y_s = _add_norm(part_s.transpose(1, 0, 2).reshape(ts, d), h2_s, gf, tm=ts)

    return (y_p.reshape(batch, seq, d),
            y_s.reshape(ts, 1, d),
            mk5,
            mv5,
            state_p[None],
            state_s[None],
            proj_s[1].reshape(1, ts, 1, H_A, CH_A))
```

```python
import functools
import math

import jax
import jax.numpy as jnp
from jax import lax
from jax.experimental import pallas as pl
from jax.experimental.pallas import tpu as pltpu

F32 = jnp.float32
BF16 = jnp.bfloat16

D_MODEL = 2048
H_A = 8
CH_A = 128
W_A = H_A * CH_A
CHUNK = 128
H_B = 8
DK_B = 128
DV_B = 128
W_B = H_B * DV_B
ROPE_THETA = 10000.0
N_MEM = 256
H_X = 4
DH_X = D_MODEL // H_X
D_FF = 4 * D_MODEL
EPS = 1e-6
PAST_LEN = 16384

SEC = 1024
N_SEC = 6
LOG_G = tuple(math.log(1.0 - 2.0 ** (-5.0 - h)) for h in range(H_B))

V7X_VMEM_BYTES = 64 * 1024 * 1024
VMEM_LIMIT = V7X_VMEM_BYTES - 4 * 1024 * 1024
ROW_CHUNK = 256


def _params(sem):
    return pltpu.CompilerParams(dimension_semantics=sem, vmem_limit_bytes=VMEM_LIMIT)


def _resident(shape):
    zeros = (0,) * len(shape)
    return pl.BlockSpec(shape, lambda *_: zeros, pipeline_mode=pl.Buffered(1))


def _whole_out(shape):
    zeros = (0,) * len(shape)
    return pl.BlockSpec(shape, lambda *_: zeros)


def _row_loop(n_rows, fn):
    rc = min(n_rows, ROW_CHUNK)
    assert n_rows % rc == 0

    def body(c, carry):
        fn(pl.ds(pl.multiple_of(c * rc, rc), rc))
        return carry

    lax.fori_loop(0, n_rows // rc, body, 0)


def _rms(h, g_ref):
    ms = jnp.mean(h * h, axis=-1, keepdims=True)
    return h * lax.rsqrt(ms + EPS) * g_ref[...]


def _rms_rows(x_ref, g_ref, xn_ref, n_rows):
    def fn(rows):
        xn_ref[rows, :] = _rms(x_ref[rows, :], g_ref).astype(BF16)

    _row_loop(n_rows, fn)


def _inproj_kernel(x_ref, g_ref, wt_ref, wb_ref, cc_ref, ss_ref, lng_ref, xs_ref, ccs_ref, sss_ref,
                   o_ref, os_ref, xn_ref, xsn_ref, *, tm, ts):
    i = pl.program_id(0)
    j = pl.program_id(1)
    kh = D_MODEL // 2

    def matmul(lhs_ref):
        return (jnp.dot(lhs_ref[:, :kh], wt_ref[...], preferred_element_type=F32)
                + jnp.dot(lhs_ref[:, kh:], wb_ref[...], preferred_element_type=F32))

    @pl.when(j == 0)
    def _():
        _rms_rows(x_ref, g_ref, xn_ref, tm)

    @pl.when((j == 0) & (i == 0))
    def _():
        _rms_rows(xs_ref, g_ref, xsn_ref, ts)

    def layernorm(acc, cc, ss):
        z = jax.nn.gelu(acc)
        mu = jnp.mean(z, axis=-1, keepdims=True)
        zc = z - mu
        var = jnp.mean(zc * zc, axis=-1, keepdims=True)
        return zc * lax.rsqrt(var + EPS) * lng_ref[...]

    def rotary(acc, cc, ss):
        scale = jnp.where(j == 3, DK_B ** -0.5, 1.0).astype(F32)
        heads = []
        for h in range(H_B):
            blk = acc[:, h * DK_B:(h + 1) * DK_B]
            rot = pltpu.roll(blk, DK_B // 2, 1)
            heads.append((blk * cc + rot * ss) * scale)
        return jnp.concatenate(heads, axis=-1)

    def section(pred, epilogue):
        @pl.when(pred)
        def _():
            o_ref[...] = epilogue(matmul(xn_ref), cc_ref[...], ss_ref[...]).astype(o_ref.dtype)

            @pl.when(i == 0)
            def _():
                os_ref[j] = epilogue(matmul(xsn_ref), ccs_ref[...], sss_ref[...])

    section(j == 0, lambda acc, cc, ss: jax.nn.gelu(acc))
    section(j == 1, layernorm)
    section((j == 2) | (j == 3), rotary)
    section(j == 4, lambda acc, cc, ss: acc)
    section(j == 5, lambda acc, cc, ss: jax.nn.silu(acc))


def _in_proj(x, xs, g, w_top, w_bot, rope_p, rope_s, lng, *, tm):
    t = x.shape[0]
    ts = xs.shape[0]
    rope_blocks = rope_p[0].shape[0] // tm
    rope_spec = pl.BlockSpec((tm, DK_B), lambda i, j: (i % rope_blocks, 0))
    w_spec = pl.BlockSpec((D_MODEL // 2, SEC), lambda i, j: (0, j))
    return pl.pallas_call(
        functools.partial(_inproj_kernel, tm=tm, ts=ts),
        grid=(t // tm, N_SEC),
        in_specs=[
            pl.BlockSpec((tm, D_MODEL), lambda i, j: (i, 0)),
            _resident((1, D_MODEL)),
            w_spec, w_spec,
            rope_spec, rope_spec,
            _resident((1, SEC)),
            _resident((ts, D_MODEL)), _resident((ts, DK_B)), _resident((ts, DK_B)),
        ],
        out_specs=[pl.BlockSpec((tm, SEC), lambda i, j: (i, j)), _whole_out((N_SEC, ts, SEC))],
        out_shape=[jax.ShapeDtypeStruct((t, N_SEC * SEC), BF16),
                   jax.ShapeDtypeStruct((N_SEC, ts, SEC), F32)],
        scratch_shapes=[pltpu.VMEM((tm, D_MODEL), BF16), pltpu.VMEM((ts, D_MODEL), BF16)],
        compiler_params=_params(("arbitrary", "arbitrary")),
        name="in_proj",
    )(x, g, w_top, w_bot, rope_p[0], rope_p[1], lng, xs, rope_s[0], rope_s[1])


def _mem_kernel(x_ref, g_ref, w_ref, win_ref, o5_ref, o2_ref, winb_ref, wb_ref):
    @pl.when(pl.program_id(0) == 0)
    def _():
        def fn(rows):
            wb_ref[rows, :] = w_ref[rows, :].astype(BF16)
        _row_loop(D_MODEL, fn)

    res = jnp.dot(_rms(x_ref[...], g_ref).astype(BF16), wb_ref[...], preferred_element_type=F32)
    o2_ref[...] = res.astype(BF16)
    for h in range(H_X):
        o5_ref[0, 0, :, h, :] = res[:, h * DH_X:(h + 1) * DH_X]

    def cast(rows):
        winb_ref[rows, :] = win_ref[rows, :].astype(BF16)
    _row_loop(win_ref.shape[0], cast)


def _mem_proj(mem, g, w, w_in, half, *, batch, name):
    d = mem.shape[1]
    rows, cols = w_in.shape[0] // 2, w_in.shape[1]
    blk = rows // batch
    return pl.pallas_call(
        _mem_kernel,
        grid=(batch,),
        in_specs=[pl.BlockSpec((N_MEM, d), lambda b: (b, 0)), _resident((1, d)), _resident((d, d)),
                  pl.BlockSpec((blk, cols), lambda b: (half * batch + b, 0))],
        out_specs=[pl.BlockSpec((1, 1, N_MEM, H_X, DH_X), lambda b: (0, b, 0, 0, 0)),
                   pl.BlockSpec((N_MEM, d), lambda b: (b, 0)),
                   pl.BlockSpec((blk, cols), lambda b: (b, 0))],
        out_shape=[jax.ShapeDtypeStruct((1, batch, N_MEM, H_X, DH_X), F32),
                   jax.ShapeDtypeStruct((batch * N_MEM, d), BF16),
                   jax.ShapeDtypeStruct((rows, cols), BF16)],
        scratch_shapes=[pltpu.VMEM((d, d), BF16)],
        compiler_params=_params(("arbitrary",)),
        name=name,
    )(mem, g, w, w_in)


def _relu2(acc):
    return jnp.square(jnp.maximum(acc, 0.0))


def _ff1_kernel(x_ref, w_ref, xs_ref, o_ref, os_ref):
    o_ref[...] = _relu2(jnp.dot(x_ref[...], w_ref[...], preferred_element_type=F32)).astype(BF16)

    @pl.when(pl.program_id(0) == 0)
    def _():
        os_ref[pl.program_id(1)] = _relu2(jnp.dot(xs_ref[...], w_ref[...],
                                                  preferred_element_type=F32)).astype(BF16)


def _ff1(x, xs, w, *, tm, tn):
    t, k = x.shape
    ts = xs.shape[0]
    n = w.shape[1]
    return pl.pallas_call(
        _ff1_kernel,
        grid=(t // tm, n // tn),
        in_specs=[pl.BlockSpec((tm, k), lambda i, j: (i, 0)),
                  pl.BlockSpec((k, tn), lambda i, j: (0, j)),
                  _resident((ts, k))],
        out_specs=[pl.BlockSpec((tm, tn), lambda i, j: (i, j)), _whole_out((n // tn, ts, tn))],
        out_shape=[jax.ShapeDtypeStruct((t, n), BF16), jax.ShapeDtypeStruct((n // tn, ts, tn), BF16)],
        compiler_params=_params(("arbitrary", "arbitrary")),
        name="ff1",
    )(x, w, xs)


def _ff2_kernel(x_ref, w_ref, xs_ref, o_ref, os_ref, *, sub):
    i, j, k = pl.program_id(0), pl.program_id(1), pl.program_id(2)

    @pl.when(k == 0)
    def _():
        o_ref[...] = jnp.zeros_like(o_ref)

    o_ref[...] += jnp.dot(x_ref[...], w_ref[...], preferred_element_type=F32)

    @pl.when(i == 0)
    def _():
        ks = xs_ref.shape[2]
        part = sum(jnp.dot(xs_ref[k * sub + c], w_ref[c * ks:(c + 1) * ks, :], preferred_element_type=F32)
                   for c in range(sub))

        @pl.when(k == 0)
        def _():
            os_ref[j] = part

        @pl.when(k > 0)
        def _():
            os_ref[j] += part


def _ff2(x, xs, w, *, tm, tn, tk):
    t, k = x.shape
    ts, ks = xs.shape[1], xs.shape[2]
    n = w.shape[1]
    return pl.pallas_call(
        functools.partial(_ff2_kernel, sub=tk // ks),
        grid=(t // tm, n // tn, k // tk),
        in_specs=[pl.BlockSpec((tm, tk), lambda i, j, kk: (i, kk)),
                  pl.BlockSpec((tk, tn), lambda i, j, kk: (kk, j)),
                  _resident(xs.shape)],
        out_specs=[pl.BlockSpec((tm, tn), lambda i, j, kk: (i, j)), _whole_out((n // tn, ts, tn))],
        out_shape=[jax.ShapeDtypeStruct((t, n), F32), jax.ShapeDtypeStruct((n // tn, ts, tn), F32)],
        compiler_params=_params(("arbitrary", "arbitrary", "arbitrary")),
        name="ff2",
    )(x, w, xs)


def _add_norm_kernel(x_ref, r_ref, g_ref, o_ref, *, tm):
    def fn(rows):
        o_ref[rows, :] = _rms(r_ref[rows, :] + x_ref[rows, :], g_ref)

    _row_loop(tm, fn)


def _add_norm(x, r, g, *, tm):
    t, d = x.shape
    row = pl.BlockSpec((tm, d), lambda i: (i, 0))
    return pl.pallas_call(
        functools.partial(_add_norm_kernel, tm=tm),
        grid=(t // tm,),
        in_specs=[row, row, _resident((1, d))],
        out_specs=row,
        out_shape=jax.ShapeDtypeStruct((t, d), F32),
        compiler_params=_params(("arbitrary",)),
        name="final_norm",
    )(x, r, g)


def _tile_spec(tm, width=D_MODEL):
    return pl.BlockSpec((tm, width), lambda s: (s, 0))


def _outq_kernel(mix_ref, x_ref, wo_ref, wq_ref, g2_ref, mixs_ref, xs_ref,
                 h1_ref, q_ref, h1s_ref, qs_ref):
    def block(mix_r, x_r, h1_r, q_r):
        h1 = x_r[...] + jnp.dot(mix_r[...], wo_ref[...], preferred_element_type=F32)
        h1_r[...] = h1
        q_r[...] = jnp.dot(_rms(h1, g2_ref).astype(BF16), wq_ref[...],
                           preferred_element_type=F32).astype(q_r.dtype)

    block(mix_ref, x_ref, h1_ref, q_ref)

    @pl.when(pl.program_id(0) == 0)
    def _():
        block(mixs_ref, xs_ref, h1s_ref, qs_ref)


def _out_q(mix, x, mix_s, xs, w_out, w_cq, g2, *, tm):
    t, d = x.shape
    ts = xs.shape[0]
    return pl.pallas_call(
        _outq_kernel,
        grid=(t // tm,),
        in_specs=[_tile_spec(tm), _tile_spec(tm), _resident((d, d)), _resident((d, d)), _resident((1, d)),
                  _resident((ts, d)), _resident((ts, d))],
        out_specs=[_tile_spec(tm), _tile_spec(tm), _whole_out((ts, d)), _whole_out((ts, d))],
        out_shape=[jax.ShapeDtypeStruct((t, d), F32), jax.ShapeDtypeStruct((t, d), BF16),
                   jax.ShapeDtypeStruct((ts, d), F32), jax.ShapeDtypeStruct((ts, d), F32)],
        compiler_params=_params(("arbitrary",)),
        name="out_q",
    )(mix, x, w_out, w_cq, g2, mix_s, xs)


def _softmax_rows(s):
    m = jnp.max(s, axis=-1, keepdims=True)
    e = jnp.exp(s - m)
    return e / jnp.sum(e, axis=-1, keepdims=True)


def _attn_co_kernel(q_ref, mk_ref, mv_ref, h1_ref, wc_ref, g3_ref, atts_ref, h1s_ref,
                    h2_ref, xn_ref, h2s_ref, xns_ref, att_ref):
    def co_block(att_r, h1_r, h2_r, xn_r):
        h2 = h1_r[...] + jnp.dot(att_r[...], wc_ref[...], preferred_element_type=F32)
        h2_r[...] = h2
        xn_r[...] = _rms(h2, g3_ref).astype(BF16)

    for h in range(H_X):
        cols = slice(h * DH_X, (h + 1) * DH_X)
        sc = lax.dot_general(q_ref[:, cols], mk_ref[:, cols], (((1,), (1,)), ((), ())),
                             preferred_element_type=F32) * (DH_X ** -0.5)
        p = _softmax_rows(sc)
        att_ref[:, cols] = jnp.dot(p.astype(BF16), mv_ref[:, cols],
                                   preferred_element_type=F32).astype(BF16)
    co_block(att_ref, h1_ref, h2_ref, xn_ref)

    @pl.when(pl.program_id(0) == 0)
    def _():
        co_block(atts_ref, h1s_ref, h2s_ref, xns_ref)


def _attn_co(q, mk, mv, h1, att_s, h1_s, w_co, g3, *, tm, tiles_per_batch):
    t, d = h1.shape
    ts = h1_s.shape[0]
    mem_spec = pl.BlockSpec((N_MEM, d), lambda s: (s // tiles_per_batch, 0))
    return pl.pallas_call(
        _attn_co_kernel,
        grid=(t // tm,),
        in_specs=[_tile_spec(tm), mem_spec, mem_spec, _tile_spec(tm), _resident((d, d)), _resident((1, d)),
                  _resident((ts, d)), _resident((ts, d))],
        out_specs=[_tile_spec(tm), _tile_spec(tm), _whole_out((ts, d)), _whole_out((ts, d))],
        out_shape=[jax.ShapeDtypeStruct((t, d), F32), jax.ShapeDtypeStruct((t, d), BF16),
                   jax.ShapeDtypeStruct((ts, d), F32), jax.ShapeDtypeStruct((ts, d), BF16)],
        scratch_shapes=[pltpu.VMEM((tm, d), BF16)],
        compiler_params=_params(("arbitrary",)),
        name="attn_co",
    )(q, mk, mv, h1, w_co, g3, att_s, h1_s)


def _mix_prompt_kernel(u_ref, v_ref, q_ref, k_ref, vb_ref, g_ref, ws_ref, bt_ref, gn_ref, *rest,
                       n_chunks, n_cast):
    cast_in, rest = rest[:n_cast], rest[n_cast:]
    o_ref, so_ref = rest[:2]
    cast_out, rest = rest[2:2 + n_cast], rest[2 + n_cast:]
    st_ref, wt_ref, bias_ref, dm_ref, qd_ref, kd_ref = rest
    b = pl.program_id(0)
    n = pl.program_id(1)

    for src, dst in zip(cast_in, cast_out):
        dst[...] = src[...].astype(BF16)

    @pl.when((b == 0) & (n == 0))
    def _():
        ii = lax.broadcasted_iota(jnp.int32, (CHUNK, CHUNK), 0)
        jj = lax.broadcasted_iota(jnp.int32, (CHUNK, CHUNK), 1)
        causal = ii >= jj
        diff = jnp.maximum((ii - jj).astype(F32), 0.0)
        ridx = ii.astype(F32)
        for h in range(H_A):
            lg = LOG_G[h]
            wt_ref[h] = (ws_ref[h] * causal.astype(F32)).astype(BF16)
            bias_ref[h] = jnp.broadcast_to(bt_ref[:, h:h + 1], (CHUNK, CH_A))
            dm_ref[h] = jnp.where(causal, jnp.exp(lg * diff), 0.0)
            qd_ref[h] = jnp.exp(lg * (ridx + 1.0))
            kd_ref[h] = jnp.exp(lg * (CHUNK - 1.0 - ridx))

    @pl.when(n == 0)
    def _():
        st_ref[...] = jnp.zeros_like(st_ref)

    for h in range(H_A):
        cols = slice(h * CH_A, (h + 1) * CH_A)
        mixed = jnp.dot(wt_ref[h], v_ref[:, cols], preferred_element_type=F32) + bias_ref[h]
        o_ref[:, cols] = (u_ref[:, cols].astype(F32) * mixed).astype(BF16)

        kh = k_ref[:, cols]
        vh = vb_ref[:, cols]
        state = st_ref[h]
        state_bf = state.astype(BF16)
        qh = q_ref[:, cols]
        scores = lax.dot_general(qh, kh, (((1,), (1,)), ((), ())), preferred_element_type=F32) * dm_ref[h]
        intra = jnp.dot(scores.astype(BF16), vh, preferred_element_type=F32)
        cross = jnp.dot(qh, state_bf, preferred_element_type=F32) * qd_ref[h]
        ret = intra + cross
        mu = jnp.mean(ret, axis=-1, keepdims=True)
        rc = ret - mu
        var = jnp.mean(rc * rc, axis=-1, keepdims=True)
        normed = rc * lax.rsqrt(var + EPS) * gn_ref[:, cols]
        o_ref[:, W_A + h * DV_B:W_A + (h + 1) * DV_B] = (g_ref[:, cols].astype(F32) * normed).astype(BF16)
        kd = (kh.astype(F32) * kd_ref[h]).astype(BF16)
        st_ref[h] = state * math.exp(LOG_G[h] * CHUNK) + lax.dot_general(
            kd, vh, (((0,), (0,)), ((), ())), preferred_element_type=F32)

    @pl.when(n == n_chunks - 1)
    def _():
        so_ref[0] = st_ref[...]


def _mix_prompt(proj, ws, bt, gn, cast_weights, *, batch, n_chunks):
    t = proj.shape[0]
    steps = batch * n_chunks

    def sec(s):
        return pl.BlockSpec((CHUNK, SEC), lambda b, n, s=s: (b * n_chunks + n, s))

    def cast_spec(w):
        return pl.BlockSpec((w.shape[0] // steps, w.shape[1]), lambda b, n: (b * n_chunks + n, 0))

    cast_specs = [cast_spec(w) for w in cast_weights]
    table = pltpu.VMEM((H_B, CHUNK, CHUNK), F32)
    res = pl.pallas_call(
        functools.partial(_mix_prompt_kernel, n_chunks=n_chunks, n_cast=len(cast_weights)),
        grid=(batch, n_chunks),
        in_specs=[sec(0), sec(1), sec(2), sec(3), sec(4), sec(5),
                  _resident((H_A, CHUNK, CHUNK)), _resident((CHUNK, H_A)), _resident((1, W_B))] + cast_specs,
        out_specs=[pl.BlockSpec((CHUNK, W_A + W_B), lambda b, n: (b * n_chunks + n, 0)),
                   pl.BlockSpec((1, H_B, DK_B, DV_B), lambda b, n: (b, 0, 0, 0))] + cast_specs,
        out_shape=[jax.ShapeDtypeStruct((t, W_A + W_B), BF16),
                   jax.ShapeDtypeStruct((batch, H_B, DK_B, DV_B), F32)]
                  + [jax.ShapeDtypeStruct(w.shape, BF16) for w in cast_weights],
        scratch_shapes=[pltpu.VMEM((H_B, DK_B, DV_B), F32), pltpu.VMEM((H_A, CHUNK, CHUNK), BF16),
                        table, table, table, table],
        compiler_params=_params(("arbitrary", "arbitrary")),
        name="mix_prompt",
    )(proj, proj, proj, proj, proj, proj, ws, bt, gn, *cast_weights)
    return res[0], res[1], res[2:]


SAMPLE_ROWS = 8


def _mix_sample_kernel(u_ref, v_ref, q_ref, k_ref, vb_ref, g_ref, w0_ref, b0_ref, gn_ref, s_ref,
                       o_ref, so_ref):
    rid = lax.broadcasted_iota(jnp.int32, (SAMPLE_ROWS, 1), 0)
    o_ref[:, :W_A] = u_ref[...] * (w0_ref[...] * v_ref[...] + b0_ref[...])

    for h in range(H_B):
        cols = slice(h * DK_B, (h + 1) * DK_B)
        g_h = math.exp(LOG_G[h])
        q_blk = q_ref[:, cols]
        k_blk = k_ref[:, cols]
        v_blk = vb_ref[:, cols]
        v_bf = v_blk.astype(BF16)
        intra = jnp.sum(q_blk * k_blk, axis=-1, keepdims=True) * v_blk
        cross = jnp.zeros((SAMPLE_ROWS, DV_B), F32)
        for r in range(SAMPLE_ROWS):
            state = s_ref[r, h]
            q_only_r = jnp.where(rid == r, q_blk, 0.0).astype(BF16)
            cross = cross + jnp.dot(q_only_r, state.astype(BF16), preferred_element_type=F32)
            k_only_r = jnp.where(rid == r, k_blk, 0.0).astype(BF16)
            outer = lax.dot_general(k_only_r, v_bf, (((0,), (0,)), ((), ())), preferred_element_type=F32)
            so_ref[r, h] = state * g_h + outer
        ret = intra + cross * g_h
        mu = jnp.mean(ret, axis=-1, keepdims=True)
        rc = ret - mu
        var = jnp.mean(rc * rc, axis=-1, keepdims=True)
        normed = rc * lax.rsqrt(var + EPS) * gn_ref[:, cols]
        o_ref[:, W_A + h * DV_B:W_A + (h + 1) * DV_B] = g_ref[:, cols] * normed


def _mix_sample(proj, w0, b0, gn, state):
    t = proj.shape[1]
    rb = SAMPLE_ROWS

    def sec(s):
        return pl.BlockSpec((None, rb, SEC), lambda i, s=s: (s, i, 0))

    return pl.pallas_call(
        _mix_sample_kernel,
        grid=(t // rb,),
        in_specs=[sec(0), sec(1), sec(2), sec(3), sec(4), sec(5),
                  _resident((1, W_A)), _resident((1, W_A)), _resident((1, W_B)),
                  pl.BlockSpec((rb, H_B, DK_B, DV_B), lambda i: (i, 0, 0, 0))],
        out_specs=[pl.BlockSpec((rb, W_A + W_B), lambda i: (i, 0)),
                   pl.BlockSpec((rb, H_B, DK_B, DV_B), lambda i: (i, 0, 0, 0))],
        out_shape=[jax.ShapeDtypeStruct((t, W_A + W_B), F32),
                   jax.ShapeDtypeStruct(state.shape, F32)],
        compiler_params=_params(("arbitrary",)),
        name="mix_sample",
    )(proj, proj, proj, proj, proj, proj, w0, b0, gn, state)


ATTN_SAMPLE_ROWS = 4
SUBLANES = 8


def _attn_sample_kernel(q_ref, ck_ref, cv_ref, o_ref):
    i = pl.program_id(0)
    pairs = N_MEM * H_X
    fold = lambda t: pltpu.roll(t, H_X, 1)
    for r in range(ATTN_SAMPLE_ROWS):
        row = i * ATTN_SAMPLE_ROWS + r
        q = q_ref[row]
        q8 = jnp.concatenate([q] * (SUBLANES // H_X), axis=0)
        k3 = ck_ref[0, r].reshape(pairs // SUBLANES, SUBLANES, DH_X)
        v3 = cv_ref[0, r].reshape(pairs // SUBLANES, SUBLANES, DH_X)
        s = jnp.sum(k3 * q8[None], axis=-1, keepdims=True) * (DH_X ** -0.5)
        m = jnp.max(s, axis=0, keepdims=True)
        m = jnp.maximum(m, fold(m))
        e = jnp.exp(s - m)
        l = jnp.sum(e, axis=0, keepdims=True)
        acc = jnp.sum(e * v3, axis=0, keepdims=True)
        o_ref[row] = ((acc + fold(acc)) / (l + fold(l)))[0, :H_X, :]


def _attn_sample(q, ck, cv):
    t = q.shape[0]
    rb = ATTN_SAMPLE_ROWS
    cache_spec = pl.BlockSpec((1, rb, N_MEM, H_X, DH_X), lambda i: (0, i, 0, 0, 0))
    return pl.pallas_call(
        _attn_sample_kernel,
        grid=(t // rb,),
        in_specs=[_resident((t, H_X, DH_X)), cache_spec, cache_spec],
        out_specs=_whole_out((t, H_X, DH_X)),
        out_shape=jax.ShapeDtypeStruct((t, H_X, DH_X), F32),
        compiler_params=_params(("arbitrary",)),
        name="attn_sample",
    )(q, ck, cv)


def _rope_tables(pos):
    half = DK_B // 2
    freqs = ROPE_THETA ** (-jnp.arange(half, dtype=F32) / half)
    ang = pos[:, None] * freqs[None, :]
    cos = jnp.cos(ang)
    sin = jnp.sin(ang)
    return jnp.concatenate([cos, cos], axis=-1), jnp.concatenate([-sin, sin], axis=-1)


def kernel(x_prompt, x_sample, mem_prompt, cache_mem_k, cache_mem_v, state_ret, norm1_g, w_in, sgu_norm_g, sgu_w_s, sgu_b, ret_gn_g, w_out, norm2_g, mem_norm_g, w_cq, w_ck, w_cv, w_co, norm3_g, w_ff1, w_ff2, final_norm_g):
    batch, seq, d = x_prompt.shape
    ts = x_sample.shape[0]
    n_chunks = seq // CHUNK
    tp = batch * seq

    g1 = norm1_g[0][None, :]
    g2 = norm2_g[0][None, :]
    g3 = norm3_g[0][None, :]
    gm = mem_norm_g[0][None, :]
    gf = final_norm_g[None, :]
    lng = sgu_norm_g[0][None, :]
    gn = ret_gn_g[0][None, :]
    ws = sgu_w_s[0]
    sb = sgu_b[0]
    xp = x_prompt.reshape(tp, d)
    xs = x_sample.reshape(ts, d)

    mem = mem_prompt.reshape(batch * N_MEM, d)
    mk5, mk, win_top = _mem_proj(mem, gm, w_ck[0], w_in[0], 0, batch=batch, name="mem_k")
    mv5, mv, win_bot = _mem_proj(mem, gm, w_cv[0], w_in[0], 1, batch=batch, name="mem_v")

    rope_p = _rope_tables(jnp.arange(seq, dtype=F32))
    rope_s = _rope_tables(jnp.full((ts,), PAST_LEN, dtype=F32))
    proj_p, proj_s = _in_proj(xp, xs, g1, win_top, win_bot, rope_p, rope_s, lng, tm=1024)
    mix_p, state_p, (wout_b, wcq_b, wco_b, wf1_b, wf2_b) = _mix_prompt(
        proj_p, ws, sb.T, gn, (w_out[0], w_cq[0], w_co[0], w_ff1[0], w_ff2[0]), batch=batch, n_chunks=n_chunks)
    w0 = jnp.repeat(ws[:, 0, 0], CH_A)[None, :]
    b0 = jnp.repeat(sb[:, 0], CH_A)[None, :]
    mix_s, state_s = _mix_sample(proj_s, w0, b0, gn, state_ret[0])

    tr = 512
    h1_p, q_p, h1_s, q_s = _out_q(mix_p, xp, mix_s.astype(BF16), xs, wout_b, wcq_b, g2, tm=tr)
    att_s = _attn_sample(q_s.reshape(ts, H_X, DH_X), cache_mem_k, cache_mem_v).reshape(ts, d)
    h2_p, xn3_p, h2_s, xn3_s = _attn_co(q_p, mk, mv, h1_p, att_s.astype(BF16), h1_s, wco_b, g3,
                                        tm=tr, tiles_per_batch=seq // tr)

    hid_p, hid_s = _ff1(xn3_p, xn3_s, wf1_b, tm=2048, tn=1024)
    part_p, part_s = _ff2(hid_p, hid_s, wf2_b, tm=2048, tn=1024, tk=2048)
    y_p = _add_norm(part_p, h2_p, gf, tm=512)
    y_s = _add_norm(part_s.transpose(1, 0, 2).reshape(ts, d), h2_s, gf, tm=ts)

    return (y_p.reshape(batch, seq, d),
            y_s.reshape(ts, 1, d),
            mk5,
            mv5,
            state_p[None],
            state_s[None],
            proj_s[1].reshape(1, ts, 1, H_A, CH_A))
```

```python
import functools
import math

import jax
import jax.numpy as jnp
from jax import lax
from jax.experimental import pallas as pl
from jax.experimental.pallas import tpu as pltpu

F32 = jnp.float32
BF16 = jnp.bfloat16

D_MODEL = 2048
H_A = 8
CH_A = 128
W_A = H_A * CH_A
CHUNK = 128
H_B = 8
DK_B = 128
DV_B = 128
W_B = H_B * DV_B
ROPE_THETA = 10000.0
N_MEM = 256
H_X = 4
DH_X = D_MODEL // H_X
D_FF = 4 * D_MODEL
EPS = 1e-6
PAST_LEN = 16384

SEC = 1024
N_SEC = 6
LOG_G = tuple(math.log(1.0 - 2.0 ** (-5.0 - h)) for h in range(H_B))

V7X_VMEM_BYTES = 64 * 1024 * 1024
VMEM_LIMIT = V7X_VMEM_BYTES - 4 * 1024 * 1024
ROW_CHUNK = 256
SUB_TILES = 1


def _params(sem):
    return pltpu.CompilerParams(dimension_semantics=sem, vmem_limit_bytes=VMEM_LIMIT)


def _resident(shape):
    zeros = (0,) * len(shape)
    return pl.BlockSpec(shape, lambda *_: zeros, pipeline_mode=pl.Buffered(1))


def _whole_out(shape):
    zeros = (0,) * len(shape)
    return pl.BlockSpec(shape, lambda *_: zeros)


def _row_loop(n_rows, fn):
    rc = min(n_rows, ROW_CHUNK)
    assert n_rows % rc == 0

    def body(c, carry):
        fn(pl.ds(pl.multiple_of(c * rc, rc), rc))
        return carry

    lax.fori_loop(0, n_rows // rc, body, 0)


def _sub_tiles(n_rows):
    step = n_rows // SUB_TILES if n_rows % (SUB_TILES * 16) == 0 else n_rows
    return [slice(r0, r0 + step) for r0 in range(0, n_rows, step)]


def _rms(h, g_ref):
    ms = jnp.mean(h * h, axis=-1, keepdims=True)
    return h * lax.rsqrt(ms + EPS) * g_ref[...]


def _rms_rows(x_ref, g_ref, xn_ref, n_rows):
    def fn(rows):
        xn_ref[rows, :] = _rms(x_ref[rows, :], g_ref).astype(BF16)

    _row_loop(n_rows, fn)


def _inproj_kernel(x_ref, g_ref, w_ref, cc_ref, ss_ref, lng_ref, xs_ref, ccs_ref, sss_ref,
                   o_ref, os_ref, xn_ref, xsn_ref, *, tm, ts):
    i = pl.program_id(0)
    j = pl.program_id(1)

    @pl.when(j == 0)
    def _():
        _rms_rows(x_ref, g_ref, xn_ref, tm)

    @pl.when((j == 0) & (i == 0))
    def _():
        _rms_rows(xs_ref, g_ref, xsn_ref, ts)

    def layernorm(acc, cc, ss):
        z = jax.nn.gelu(acc)
        mu = jnp.mean(z, axis=-1, keepdims=True)
        zc = z - mu
        var = jnp.mean(zc * zc, axis=-1, keepdims=True)
        return zc * lax.rsqrt(var + EPS) * lng_ref[...]

    def rotary(acc, cc, ss):
        scale = jnp.where(j == 3, DK_B ** -0.5, 1.0).astype(F32)
        heads = []
        for h in range(H_B):
            blk = acc[:, h * DK_B:(h + 1) * DK_B]
            rot = pltpu.roll(blk, DK_B // 2, 1)
            heads.append((blk * cc + rot * ss) * scale)
        return jnp.concatenate(heads, axis=-1)

    def section(pred, epilogue):
        @pl.when(pred)
        def _():
            acc = jnp.dot(xn_ref[...], w_ref[...].astype(BF16), preferred_element_type=F32)
            o_ref[...] = epilogue(acc, cc_ref[...], ss_ref[...]).astype(o_ref.dtype)

            @pl.when(i == 0)
            def _():
                acc_s = jnp.dot(xsn_ref[...], w_ref[...].astype(BF16), preferred_element_type=F32)
                os_ref[j] = epilogue(acc_s, ccs_ref[...], sss_ref[...])

    section(j == 0, lambda acc, cc, ss: jax.nn.gelu(acc))
    section(j == 1, layernorm)
    section((j == 2) | (j == 3), rotary)
    section(j == 4, lambda acc, cc, ss: acc)
    section(j == 5, lambda acc, cc, ss: jax.nn.silu(acc))


def _in_proj(x, xs, g, w, rope_p, rope_s, lng, *, tm):
    t = x.shape[0]
    ts = xs.shape[0]
    rope_blocks = rope_p[0].shape[0] // tm
    rope_spec = pl.BlockSpec((tm, DK_B), lambda i, j: (i % rope_blocks, 0))
    return pl.pallas_call(
        functools.partial(_inproj_kernel, tm=tm, ts=ts),
        grid=(t // tm, N_SEC),
        in_specs=[
            pl.BlockSpec((tm, D_MODEL), lambda i, j: (i, 0)),
            _resident((1, D_MODEL)),
            pl.BlockSpec((D_MODEL, SEC), lambda i, j: (0, j)),
            rope_spec, rope_spec,
            _resident((1, SEC)),
            _resident((ts, D_MODEL)), _resident((ts, DK_B)), _resident((ts, DK_B)),
        ],
        out_specs=[pl.BlockSpec((tm, SEC), lambda i, j: (i, j)), _whole_out((N_SEC, ts, SEC))],
        out_shape=[jax.ShapeDtypeStruct((t, N_SEC * SEC), BF16),
                   jax.ShapeDtypeStruct((N_SEC, ts, SEC), F32)],
        scratch_shapes=[pltpu.VMEM((tm, D_MODEL), BF16), pltpu.VMEM((ts, D_MODEL), BF16)],
        compiler_params=_params(("arbitrary", "arbitrary")),
        name="in_proj",
    )(x, g, w, rope_p[0], rope_p[1], lng, xs, rope_s[0], rope_s[1])


def _mem_kernel(x_ref, g_ref, w_ref, o5_ref, o2_ref, wb_ref):
    @pl.when(pl.program_id(0) == 0)
    def _():
        def fn(rows):
            wb_ref[rows, :] = w_ref[rows, :].astype(BF16)
        _row_loop(D_MODEL, fn)

    res = jnp.dot(_rms(x_ref[...], g_ref).astype(BF16), wb_ref[...], preferred_element_type=F32)
    o2_ref[...] = res.astype(BF16)
    for h in range(H_X):
        o5_ref[0, 0, :, h, :] = res[:, h * DH_X:(h + 1) * DH_X]


def _mem_proj(mem, g, w, *, batch, name):
    d = mem.shape[1]
    return pl.pallas_call(
        _mem_kernel,
        grid=(batch,),
        in_specs=[pl.BlockSpec((N_MEM, d), lambda b: (b, 0)), _resident((1, d)), _resident((d, d))],
        out_specs=[pl.BlockSpec((1, 1, N_MEM, H_X, DH_X), lambda b: (0, b, 0, 0, 0)),
                   pl.BlockSpec((N_MEM, d), lambda b: (b, 0))],
        out_shape=[jax.ShapeDtypeStruct((1, batch, N_MEM, H_X, DH_X), F32),
                   jax.ShapeDtypeStruct((batch * N_MEM, d), BF16)],
        scratch_shapes=[pltpu.VMEM((d, d), BF16)],
        compiler_params=_params(("arbitrary",)),
        name=name,
    )(mem, g, w)


def _relu2(acc):
    return jnp.square(jnp.maximum(acc, 0.0))


def _ff1_kernel(x_ref, w_ref, xs_ref, o_ref, os_ref):
    o_ref[...] = _relu2(jnp.dot(x_ref[...], w_ref[...].astype(BF16), preferred_element_type=F32)).astype(BF16)

    @pl.when(pl.program_id(0) == 0)
    def _():
        os_ref[pl.program_id(1)] = _relu2(jnp.dot(xs_ref[...], w_ref[...].astype(BF16),
                                                  preferred_element_type=F32)).astype(BF16)


def _ff1(x, xs, w, *, tm, tn):
    t, k = x.shape
    ts = xs.shape[0]
    n = w.shape[1]
    return pl.pallas_call(
        _ff1_kernel,
        grid=(t // tm, n // tn),
        in_specs=[pl.BlockSpec((tm, k), lambda i, j: (i, 0)),
                  pl.BlockSpec((k, tn), lambda i, j: (0, j)),
                  _resident((ts, k))],
        out_specs=[pl.BlockSpec((tm, tn), lambda i, j: (i, j)), _whole_out((n // tn, ts, tn))],
        out_shape=[jax.ShapeDtypeStruct((t, n), BF16), jax.ShapeDtypeStruct((n // tn, ts, tn), BF16)],
        compiler_params=_params(("arbitrary", "arbitrary")),
        name="ff1",
    )(x, w, xs)


def _ff2_kernel(x_ref, w_ref, xs_ref, o_ref, os_ref):
    i, j, k = pl.program_id(0), pl.program_id(1), pl.program_id(2)

    @pl.when(k == 0)
    def _():
        o_ref[...] = jnp.zeros_like(o_ref)

    o_ref[...] += jnp.dot(x_ref[...], w_ref[...].astype(BF16), preferred_element_type=F32)

    @pl.when(i == 0)
    def _():
        part = jnp.dot(xs_ref[k], w_ref[...].astype(BF16), preferred_element_type=F32)

        @pl.when(k == 0)
        def _():
            os_ref[j] = part

        @pl.when(k > 0)
        def _():
            os_ref[j] += part


def _ff2(x, xs, w, *, tm, tn, tk):
    t, k = x.shape
    ts = xs.shape[1]
    n = w.shape[1]
    return pl.pallas_call(
        _ff2_kernel,
        grid=(t // tm, n // tn, k // tk),
        in_specs=[pl.BlockSpec((tm, tk), lambda i, j, kk: (i, kk)),
                  pl.BlockSpec((tk, tn), lambda i, j, kk: (kk, j)),
                  _resident((k // tk, ts, tk))],
        out_specs=[pl.BlockSpec((tm, tn), lambda i, j, kk: (i, j)), _whole_out((n // tn, ts, tn))],
        out_shape=[jax.ShapeDtypeStruct((t, n), F32), jax.ShapeDtypeStruct((n // tn, ts, tn), F32)],
        compiler_params=_params(("arbitrary", "arbitrary", "arbitrary")),
        name="ff2",
    )(x, w, xs)


def _add_norm_kernel(x_ref, r_ref, g_ref, o_ref, *, tm):
    def fn(rows):
        o_ref[rows, :] = _rms(r_ref[rows, :] + x_ref[rows, :], g_ref)

    _row_loop(tm, fn)


def _add_norm(x, r, g, *, tm):
    t, d = x.shape
    row = pl.BlockSpec((tm, d), lambda i: (i, 0))
    return pl.pallas_call(
        functools.partial(_add_norm_kernel, tm=tm),
        grid=(t // tm,),
        in_specs=[row, row, _resident((1, d))],
        out_specs=row,
        out_shape=jax.ShapeDtypeStruct((t, d), F32),
        compiler_params=_params(("arbitrary",)),
        name="final_norm",
    )(x, r, g)


W_CHUNK = 256
N_LOAD = D_MODEL // W_CHUNK


def _load_weight_chunk(s, pairs):
    rows = pl.ds(pl.multiple_of(s * W_CHUNK, W_CHUNK), W_CHUNK)
    for src, dst in pairs:
        dst[rows, :] = src[...].astype(BF16)


def _w_chunk_spec():
    return pl.BlockSpec((W_CHUNK, D_MODEL), lambda s: (jnp.minimum(s, N_LOAD - 1), 0))


def _tile_spec(tm, width=D_MODEL):
    return pl.BlockSpec((tm, width), lambda s: (jnp.maximum(s - N_LOAD, 0), 0))


def _outq_kernel(mix_ref, x_ref, wo_ref, wq_ref, g2_ref, mixs_ref, xs_ref,
                 h1_ref, q_ref, h1s_ref, qs_ref, wo_bf, wq_bf):
    s = pl.program_id(0)

    def block(mix_r, x_r, h1_r, q_r):
        for rows in _sub_tiles(x_r.shape[0]):
            h1 = x_r[rows, :] + jnp.dot(mix_r[rows, :], wo_bf[...], preferred_element_type=F32)
            h1_r[rows, :] = h1
            q_r[rows, :] = jnp.dot(_rms(h1, g2_ref).astype(BF16), wq_bf[...],
                                   preferred_element_type=F32).astype(q_r.dtype)

    @pl.when(s < N_LOAD)
    def _():
        _load_weight_chunk(s, ((wo_ref, wo_bf), (wq_ref, wq_bf)))

    @pl.when(s >= N_LOAD)
    def _():
        block(mix_ref, x_ref, h1_ref, q_ref)

    @pl.when(s == N_LOAD)
    def _():
        block(mixs_ref, xs_ref, h1s_ref, qs_ref)


def _out_q(mix, x, mix_s, xs, w_out, w_cq, g2, *, tm):
    t, d = x.shape
    ts = xs.shape[0]
    return pl.pallas_call(
        _outq_kernel,
        grid=(N_LOAD + t // tm,),
        in_specs=[_tile_spec(tm), _tile_spec(tm), _w_chunk_spec(), _w_chunk_spec(), _resident((1, d)),
                  _resident((ts, d)), _resident((ts, d))],
        out_specs=[_tile_spec(tm), _tile_spec(tm), _whole_out((ts, d)), _whole_out((ts, d))],
        out_shape=[jax.ShapeDtypeStruct((t, d), F32), jax.ShapeDtypeStruct((t, d), BF16),
                   jax.ShapeDtypeStruct((ts, d), F32), jax.ShapeDtypeStruct((ts, d), F32)],
        scratch_shapes=[pltpu.VMEM((d, d), BF16), pltpu.VMEM((d, d), BF16)],
        compiler_params=_params(("arbitrary",)),
        name="out_q",
    )(mix, x, w_out, w_cq, g2, mix_s, xs)


def _softmax_rows(s):
    m = jnp.max(s, axis=-1, keepdims=True)
    e = jnp.exp(s - m)
    return e / jnp.sum(e, axis=-1, keepdims=True)


def _attn_co_kernel(q_ref, mk_ref, mv_ref, h1_ref, wc_ref, g3_ref, atts_ref, h1s_ref,
                    h2_ref, xn_ref, h2s_ref, xns_ref, wc_bf, att_ref):
    s = pl.program_id(0)

    def co_block(att_r, h1_r, h2_r, xn_r, rows):
        h2 = h1_r[rows, :] + jnp.dot(att_r[rows, :], wc_bf[...], preferred_element_type=F32)
        h2_r[rows, :] = h2
        xn_r[rows, :] = _rms(h2, g3_ref).astype(BF16)

    @pl.when(s < N_LOAD)
    def _():
        _load_weight_chunk(s, ((wc_ref, wc_bf),))

    @pl.when(s >= N_LOAD)
    def _():
        for rows in _sub_tiles(h1_ref.shape[0]):
            for h in range(H_X):
                cols = slice(h * DH_X, (h + 1) * DH_X)
                sc = lax.dot_general(q_ref[rows, cols], mk_ref[:, cols], (((1,), (1,)), ((), ())),
                                     preferred_element_type=F32) * (DH_X ** -0.5)
                p = _softmax_rows(sc)
                att_ref[rows, cols] = jnp.dot(p.astype(BF16), mv_ref[:, cols],
                                              preferred_element_type=F32).astype(BF16)
            co_block(att_ref, h1_ref, h2_ref, xn_ref, rows)

    @pl.when(s == N_LOAD)
    def _():
        co_block(atts_ref, h1s_ref, h2s_ref, xns_ref, slice(None))


def _attn_co(q, mk, mv, h1, att_s, h1_s, w_co, g3, *, tm, tiles_per_batch):
    t, d = h1.shape
    ts = h1_s.shape[0]
    mem_spec = pl.BlockSpec((N_MEM, d), lambda s: (jnp.maximum(s - N_LOAD, 0) // tiles_per_batch, 0))
    return pl.pallas_call(
        _attn_co_kernel,
        grid=(N_LOAD + t // tm,),
        in_specs=[_tile_spec(tm), mem_spec, mem_spec, _tile_spec(tm), _w_chunk_spec(), _resident((1, d)),
                  _resident((ts, d)), _resident((ts, d))],
        out_specs=[_tile_spec(tm), _tile_spec(tm), _whole_out((ts, d)), _whole_out((ts, d))],
        out_shape=[jax.ShapeDtypeStruct((t, d), F32), jax.ShapeDtypeStruct((t, d), BF16),
                   jax.ShapeDtypeStruct((ts, d), F32), jax.ShapeDtypeStruct((ts, d), BF16)],
        scratch_shapes=[pltpu.VMEM((d, d), BF16), pltpu.VMEM((tm, d), BF16)],
        compiler_params=_params(("arbitrary",)),
        name="attn_co",
    )(q, mk, mv, h1, w_co, g3, att_s, h1_s)


def _mix_prompt_kernel(u_ref, v_ref, q_ref, k_ref, vb_ref, g_ref, ws_ref, bt_ref, gn_ref,
                       o_ref, so_ref, st_ref, wt_ref, bias_ref, dm_ref, qd_ref, kd_ref, *, batch, n_chunks):
    n = pl.program_id(0)

    @pl.when(n == 0)
    def _():
        ii = lax.broadcasted_iota(jnp.int32, (CHUNK, CHUNK), 0)
        jj = lax.broadcasted_iota(jnp.int32, (CHUNK, CHUNK), 1)
        causal = ii >= jj
        diff = jnp.maximum((ii - jj).astype(F32), 0.0)
        ridx = ii.astype(F32)
        for h in range(H_A):
            lg = LOG_G[h]
            wt_ref[h] = (ws_ref[h] * causal.astype(F32)).astype(BF16)
            bias_ref[h] = jnp.broadcast_to(bt_ref[:, h:h + 1], (CHUNK, CH_A))
            dm_ref[h] = jnp.where(causal, jnp.exp(lg * diff), 0.0)
            qd_ref[h] = jnp.exp(lg * (ridx + 1.0))
            kd_ref[h] = jnp.exp(lg * (CHUNK - 1.0 - ridx))
        st_ref[...] = jnp.zeros_like(st_ref)

    for b in range(batch):
        for h in range(H_A):
            cols = slice(h * CH_A, (h + 1) * CH_A)
            mixed = jnp.dot(wt_ref[h], v_ref[b, :, cols], preferred_element_type=F32) + bias_ref[h]
            o_ref[b, :, cols] = (u_ref[b, :, cols].astype(F32) * mixed).astype(BF16)

            kh = k_ref[b, :, cols]
            vh = vb_ref[b, :, cols]
            state = st_ref[b, h]
            state_bf = state.astype(BF16)
            qh = q_ref[b, :, cols]
            scores = lax.dot_general(qh, kh, (((1,), (1,)), ((), ())), preferred_element_type=F32) * dm_ref[h]
            intra = jnp.dot(scores.astype(BF16), vh, preferred_element_type=F32)
            cross = jnp.dot(qh, state_bf, preferred_element_type=F32) * qd_ref[h]
            ret = intra + cross
            mu = jnp.mean(ret, axis=-1, keepdims=True)
            rc = ret - mu
            var = jnp.mean(rc * rc, axis=-1, keepdims=True)
            normed = rc * lax.rsqrt(var + EPS) * gn_ref[:, cols]
            o_ref[b, :, W_A + h * DV_B:W_A + (h + 1) * DV_B] = (
                g_ref[b, :, cols].astype(F32) * normed).astype(BF16)
            kd = (kh.astype(F32) * kd_ref[h]).astype(BF16)
            st_ref[b, h] = state * math.exp(LOG_G[h] * CHUNK) + lax.dot_general(
                kd, vh, (((0,), (0,)), ((), ())), preferred_element_type=F32)

    @pl.when(n == n_chunks - 1)
    def _():
        so_ref[...] = st_ref[...]


def _mix_prompt(proj, ws, bt, gn, *, n_chunks):
    batch, seq, _ = proj.shape

    def sec(s):
        return pl.BlockSpec((batch, CHUNK, SEC), lambda n, s=s: (0, n, s))

    table = pltpu.VMEM((H_B, CHUNK, CHUNK), F32)
    return pl.pallas_call(
        functools.partial(_mix_prompt_kernel, batch=batch, n_chunks=n_chunks),
        grid=(n_chunks,),
        in_specs=[sec(0), sec(1), sec(2), sec(3), sec(4), sec(5),
                  _resident((H_A, CHUNK, CHUNK)), _resident((CHUNK, H_A)), _resident((1, W_B))],
        out_specs=[pl.BlockSpec((batch, CHUNK, W_A + W_B), lambda n: (0, n, 0)),
                   _whole_out((batch, H_B, DK_B, DV_B))],
        out_shape=[jax.ShapeDtypeStruct((batch, seq, W_A + W_B), BF16),
                   jax.ShapeDtypeStruct((batch, H_B, DK_B, DV_B), F32)],
        scratch_shapes=[pltpu.VMEM((batch, H_B, DK_B, DV_B), F32), pltpu.VMEM((H_A, CHUNK, CHUNK), BF16),
                        table, table, table, table],
        compiler_params=_params(("arbitrary",)),
        name="mix_prompt",
    )(proj, proj, proj, proj, proj, proj, ws, bt, gn)


SAMPLE_ROWS = 8


def _mix_sample_kernel(u_ref, v_ref, q_ref, k_ref, vb_ref, g_ref, w0_ref, b0_ref, gn_ref, s_ref,
                       o_ref, so_ref):
    rid = lax.broadcasted_iota(jnp.int32, (SAMPLE_ROWS, 1), 0)
    o_ref[:, :W_A] = u_ref[...] * (w0_ref[...] * v_ref[...] + b0_ref[...])

    for h in range(H_B):
        cols = slice(h * DK_B, (h + 1) * DK_B)
        g_h = math.exp(LOG_G[h])
        q_blk = q_ref[:, cols]
        k_blk = k_ref[:, cols]
        v_blk = vb_ref[:, cols]
        v_bf = v_blk.astype(BF16)
        intra = jnp.sum(q_blk * k_blk, axis=-1, keepdims=True) * v_blk
        cross = jnp.zeros((SAMPLE_ROWS, DV_B), F32)
        for r in range(SAMPLE_ROWS):
            state = s_ref[r, h]
            q_only_r = jnp.where(rid == r, q_blk, 0.0).astype(BF16)
            cross = cross + jnp.dot(q_only_r, state.astype(BF16), preferred_element_type=F32)
            k_only_r = jnp.where(rid == r, k_blk, 0.0).astype(BF16)
            outer = lax.dot_general(k_only_r, v_bf, (((0,), (0,)), ((), ())), preferred_element_type=F32)
            so_ref[r, h] = state * g_h + outer
        ret = intra + cross * g_h
        mu = jnp.mean(ret, axis=-1, keepdims=True)
        rc = ret - mu
        var = jnp.mean(rc * rc, axis=-1, keepdims=True)
        normed = rc * lax.rsqrt(var + EPS) * gn_ref[:, cols]
        o_ref[:, W_A + h * DV_B:W_A + (h + 1) * DV_B] = g_ref[:, cols] * normed


def _mix_sample(proj, w0, b0, gn, state):
    t = proj.shape[1]
    rb = SAMPLE_ROWS

    def sec(s):
        return pl.BlockSpec((None, rb, SEC), lambda i, s=s: (s, i, 0))

    return pl.pallas_call(
        _mix_sample_kernel,
        grid=(t // rb,),
        in_specs=[sec(0), sec(1), sec(2), sec(3), sec(4), sec(5),
                  _resident((1, W_A)), _resident((1, W_A)), _resident((1, W_B)),
                  pl.BlockSpec((rb, H_B, DK_B, DV_B), lambda i: (i, 0, 0, 0))],
        out_specs=[pl.BlockSpec((rb, W_A + W_B), lambda i: (i, 0)),
                   pl.BlockSpec((rb, H_B, DK_B, DV_B), lambda i: (i, 0, 0, 0))],
        out_shape=[jax.ShapeDtypeStruct((t, W_A + W_B), F32),
                   jax.ShapeDtypeStruct(state.shape, F32)],
        compiler_params=_params(("arbitrary",)),
        name="mix_sample",
    )(proj, proj, proj, proj, proj, proj, w0, b0, gn, state)


ATTN_SAMPLE_ROWS = 4
SUBLANES = 8


def _attn_sample_kernel(q_ref, ck_ref, cv_ref, o_ref):
    i = pl.program_id(0)
    pairs = N_MEM * H_X
    fold = lambda t: pltpu.roll(t, H_X, 1)
    for r in range(ATTN_SAMPLE_ROWS):
        row = i * ATTN_SAMPLE_ROWS + r
        q = q_ref[row]
        q8 = jnp.concatenate([q] * (SUBLANES // H_X), axis=0)
        k3 = ck_ref[0, r].reshape(pairs // SUBLANES, SUBLANES, DH_X)
        v3 = cv_ref[0, r].reshape(pairs // SUBLANES, SUBLANES, DH_X)
        s = jnp.sum(k3 * q8[None], axis=-1, keepdims=True) * (DH_X ** -0.5)
        m = jnp.max(s, axis=0, keepdims=True)
        m = jnp.maximum(m, fold(m))
        e = jnp.exp(s - m)
        l = jnp.sum(e, axis=0, keepdims=True)
        acc = jnp.sum(e * v3, axis=0, keepdims=True)
        o_ref[row] = ((acc + fold(acc)) / (l + fold(l)))[0, :H_X, :]


def _attn_sample(q, ck, cv):
    t = q.shape[0]
    rb = ATTN_SAMPLE_ROWS
    cache_spec = pl.BlockSpec((1, rb, N_MEM, H_X, DH_X), lambda i: (0, i, 0, 0, 0))
    return pl.pallas_call(
        _attn_sample_kernel,
        grid=(t // rb,),
        in_specs=[_resident((t, H_X, DH_X)), cache_spec, cache_spec],
        out_specs=_whole_out((t, H_X, DH_X)),
        out_shape=jax.ShapeDtypeStruct((t, H_X, DH_X), F32),
        compiler_params=_params(("arbitrary",)),
        name="attn_sample",
    )(q, ck, cv)


def _rope_tables(pos):
    half = DK_B // 2
    freqs = ROPE_THETA ** (-jnp.arange(half, dtype=F32) / half)
    ang = pos[:, None] * freqs[None, :]
    cos = jnp.cos(ang)
    sin = jnp.sin(ang)
    return jnp.concatenate([cos, cos], axis=-1), jnp.concatenate([-sin, sin], axis=-1)


def kernel(x_prompt, x_sample, mem_prompt, cache_mem_k, cache_mem_v, state_ret, norm1_g, w_in, sgu_norm_g, sgu_w_s, sgu_b, ret_gn_g, w_out, norm2_g, mem_norm_g, w_cq, w_ck, w_cv, w_co, norm3_g, w_ff1, w_ff2, final_norm_g):
    batch, seq, d = x_prompt.shape
    ts = x_sample.shape[0]
    n_chunks = seq // CHUNK
    tp = batch * seq

    g1 = norm1_g[0][None, :]
    g2 = norm2_g[0][None, :]
    g3 = norm3_g[0][None, :]
    gm = mem_norm_g[0][None, :]
    gf = final_norm_g[None, :]
    lng = sgu_norm_g[0][None, :]
    gn = ret_gn_g[0][None, :]
    ws = sgu_w_s[0]
    sb = sgu_b[0]
    xp = x_prompt.reshape(tp, d)
    xs = x_sample.reshape(ts, d)

    rope_p = _rope_tables(jnp.arange(seq, dtype=F32))
    rope_s = _rope_tables(jnp.full((ts,), PAST_LEN, dtype=F32))
    proj_p, proj_s = _in_proj(xp, xs, g1, w_in[0], rope_p, rope_s, lng, tm=1024)
    mix_p, state_p = _mix_prompt(proj_p.reshape(batch, seq, N_SEC * SEC), ws, sb.T, gn, n_chunks=n_chunks)
    mix_p = mix_p.reshape(tp, W_A + W_B)
    w0 = jnp.repeat(ws[:, 0, 0], CH_A)[None, :]
    b0 = jnp.repeat(sb[:, 0], CH_A)[None, :]
    mix_s, state_s = _mix_sample(proj_s, w0, b0, gn, state_ret[0])

    mem = mem_prompt.reshape(batch * N_MEM, d)
    mk5, mk = _mem_proj(mem, gm, w_ck[0], batch=batch, name="mem_k")
    mv5, mv = _mem_proj(mem, gm, w_cv[0], batch=batch, name="mem_v")
    tr = 512
    h1_p, q_p, h1_s, q_s = _out_q(mix_p, xp, mix_s.astype(BF16), xs, w_out[0], w_cq[0], g2, tm=tr)
    att_s = _attn_sample(q_s.reshape(ts, H_X, DH_X), cache_mem_k, cache_mem_v).reshape(ts, d)
    h2_p, xn3_p, h2_s, xn3_s = _attn_co(q_p, mk, mv, h1_p, att_s.astype(BF16), h1_s, w_co[0], g3,
                                        tm=tr, tiles_per_batch=seq // tr)

    hid_p, hid_s = _ff1(xn3_p, xn3_s, w_ff1[0], tm=2048, tn=1024)
    part_p, part_s = _ff2(hid_p, hid_s, w_ff2[0], tm=2048, tn=1024, tk=1024)
    y_p = _add_norm(part_p, h2_p, gf, tm=512)
    y_s = _add_norm(part_s.transpose(1, 0, 2).reshape(ts, d), h2_s, gf, tm=ts)

    return (y_p.reshape(batch, seq, d),
            y_s.reshape(ts, 1, d),
            mk5,
            mv5,
            state_p[None],
            state_s[None],
            proj_s[1].reshape(1, ts, 1, H_A, CH_A))
```

```python
import functools
import math

import jax
import jax.numpy as jnp
from jax import lax
from jax.experimental import pallas as pl
from jax.experimental.pallas import tpu as pltpu

F32 = jnp.float32
BF16 = jnp.bfloat16

D_MODEL = 2048
H_A = 8
CH_A = 128
W_A = H_A * CH_A
CHUNK = 128
H_B = 8
DK_B = 128
DV_B = 128
W_B = H_B * DV_B
ROPE_THETA = 10000.0
N_MEM = 256
H_X = 4
DH_X = D_MODEL // H_X
D_FF = 4 * D_MODEL
EPS = 1e-6
PAST_LEN = 16384

SEC = 1024
N_SEC = 6
LOG_G = tuple(math.log(1.0 - 2.0 ** (-5.0 - h)) for h in range(H_B))

V7X_VMEM_BYTES = 64 * 1024 * 1024
VMEM_LIMIT = V7X_VMEM_BYTES - 4 * 1024 * 1024
ROW_CHUNK = 256
SUB_TILES = 1


def _params(sem):
    return pltpu.CompilerParams(dimension_semantics=sem, vmem_limit_bytes=VMEM_LIMIT)


def _resident(shape):
    zeros = (0,) * len(shape)
    return pl.BlockSpec(shape, lambda *_: zeros, pipeline_mode=pl.Buffered(1))


def _whole_out(shape):
    zeros = (0,) * len(shape)
    return pl.BlockSpec(shape, lambda *_: zeros)


def _row_loop(n_rows, fn):
    rc = min(n_rows, ROW_CHUNK)
    assert n_rows % rc == 0

    def body(c, carry):
        fn(pl.ds(pl.multiple_of(c * rc, rc), rc))
        return carry

    lax.fori_loop(0, n_rows // rc, body, 0)


def _sub_tiles(n_rows):
    step = n_rows // SUB_TILES if n_rows % (SUB_TILES * 16) == 0 else n_rows
    return [slice(r0, r0 + step) for r0 in range(0, n_rows, step)]


def _rms(h, g_ref):
    ms = jnp.mean(h * h, axis=-1, keepdims=True)
    return h * lax.rsqrt(ms + EPS) * g_ref[...]


def _rms_rows(x_ref, g_ref, xn_ref, n_rows):
    def fn(rows):
        xn_ref[rows, :] = _rms(x_ref[rows, :], g_ref).astype(BF16)

    _row_loop(n_rows, fn)


def _inproj_kernel(x_ref, g_ref, w_ref, cc_ref, ss_ref, lng_ref, xs_ref, ccs_ref, sss_ref,
                   o_ref, os_ref, xn_ref, xsn_ref, *, tm, ts):
    i = pl.program_id(0)
    j = pl.program_id(1)

    @pl.when(j == 0)
    def _():
        _rms_rows(x_ref, g_ref, xn_ref, tm)

    @pl.when((j == 0) & (i == 0))
    def _():
        _rms_rows(xs_ref, g_ref, xsn_ref, ts)

    def layernorm(acc, cc, ss):
        z = jax.nn.gelu(acc)
        mu = jnp.mean(z, axis=-1, keepdims=True)
        zc = z - mu
        var = jnp.mean(zc * zc, axis=-1, keepdims=True)
        return zc * lax.rsqrt(var + EPS) * lng_ref[...]

    def rotary(acc, cc, ss):
        scale = jnp.where(j == 3, DK_B ** -0.5, 1.0).astype(F32)
        heads = []
        for h in range(H_B):
            blk = acc[:, h * DK_B:(h + 1) * DK_B]
            rot = pltpu.roll(blk, DK_B // 2, 1)
            heads.append((blk * cc + rot * ss) * scale)
        return jnp.concatenate(heads, axis=-1)

    def section(pred, epilogue):
        @pl.when(pred)
        def _():
            acc = jnp.dot(xn_ref[...], w_ref[...].astype(BF16), preferred_element_type=F32)
            o_ref[...] = epilogue(acc, cc_ref[...], ss_ref[...]).astype(o_ref.dtype)

            @pl.when(i == 0)
            def _():
                acc_s = jnp.dot(xsn_ref[...], w_ref[...].astype(BF16), preferred_element_type=F32)
                os_ref[j] = epilogue(acc_s, ccs_ref[...], sss_ref[...])

    section(j == 0, lambda acc, cc, ss: jax.nn.gelu(acc))
    section(j == 1, layernorm)
    section((j == 2) | (j == 3), rotary)
    section(j == 4, lambda acc, cc, ss: acc)
    section(j == 5, lambda acc, cc, ss: jax.nn.silu(acc))


def _in_proj(x, xs, g, w, rope_p, rope_s, lng, *, tm):
    t = x.shape[0]
    ts = xs.shape[0]
    rope_blocks = rope_p[0].shape[0] // tm
    rope_spec = pl.BlockSpec((tm, DK_B), lambda i, j: (i % rope_blocks, 0))
    return pl.pallas_call(
        functools.partial(_inproj_kernel, tm=tm, ts=ts),
        grid=(t // tm, N_SEC),
        in_specs=[
            pl.BlockSpec((tm, D_MODEL), lambda i, j: (i, 0)),
            _resident((1, D_MODEL)),
            pl.BlockSpec((D_MODEL, SEC), lambda i, j: (0, j)),
            rope_spec, rope_spec,
            _resident((1, SEC)),
            _resident((ts, D_MODEL)), _resident((ts, DK_B)), _resident((ts, DK_B)),
        ],
        out_specs=[pl.BlockSpec((tm, SEC), lambda i, j: (i, j)), _whole_out((N_SEC, ts, SEC))],
        out_shape=[jax.ShapeDtypeStruct((t, N_SEC * SEC), BF16),
                   jax.ShapeDtypeStruct((N_SEC, ts, SEC), F32)],
        scratch_shapes=[pltpu.VMEM((tm, D_MODEL), BF16), pltpu.VMEM((ts, D_MODEL), BF16)],
        compiler_params=_params(("arbitrary", "arbitrary")),
        name="in_proj",
    )(x, g, w, rope_p[0], rope_p[1], lng, xs, rope_s[0], rope_s[1])


def _mem_kernel(x_ref, g_ref, w_ref, o5_ref, o2_ref, wb_ref):
    @pl.when(pl.program_id(0) == 0)
    def _():
        def fn(rows):
            wb_ref[rows, :] = w_ref[rows, :].astype(BF16)
        _row_loop(D_MODEL, fn)

    res = jnp.dot(_rms(x_ref[...], g_ref).astype(BF16), wb_ref[...], preferred_element_type=F32)
    o2_ref[...] = res.astype(BF16)
    for h in range(H_X):
        o5_ref[0, 0, :, h, :] = res[:, h * DH_X:(h + 1) * DH_X]


def _mem_proj(mem, g, w, *, batch, name):
    d = mem.shape[1]
    return pl.pallas_call(
        _mem_kernel,
        grid=(batch,),
        in_specs=[pl.BlockSpec((N_MEM, d), lambda b: (b, 0)), _resident((1, d)), _resident((d, d))],
        out_specs=[pl.BlockSpec((1, 1, N_MEM, H_X, DH_X), lambda b: (0, b, 0, 0, 0)),
                   pl.BlockSpec((N_MEM, d), lambda b: (b, 0))],
        out_shape=[jax.ShapeDtypeStruct((1, batch, N_MEM, H_X, DH_X), F32),
                   jax.ShapeDtypeStruct((batch * N_MEM, d), BF16)],
        scratch_shapes=[pltpu.VMEM((d, d), BF16)],
        compiler_params=_params(("arbitrary",)),
        name=name,
    )(mem, g, w)


def _relu2(acc):
    return jnp.square(jnp.maximum(acc, 0.0))


def _ff1_kernel(x_ref, w_ref, xs_ref, o_ref, os_ref):
    o_ref[...] = _relu2(jnp.dot(x_ref[...], w_ref[...].astype(BF16), preferred_element_type=F32)).astype(BF16)

    @pl.when(pl.program_id(0) == 0)
    def _():
        os_ref[pl.program_id(1)] = _relu2(jnp.dot(xs_ref[...], w_ref[...].astype(BF16),
                                                  preferred_element_type=F32)).astype(BF16)


def _ff1(x, xs, w, *, tm, tn):
    t, k = x.shape
    ts = xs.shape[0]
    n = w.shape[1]
    return pl.pallas_call(
        _ff1_kernel,
        grid=(t // tm, n // tn),
        in_specs=[pl.BlockSpec((tm, k), lambda i, j: (i, 0)),
                  pl.BlockSpec((k, tn), lambda i, j: (0, j)),
                  _resident((ts, k))],
        out_specs=[pl.BlockSpec((tm, tn), lambda i, j: (i, j)), _whole_out((n // tn, ts, tn))],
        out_shape=[jax.ShapeDtypeStruct((t, n), BF16), jax.ShapeDtypeStruct((n // tn, ts, tn), BF16)],
        compiler_params=_params(("arbitrary", "arbitrary")),
        name="ff1",
    )(x, w, xs)


def _ff2_kernel(x_ref, w_ref, xs_ref, o_ref, os_ref):
    i, j, k = pl.program_id(0), pl.program_id(1), pl.program_id(2)

    @pl.when(k == 0)
    def _():
        o_ref[...] = jnp.zeros_like(o_ref)

    o_ref[...] += jnp.dot(x_ref[...], w_ref[...].astype(BF16), preferred_element_type=F32)

    @pl.when(i == 0)
    def _():
        part = jnp.dot(xs_ref[k], w_ref[...].astype(BF16), preferred_element_type=F32)

        @pl.when(k == 0)
        def _():
            os_ref[j] = part

        @pl.when(k > 0)
        def _():
            os_ref[j] += part


def _ff2(x, xs, w, *, tm, tn, tk):
    t, k = x.shape
    ts = xs.shape[1]
    n = w.shape[1]
    return pl.pallas_call(
        _ff2_kernel,
        grid=(t // tm, n // tn, k // tk),
        in_specs=[pl.BlockSpec((tm, tk), lambda i, j, kk: (i, kk)),
                  pl.BlockSpec((tk, tn), lambda i, j, kk: (kk, j)),
                  _resident((k // tk, ts, tk))],
        out_specs=[pl.BlockSpec((tm, tn), lambda i, j, kk: (i, j)), _whole_out((n // tn, ts, tn))],
        out_shape=[jax.ShapeDtypeStruct((t, n), F32), jax.ShapeDtypeStruct((n // tn, ts, tn), F32)],
        compiler_params=_params(("arbitrary", "arbitrary", "arbitrary")),
        name="ff2",
    )(x, w, xs)


def _add_norm_kernel(x_ref, r_ref, g_ref, o_ref, *, tm):
    def fn(rows):
        o_ref[rows, :] = _rms(r_ref[rows, :] + x_ref[rows, :], g_ref)

    _row_loop(tm, fn)


def _add_norm(x, r, g, *, tm):
    t, d = x.shape
    row = pl.BlockSpec((tm, d), lambda i: (i, 0))
    return pl.pallas_call(
        functools.partial(_add_norm_kernel, tm=tm),
        grid=(t // tm,),
        in_specs=[row, row, _resident((1, d))],
        out_specs=row,
        out_shape=jax.ShapeDtypeStruct((t, d), F32),
        compiler_params=_params(("arbitrary",)),
        name="final_norm",
    )(x, r, g)


W_CHUNK = 256
N_LOAD = D_MODEL // W_CHUNK


def _load_weight_chunk(s, pairs):
    rows = pl.ds(pl.multiple_of(s * W_CHUNK, W_CHUNK), W_CHUNK)
    for src, dst in pairs:
        dst[rows, :] = src[...].astype(BF16)


def _w_chunk_spec():
    return pl.BlockSpec((W_CHUNK, D_MODEL), lambda s: (jnp.minimum(s, N_LOAD - 1), 0))


def _tile_spec(tm, width=D_MODEL):
    return pl.BlockSpec((tm, width), lambda s: (jnp.maximum(s - N_LOAD, 0), 0))


def _outq_kernel(mix_ref, x_ref, wo_ref, wq_ref, g2_ref, h1_ref, q_ref, wo_bf, wq_bf):
    s = pl.program_id(0)

    @pl.when(s < N_LOAD)
    def _():
        _load_weight_chunk(s, ((wo_ref, wo_bf), (wq_ref, wq_bf)))

    @pl.when(s >= N_LOAD)
    def _():
        for rows in _sub_tiles(x_ref.shape[0]):
            h1 = x_ref[rows, :] + jnp.dot(mix_ref[rows, :], wo_bf[...], preferred_element_type=F32)
            h1_ref[rows, :] = h1
            q_ref[rows, :] = jnp.dot(_rms(h1, g2_ref).astype(BF16), wq_bf[...],
                                     preferred_element_type=F32).astype(q_ref.dtype)


def _out_q(mix, x, w_out, w_cq, g2, *, tm):
    t, d = x.shape
    return pl.pallas_call(
        _outq_kernel,
        grid=(N_LOAD + t // tm,),
        in_specs=[_tile_spec(tm), _tile_spec(tm), _w_chunk_spec(), _w_chunk_spec(), _resident((1, d))],
        out_specs=[_tile_spec(tm), _tile_spec(tm)],
        out_shape=[jax.ShapeDtypeStruct((t, d), F32), jax.ShapeDtypeStruct((t, d), BF16)],
        scratch_shapes=[pltpu.VMEM((d, d), BF16), pltpu.VMEM((d, d), BF16)],
        compiler_params=_params(("arbitrary",)),
        name="out_q",
    )(mix, x, w_out, w_cq, g2)


def _outq_sample_kernel(mix_ref, x_ref, wo_ref, wq_ref, g2_ref, h1_ref, q_ref, xn_ref, *, n_k, tk):
    j = pl.program_id(0)

    @pl.when(j == 0)
    def _():
        h1_ref[...] = x_ref[...]

    @pl.when(j < n_k)
    def _():
        h1_ref[...] += jnp.dot(mix_ref[j], wo_ref[...].astype(BF16), preferred_element_type=F32)

    @pl.when(j == n_k)
    def _():
        xn = _rms(h1_ref[...], g2_ref).astype(BF16)
        for t in range(n_k):
            xn_ref[t] = xn[:, t * tk:(t + 1) * tk]
        q_ref[...] = jnp.zeros_like(q_ref)

    @pl.when(j >= n_k)
    def _():
        q_ref[...] += jnp.dot(xn_ref[j - n_k], wq_ref[...].astype(BF16), preferred_element_type=F32)


def _out_q_sample(mix, x, w_out, w_cq, g2, *, tk):
    ts, d = x.shape
    n_k = d // tk
    mix_k = mix.reshape(ts, n_k, tk).transpose(1, 0, 2)
    return pl.pallas_call(
        functools.partial(_outq_sample_kernel, n_k=n_k, tk=tk),
        grid=(2 * n_k,),
        in_specs=[_resident((n_k, ts, tk)), _resident((ts, d)),
                  pl.BlockSpec((tk, d), lambda j: (jnp.minimum(j, n_k - 1), 0)),
                  pl.BlockSpec((tk, d), lambda j: (jnp.maximum(j - n_k, 0), 0)),
                  _resident((1, d))],
        out_specs=[_whole_out((ts, d)), _whole_out((ts, d))],
        out_shape=[jax.ShapeDtypeStruct((ts, d), F32), jax.ShapeDtypeStruct((ts, d), F32)],
        scratch_shapes=[pltpu.VMEM((n_k, ts, tk), BF16)],
        compiler_params=_params(("arbitrary",)),
        name="out_q_sample",
    )(mix_k, x, w_out, w_cq, g2)


def _softmax_rows(s):
    m = jnp.max(s, axis=-1, keepdims=True)
    e = jnp.exp(s - m)
    return e / jnp.sum(e, axis=-1, keepdims=True)


def _attn_co_kernel(q_ref, mk_ref, mv_ref, h1_ref, wc_ref, g3_ref, atts_ref, h1s_ref,
                    h2_ref, xn_ref, h2s_ref, xns_ref, wc_bf, att_ref):
    s = pl.program_id(0)

    def co_block(att_r, h1_r, h2_r, xn_r, rows):
        h2 = h1_r[rows, :] + jnp.dot(att_r[rows, :], wc_bf[...], preferred_element_type=F32)
        h2_r[rows, :] = h2
        xn_r[rows, :] = _rms(h2, g3_ref).astype(BF16)

    @pl.when(s < N_LOAD)
    def _():
        _load_weight_chunk(s, ((wc_ref, wc_bf),))

    @pl.when(s >= N_LOAD)
    def _():
        for rows in _sub_tiles(h1_ref.shape[0]):
            for h in range(H_X):
                cols = slice(h * DH_X, (h + 1) * DH_X)
                sc = lax.dot_general(q_ref[rows, cols], mk_ref[:, cols], (((1,), (1,)), ((), ())),
                                     preferred_element_type=F32) * (DH_X ** -0.5)
                p = _softmax_rows(sc)
                att_ref[rows, cols] = jnp.dot(p.astype(BF16), mv_ref[:, cols],
                                              preferred_element_type=F32).astype(BF16)
            co_block(att_ref, h1_ref, h2_ref, xn_ref, rows)

    @pl.when(s == N_LOAD)
    def _():
        co_block(atts_ref, h1s_ref, h2s_ref, xns_ref, slice(None))


def _attn_co(q, mk, mv, h1, att_s, h1_s, w_co, g3, *, tm, tiles_per_batch):
    t, d = h1.shape
    ts = h1_s.shape[0]
    mem_spec = pl.BlockSpec((N_MEM, d), lambda s: (jnp.maximum(s - N_LOAD, 0) // tiles_per_batch, 0))
    return pl.pallas_call(
        _attn_co_kernel,
        grid=(N_LOAD + t // tm,),
        in_specs=[_tile_spec(tm), mem_spec, mem_spec, _tile_spec(tm), _w_chunk_spec(), _resident((1, d)),
                  _resident((ts, d)), _resident((ts, d))],
        out_specs=[_tile_spec(tm), _tile_spec(tm), _whole_out((ts, d)), _whole_out((ts, d))],
        out_shape=[jax.ShapeDtypeStruct((t, d), F32), jax.ShapeDtypeStruct((t, d), BF16),
                   jax.ShapeDtypeStruct((ts, d), F32), jax.ShapeDtypeStruct((ts, d), BF16)],
        scratch_shapes=[pltpu.VMEM((d, d), BF16), pltpu.VMEM((tm, d), BF16)],
        compiler_params=_params(("arbitrary",)),
        name="attn_co",
    )(q, mk, mv, h1, w_co, g3, att_s, h1_s)


def _mix_prompt_kernel(u_ref, v_ref, q_ref, k_ref, vb_ref, g_ref, ws_ref, bt_ref, gn_ref,
                       qs_ref, ck_ref, cv_ref,
                       o_ref, so_ref, att_ref, st_ref, wt_ref, bias_ref, dm_ref, qd_ref, kd_ref,
                       *, batch, n_chunks, cache_rows):
    n = pl.program_id(0)
    b = pl.program_id(1)
    _attn_sample_rows(qs_ref, ck_ref, cv_ref, att_ref, (n * batch + b) * cache_rows, cache_rows)

    @pl.when((n == 0) & (b == 0))
    def _():
        ii = lax.broadcasted_iota(jnp.int32, (CHUNK, CHUNK), 0)
        jj = lax.broadcasted_iota(jnp.int32, (CHUNK, CHUNK), 1)
        causal = ii >= jj
        diff = jnp.maximum((ii - jj).astype(F32), 0.0)
        ridx = ii.astype(F32)
        for h in range(H_A):
            lg = LOG_G[h]
            wt_ref[h] = (ws_ref[h] * causal.astype(F32)).astype(BF16)
            bias_ref[h] = jnp.broadcast_to(bt_ref[:, h:h + 1], (CHUNK, CH_A))
            dm_ref[h] = jnp.where(causal, jnp.exp(lg * diff), 0.0)
            qd_ref[h] = jnp.exp(lg * (ridx + 1.0))
            kd_ref[h] = jnp.exp(lg * (CHUNK - 1.0 - ridx))
        st_ref[...] = jnp.zeros_like(st_ref)

    for h in range(H_A):
        cols = slice(h * CH_A, (h + 1) * CH_A)
        mixed = jnp.dot(wt_ref[h], v_ref[:, cols], preferred_element_type=F32) + bias_ref[h]
        o_ref[:, cols] = (u_ref[:, cols].astype(F32) * mixed).astype(BF16)

        kh = k_ref[:, cols]
        vh = vb_ref[:, cols]
        state = st_ref[b, h]
        state_bf = state.astype(BF16)
        qh = q_ref[:, cols]
        scores = lax.dot_general(qh, kh, (((1,), (1,)), ((), ())), preferred_element_type=F32) * dm_ref[h]
        intra = jnp.dot(scores.astype(BF16), vh, preferred_element_type=F32)
        cross = jnp.dot(qh, state_bf, preferred_element_type=F32) * qd_ref[h]
        ret = intra + cross
        mu = jnp.mean(ret, axis=-1, keepdims=True)
        rc = ret - mu
        var = jnp.mean(rc * rc, axis=-1, keepdims=True)
        normed = rc * lax.rsqrt(var + EPS) * gn_ref[:, cols]
        o_ref[:, W_A + h * DV_B:W_A + (h + 1) * DV_B] = (g_ref[:, cols].astype(F32) * normed).astype(BF16)
        kd = (kh.astype(F32) * kd_ref[h]).astype(BF16)
        st_ref[b, h] = state * math.exp(LOG_G[h] * CHUNK) + lax.dot_general(
            kd, vh, (((0,), (0,)), ((), ())), preferred_element_type=F32)

    @pl.when((n == n_chunks - 1) & (b == batch - 1))
    def _():
        so_ref[...] = st_ref[...]


def _mix_prompt(proj, ws, bt, gn, q_s, ck, cv, *, n_chunks):
    batch, seq, _ = proj.shape
    ts = q_s.shape[0]
    cache_rows = ts // (n_chunks * batch)
    assert cache_rows * n_chunks * batch == ts

    def sec(s):
        return pl.BlockSpec((None, CHUNK, SEC), lambda n, b, s=s: (b, n, s))

    cache_spec = pl.BlockSpec((1, cache_rows, N_MEM, H_X, DH_X), lambda n, b: (0, n * batch + b, 0, 0, 0))
    table = pltpu.VMEM((H_B, CHUNK, CHUNK), F32)
    return pl.pallas_call(
        functools.partial(_mix_prompt_kernel, batch=batch, n_chunks=n_chunks, cache_rows=cache_rows),
        grid=(n_chunks, batch),
        in_specs=[sec(0), sec(1), sec(2), sec(3), sec(4), sec(5),
                  _resident((H_A, CHUNK, CHUNK)), _resident((CHUNK, H_A)), _resident((1, W_B)),
                  _resident((ts, H_X, DH_X)), cache_spec, cache_spec],
        out_specs=[pl.BlockSpec((None, CHUNK, W_A + W_B), lambda n, b: (b, n, 0)),
                   _whole_out((batch, H_B, DK_B, DV_B)),
                   _whole_out((ts, H_X, DH_X))],
        out_shape=[jax.ShapeDtypeStruct((batch, seq, W_A + W_B), BF16),
                   jax.ShapeDtypeStruct((batch, H_B, DK_B, DV_B), F32),
                   jax.ShapeDtypeStruct((ts, H_X, DH_X), F32)],
        scratch_shapes=[pltpu.VMEM((batch, H_B, DK_B, DV_B), F32), pltpu.VMEM((H_A, CHUNK, CHUNK), BF16),
                        table, table, table, table],
        compiler_params=_params(("arbitrary", "arbitrary")),
        name="mix_prompt",
    )(proj, proj, proj, proj, proj, proj, ws, bt, gn, q_s, ck, cv)


SAMPLE_ROWS = 8


def _mix_sample_kernel(u_ref, v_ref, q_ref, k_ref, vb_ref, g_ref, w0_ref, b0_ref, gn_ref, s_ref,
                       o_ref, so_ref):
    rid = lax.broadcasted_iota(jnp.int32, (SAMPLE_ROWS, 1), 0)
    o_ref[:, :W_A] = u_ref[...] * (w0_ref[...] * v_ref[...] + b0_ref[...])

    for h in range(H_B):
        cols = slice(h * DK_B, (h + 1) * DK_B)
        g_h = math.exp(LOG_G[h])
        q_blk = q_ref[:, cols]
        k_blk = k_ref[:, cols]
        v_blk = vb_ref[:, cols]
        v_bf = v_blk.astype(BF16)
        intra = jnp.sum(q_blk * k_blk, axis=-1, keepdims=True) * v_blk
        cross = jnp.zeros((SAMPLE_ROWS, DV_B), F32)
        for r in range(SAMPLE_ROWS):
            state = s_ref[r, h]
            q_only_r = jnp.where(rid == r, q_blk, 0.0).astype(BF16)
            cross = cross + jnp.dot(q_only_r, state.astype(BF16), preferred_element_type=F32)
            k_only_r = jnp.where(rid == r, k_blk, 0.0).astype(BF16)
            outer = lax.dot_general(k_only_r, v_bf, (((0,), (0,)), ((), ())), preferred_element_type=F32)
            so_ref[r, h] = state * g_h + outer
        ret = intra + cross * g_h
        mu = jnp.mean(ret, axis=-1, keepdims=True)
        rc = ret - mu
        var = jnp.mean(rc * rc, axis=-1, keepdims=True)
        normed = rc * lax.rsqrt(var + EPS) * gn_ref[:, cols]
        o_ref[:, W_A + h * DV_B:W_A + (h + 1) * DV_B] = g_ref[:, cols] * normed


def _mix_sample(proj, w0, b0, gn, state):
    t = proj.shape[1]
    rb = SAMPLE_ROWS

    def sec(s):
        return pl.BlockSpec((None, rb, SEC), lambda i, s=s: (s, i, 0))

    return pl.pallas_call(
        _mix_sample_kernel,
        grid=(t // rb,),
        in_specs=[sec(0), sec(1), sec(2), sec(3), sec(4), sec(5),
                  _resident((1, W_A)), _resident((1, W_A)), _resident((1, W_B)),
                  pl.BlockSpec((rb, H_B, DK_B, DV_B), lambda i: (i, 0, 0, 0))],
        out_specs=[pl.BlockSpec((rb, W_A + W_B), lambda i: (i, 0)),
                   pl.BlockSpec((rb, H_B, DK_B, DV_B), lambda i: (i, 0, 0, 0))],
        out_shape=[jax.ShapeDtypeStruct((t, W_A + W_B), F32),
                   jax.ShapeDtypeStruct(state.shape, F32)],
        compiler_params=_params(("arbitrary",)),
        name="mix_sample",
    )(proj, proj, proj, proj, proj, proj, w0, b0, gn, state)


SUBLANES = 8


def _attn_sample_rows(q_ref, ck_ref, cv_ref, o_ref, row0, n_rows):
    pairs = N_MEM * H_X
    fold = lambda t: pltpu.roll(t, H_X, 1)
    for r in range(n_rows):
        row = row0 + r
        q = q_ref[row]
        q8 = jnp.concatenate([q] * (SUBLANES // H_X), axis=0)
        k3 = ck_ref[0, r].reshape(pairs // SUBLANES, SUBLANES, DH_X)
        v3 = cv_ref[0, r].reshape(pairs // SUBLANES, SUBLANES, DH_X)
        s = jnp.sum(k3 * q8[None], axis=-1, keepdims=True) * (DH_X ** -0.5)
        m = jnp.max(s, axis=0, keepdims=True)
        m = jnp.maximum(m, fold(m))
        e = jnp.exp(s - m)
        l = jnp.sum(e, axis=0, keepdims=True)
        acc = jnp.sum(e * v3, axis=0, keepdims=True)
        o_ref[row] = ((acc + fold(acc)) / (l + fold(l)))[0, :H_X, :]


def _rope_tables(pos):
    half = DK_B // 2
    freqs = ROPE_THETA ** (-jnp.arange(half, dtype=F32) / half)
    ang = pos[:, None] * freqs[None, :]
    cos = jnp.cos(ang)
    sin = jnp.sin(ang)
    return jnp.concatenate([cos, cos], axis=-1), jnp.concatenate([-sin, sin], axis=-1)


def kernel(x_prompt, x_sample, mem_prompt, cache_mem_k, cache_mem_v, state_ret, norm1_g, w_in, sgu_norm_g, sgu_w_s, sgu_b, ret_gn_g, w_out, norm2_g, mem_norm_g, w_cq, w_ck, w_cv, w_co, norm3_g, w_ff1, w_ff2, final_norm_g):
    batch, seq, d = x_prompt.shape
    ts = x_sample.shape[0]
    n_chunks = seq // CHUNK
    tp = batch * seq

    g1 = norm1_g[0][None, :]
    g2 = norm2_g[0][None, :]
    g3 = norm3_g[0][None, :]
    gm = mem_norm_g[0][None, :]
    gf = final_norm_g[None, :]
    lng = sgu_norm_g[0][None, :]
    gn = ret_gn_g[0][None, :]
    ws = sgu_w_s[0]
    sb = sgu_b[0]
    xp = x_prompt.reshape(tp, d)
    xs = x_sample.reshape(ts, d)

    rope_p = _rope_tables(jnp.arange(seq, dtype=F32))
    rope_s = _rope_tables(jnp.full((ts,), PAST_LEN, dtype=F32))
    proj_p, proj_s = _in_proj(xp, xs, g1, w_in[0], rope_p, rope_s, lng, tm=1024)
    w0 = jnp.repeat(ws[:, 0, 0], CH_A)[None, :]
    b0 = jnp.repeat(sb[:, 0], CH_A)[None, :]
    mix_s, state_s = _mix_sample(proj_s, w0, b0, gn, state_ret[0])
    h1_s, q_s = _out_q_sample(mix_s.astype(BF16), xs, w_out[0], w_cq[0], g2, tk=256)
    mix_p, state_p, att_s = _mix_prompt(proj_p.reshape(batch, seq, N_SEC * SEC), ws, sb.T, gn,
                                        q_s.reshape(ts, H_X, DH_X), cache_mem_k, cache_mem_v,
                                        n_chunks=n_chunks)
    mix_p = mix_p.reshape(tp, W_A + W_B)
    att_s = att_s.reshape(ts, d)

    mem = mem_prompt.reshape(batch * N_MEM, d)
    mk5, mk = _mem_proj(mem, gm, w_ck[0], batch=batch, name="mem_k")
    mv5, mv = _mem_proj(mem, gm, w_cv[0], batch=batch, name="mem_v")
    tr = 512
    h1_p, q_p = _out_q(mix_p, xp, w_out[0], w_cq[0], g2, tm=tr)
    h2_p, xn3_p, h2_s, xn3_s = _attn_co(q_p, mk, mv, h1_p, att_s.astype(BF16), h1_s, w_co[0], g3,
                                        tm=tr, tiles_per_batch=seq // tr)

    hid_p, hid_s = _ff1(xn3_p, xn3_s, w_ff1[0], tm=2048, tn=1024)
    part_p, part_s = _ff2(hid_p, hid_s, w_ff2[0], tm=2048, tn=1024, tk=1024)
    y_p = _add_norm(part_p, h2_p, gf, tm=512)
    y_s = _add_norm(part_s.transpose(1, 0, 2).reshape(ts, d), h2_s, gf, tm=ts)

    return (y_p.reshape(batch, seq, d),
            y_s.reshape(ts, 1, d),
            mk5,
            mv5,
            state_p[None],
            state_s[None],
            proj_s[1].reshape(1, ts, 1, H_A, CH_A))
```

```python
import functools
import math

import jax
import jax.numpy as jnp
from jax import lax
from jax.experimental import pallas as pl
from jax.experimental.pallas import tpu as pltpu

F32 = jnp.float32
BF16 = jnp.bfloat16

D_MODEL = 2048
H_A = 8
CH_A = 128
W_A = H_A * CH_A
CHUNK = 128
H_B = 8
DK_B = 128
DV_B = 128
W_B = H_B * DV_B
ROPE_THETA = 10000.0
N_MEM = 256
H_X = 4
DH_X = D_MODEL // H_X
D_FF = 4 * D_MODEL
EPS = 1e-6
PAST_LEN = 16384

SEC = 1024
N_SEC = 6
LOG_G = tuple(math.log(1.0 - 2.0 ** (-5.0 - h)) for h in range(H_B))

V7X_VMEM_BYTES = 64 * 1024 * 1024
VMEM_LIMIT = V7X_VMEM_BYTES - 4 * 1024 * 1024
ROW_CHUNK = 256


def _params(sem):
    return pltpu.CompilerParams(dimension_semantics=sem, vmem_limit_bytes=VMEM_LIMIT)


def _resident(shape):
    zeros = (0,) * len(shape)
    return pl.BlockSpec(shape, lambda *_: zeros, pipeline_mode=pl.Buffered(1))


def _whole_out(shape):
    zeros = (0,) * len(shape)
    return pl.BlockSpec(shape, lambda *_: zeros)


def _row_loop(n_rows, fn):
    rc = min(n_rows, ROW_CHUNK)
    assert n_rows % rc == 0

    def body(c, carry):
        fn(pl.ds(pl.multiple_of(c * rc, rc), rc))
        return carry

    lax.fori_loop(0, n_rows // rc, body, 0)


def _rms(h, g_ref):
    ms = jnp.mean(h * h, axis=-1, keepdims=True)
    return h * lax.rsqrt(ms + EPS) * g_ref[...]


def _rms_rows(x_ref, g_ref, xn_ref, n_rows):
    def fn(rows):
        xn_ref[rows, :] = _rms(x_ref[rows, :], g_ref).astype(BF16)

    _row_loop(n_rows, fn)


def _cast_rows(w_ref, wb_ref):
    def fn(rows):
        wb_ref[rows, :] = w_ref[rows, :].astype(BF16)

    _row_loop(w_ref.shape[0], fn)


def _inproj_u_kernel(x_ref, g_ref, w_ref, xs_ref, u_ref, xn_ref, us_ref, xsn_ref, wb_ref, xn2_ref,
                     *, ts, n_tiles):
    s = pl.program_id(0)
    slot = s % 2

    def norm():
        xn = _rms(x_ref[...], g_ref).astype(BF16)
        xn_ref[...] = xn
        xn2_ref[slot] = xn

    def matmul():
        u_ref[...] = jax.nn.gelu(jnp.dot(xn2_ref[1 - slot], wb_ref[...],
                                         preferred_element_type=F32)).astype(BF16)

    @pl.when(s == 0)
    def _():
        _cast_rows(w_ref, wb_ref)
        _rms_rows(xs_ref, g_ref, xsn_ref, ts)
        us_ref[...] = jax.nn.gelu(jnp.dot(xsn_ref[...], wb_ref[...], preferred_element_type=F32))
        norm()

    @pl.when((s > 0) & (s < n_tiles))
    def _():
        norm()
        matmul()

    @pl.when(s == n_tiles)
    def _():
        matmul()


def _in_proj_u(x, xs, g, w, *, tm):
    t, d = x.shape
    ts = xs.shape[0]
    n_tiles = t // tm
    cur = lambda s: (jnp.minimum(s, n_tiles - 1), 0)
    prev = lambda s: (jnp.maximum(s - 1, 0), 0)
    return pl.pallas_call(
        functools.partial(_inproj_u_kernel, ts=ts, n_tiles=n_tiles),
        grid=(n_tiles + 1,),
        in_specs=[pl.BlockSpec((tm, d), cur), _resident((1, d)),
                  pl.BlockSpec((d, SEC), lambda s: (0, 0), pipeline_mode=pl.Buffered(1)),
                  _resident((ts, d))],
        out_specs=[pl.BlockSpec((tm, SEC), prev), pl.BlockSpec((tm, d), cur),
                   _whole_out((ts, SEC)), _whole_out((ts, d))],
        out_shape=[jax.ShapeDtypeStruct((t, SEC), BF16), jax.ShapeDtypeStruct((t, d), BF16),
                   jax.ShapeDtypeStruct((ts, SEC), F32), jax.ShapeDtypeStruct((ts, d), BF16)],
        scratch_shapes=[pltpu.VMEM((d, SEC), BF16), pltpu.VMEM((2, tm, d), BF16)],
        compiler_params=_params(("arbitrary",)),
        name="in_proj_u",
    )(x, g, w, xs)


def _inproj_rest_kernel(xn_ref, w_ref, cc_ref, ss_ref, lng_ref, xsn_ref, ccs_ref, sss_ref,
                        o_ref, os_ref, wb_ref):
    j = pl.program_id(0)
    i = pl.program_id(1)

    @pl.when(i == 0)
    def _():
        _cast_rows(w_ref, wb_ref)

    def layernorm(acc, cc, ss):
        z = jax.nn.gelu(acc)
        mu = jnp.mean(z, axis=-1, keepdims=True)
        zc = z - mu
        var = jnp.mean(zc * zc, axis=-1, keepdims=True)
        return zc * lax.rsqrt(var + EPS) * lng_ref[...]

    def rotary(acc, cc, ss):
        scale = jnp.where(j == 2, DK_B ** -0.5, 1.0).astype(F32)
        heads = []
        for h in range(H_B):
            blk = acc[:, h * DK_B:(h + 1) * DK_B]
            rot = pltpu.roll(blk, DK_B // 2, 1)
            heads.append((blk * cc + rot * ss) * scale)
        return jnp.concatenate(heads, axis=-1)

    def section(pred, epilogue):
        @pl.when(pred)
        def _():
            acc = jnp.dot(xn_ref[...], wb_ref[...], preferred_element_type=F32)
            o_ref[...] = epilogue(acc, cc_ref[...], ss_ref[...]).astype(o_ref.dtype)

            @pl.when(i == 0)
            def _():
                acc_s = jnp.dot(xsn_ref[...], wb_ref[...], preferred_element_type=F32)
                os_ref[j] = epilogue(acc_s, ccs_ref[...], sss_ref[...])

    section(j == 0, layernorm)
    section((j == 1) | (j == 2), rotary)
    section(j == 3, lambda acc, cc, ss: acc)
    section(j == 4, lambda acc, cc, ss: jax.nn.silu(acc))


def _in_proj_rest(xn, xsn, w, rope_p, rope_s, lng, *, tm):
    t, d = xn.shape
    ts = xsn.shape[0]
    n_rest = N_SEC - 1
    rope_blocks = rope_p[0].shape[0] // tm
    rope_spec = pl.BlockSpec((tm, DK_B), lambda j, i: (i % rope_blocks, 0))
    return pl.pallas_call(
        _inproj_rest_kernel,
        grid=(n_rest, t // tm),
        in_specs=[pl.BlockSpec((tm, d), lambda j, i: (i, 0)),
                  pl.BlockSpec((d, SEC), lambda j, i: (0, j + 1)),
                  rope_spec, rope_spec, _resident((1, SEC)),
                  _resident((ts, d)), _resident((ts, DK_B)), _resident((ts, DK_B))],
        out_specs=[pl.BlockSpec((tm, SEC), lambda j, i: (i, j)), _whole_out((n_rest, ts, SEC))],
        out_shape=[jax.ShapeDtypeStruct((t, n_rest * SEC), BF16),
                   jax.ShapeDtypeStruct((n_rest, ts, SEC), F32)],
        scratch_shapes=[pltpu.VMEM((d, SEC), BF16)],
        compiler_params=_params(("arbitrary", "arbitrary")),
        name="in_proj_rest",
    )(xn, w, rope_p[0], rope_p[1], lng, xsn, rope_s[0], rope_s[1])


def _mem_kernel(x_ref, g_ref, w_ref, o5_ref, o2_ref, wb_ref):
    @pl.when(pl.program_id(0) == 0)
    def _():
        _cast_rows(w_ref, wb_ref)

    res = jnp.dot(_rms(x_ref[...], g_ref).astype(BF16), wb_ref[...], preferred_element_type=F32)
    o2_ref[...] = res.astype(BF16)
    for h in range(H_X):
        o5_ref[0, 0, :, h, :] = res[:, h * DH_X:(h + 1) * DH_X]


def _mem_proj(mem, g, w, *, batch, name):
    d = mem.shape[1]
    return pl.pallas_call(
        _mem_kernel,
        grid=(batch,),
        in_specs=[pl.BlockSpec((N_MEM, d), lambda b: (b, 0)), _resident((1, d)), _resident((d, d))],
        out_specs=[pl.BlockSpec((1, 1, N_MEM, H_X, DH_X), lambda b: (0, b, 0, 0, 0)),
                   pl.BlockSpec((N_MEM, d), lambda b: (b, 0))],
        out_shape=[jax.ShapeDtypeStruct((1, batch, N_MEM, H_X, DH_X), F32),
                   jax.ShapeDtypeStruct((batch * N_MEM, d), BF16)],
        scratch_shapes=[pltpu.VMEM((d, d), BF16)],
        compiler_params=_params(("arbitrary",)),
        name=name,
    )(mem, g, w)


def _relu2(acc):
    return jnp.square(jnp.maximum(acc, 0.0))


def _ff1_kernel(x_ref, w_ref, xs_ref, o_ref, os_ref):
    o_ref[...] = _relu2(jnp.dot(x_ref[...], w_ref[...].astype(BF16), preferred_element_type=F32)).astype(BF16)

    @pl.when(pl.program_id(0) == 0)
    def _():
        os_ref[pl.program_id(1)] = _relu2(jnp.dot(xs_ref[...], w_ref[...].astype(BF16),
                                                  preferred_element_type=F32)).astype(BF16)


def _ff1(x, xs, w, *, tm, tn):
    t, k = x.shape
    ts = xs.shape[0]
    n = w.shape[1]
    return pl.pallas_call(
        _ff1_kernel,
        grid=(t // tm, n // tn),
        in_specs=[pl.BlockSpec((tm, k), lambda i, j: (i, 0)),
                  pl.BlockSpec((k, tn), lambda i, j: (0, j)),
                  _resident((ts, k))],
        out_specs=[pl.BlockSpec((tm, tn), lambda i, j: (i, j)), _whole_out((n // tn, ts, tn))],
        out_shape=[jax.ShapeDtypeStruct((t, n), BF16), jax.ShapeDtypeStruct((n // tn, ts, tn), BF16)],
        compiler_params=_params(("arbitrary", "arbitrary")),
        name="ff1",
    )(x, w, xs)


def _ff2_kernel(x_ref, w_ref, xs_ref, o_ref, os_ref):
    i, j, k = pl.program_id(0), pl.program_id(1), pl.program_id(2)

    @pl.when(k == 0)
    def _():
        o_ref[...] = jnp.dot(x_ref[...], w_ref[...].astype(BF16), preferred_element_type=F32)

    @pl.when(k > 0)
    def _():
        o_ref[...] += jnp.dot(x_ref[...], w_ref[...].astype(BF16), preferred_element_type=F32)

    @pl.when(i == 0)
    def _():
        part = jnp.dot(xs_ref[k], w_ref[...].astype(BF16), preferred_element_type=F32)

        @pl.when(k == 0)
        def _():
            os_ref[j] = part

        @pl.when(k > 0)
        def _():
            os_ref[j] += part


def _ff2(x, xs, w, *, tm, tn, tk):
    t, k = x.shape
    ts = xs.shape[1]
    n = w.shape[1]
    return pl.pallas_call(
        _ff2_kernel,
        grid=(t // tm, n // tn, k // tk),
        in_specs=[pl.BlockSpec((tm, tk), lambda i, j, kk: (i, kk)),
                  pl.BlockSpec((tk, tn), lambda i, j, kk: (kk, j)),
                  _resident((k // tk, ts, tk))],
        out_specs=[pl.BlockSpec((tm, tn), lambda i, j, kk: (i, j)), _whole_out((n // tn, ts, tn))],
        out_shape=[jax.ShapeDtypeStruct((t, n), F32), jax.ShapeDtypeStruct((n // tn, ts, tn), F32)],
        compiler_params=_params(("arbitrary", "arbitrary", "arbitrary")),
        name="ff2",
    )(x, w, xs)


def _add_norm_kernel(x_ref, r_ref, g_ref, o_ref, *, tm):
    def fn(rows):
        o_ref[rows, :] = _rms(r_ref[rows, :] + x_ref[rows, :], g_ref)

    _row_loop(tm, fn)


def _add_norm(x, r, g, *, tm):
    t, d = x.shape
    row = pl.BlockSpec((tm, d), lambda i: (i, 0))
    return pl.pallas_call(
        functools.partial(_add_norm_kernel, tm=tm),
        grid=(t // tm,),
        in_specs=[row, row, _resident((1, d))],
        out_specs=row,
        out_shape=jax.ShapeDtypeStruct((t, d), F32),
        compiler_params=_params(("arbitrary",)),
        name="final_norm",
    )(x, r, g)


W_CHUNK = 256
N_LOAD = D_MODEL // W_CHUNK


def _load_weight_chunk(s, pairs):
    rows = pl.ds(pl.multiple_of(s * W_CHUNK, W_CHUNK), W_CHUNK)
    for src, dst in pairs:
        dst[rows, :] = src[...].astype(BF16)


def _w_chunk_spec():
    return pl.BlockSpec((W_CHUNK, D_MODEL), lambda s: (jnp.minimum(s, N_LOAD - 1), 0))


def _tile_spec(tm, width=D_MODEL):
    return pl.BlockSpec((tm, width), lambda s: (jnp.maximum(s - N_LOAD, 0), 0))


def _outq_kernel(mix_ref, x_ref, wo_ref, wq_ref, g2_ref, mixs_ref, xs_ref,
                 h1_ref, q_ref, h1s_ref, qs_ref, wo_bf, wq_bf):
    s = pl.program_id(0)

    def block(mix_r, x_r, h1_r, q_r):
        h1 = x_r[...] + jnp.dot(mix_r[...], wo_bf[...], preferred_element_type=F32)
        h1_r[...] = h1
        q_r[...] = jnp.dot(_rms(h1, g2_ref).astype(BF16), wq_bf[...],
                           preferred_element_type=F32).astype(q_r.dtype)

    @pl.when(s < N_LOAD)
    def _():
        _load_weight_chunk(s, ((wo_ref, wo_bf), (wq_ref, wq_bf)))

    @pl.when(s >= N_LOAD)
    def _():
        block(mix_ref, x_ref, h1_ref, q_ref)

    @pl.when(s == N_LOAD)
    def _():
        block(mixs_ref, xs_ref, h1s_ref, qs_ref)


def _out_q(mix, x, mix_s, xs, w_out, w_cq, g2, *, tm):
    t, d = x.shape
    ts = xs.shape[0]
    return pl.pallas_call(
        _outq_kernel,
        grid=(N_LOAD + t // tm,),
        in_specs=[_tile_spec(tm), _tile_spec(tm), _w_chunk_spec(), _w_chunk_spec(), _resident((1, d)),
                  _resident((ts, d)), _resident((ts, d))],
        out_specs=[_tile_spec(tm), _tile_spec(tm), _whole_out((ts, d)), _whole_out((ts, d))],
        out_shape=[jax.ShapeDtypeStruct((t, d), F32), jax.ShapeDtypeStruct((t, d), BF16),
                   jax.ShapeDtypeStruct((ts, d), F32), jax.ShapeDtypeStruct((ts, d), F32)],
        scratch_shapes=[pltpu.VMEM((d, d), BF16), pltpu.VMEM((d, d), BF16)],
        compiler_params=_params(("arbitrary",)),
        name="out_q",
    )(mix, x, w_out, w_cq, g2, mix_s, xs)


def _softmax_rows(s):
    m = jnp.max(s, axis=-1, keepdims=True)
    e = jnp.exp(s - m)
    return e / jnp.sum(e, axis=-1, keepdims=True)


def _attn_co_kernel(q_ref, mk_ref, mv_ref, h1_ref, wc_ref, g3_ref, atts_ref, h1s_ref,
                    h2_ref, xn_ref, h2s_ref, xns_ref, wc_bf, att_ref):
    s = pl.program_id(0)

    def co_block(att_r, h1_r, h2_r, xn_r):
        h2 = h1_r[...] + jnp.dot(att_r[...], wc_bf[...], preferred_element_type=F32)
        h2_r[...] = h2
        xn_r[...] = _rms(h2, g3_ref).astype(BF16)

    @pl.when(s < N_LOAD)
    def _():
        _load_weight_chunk(s, ((wc_ref, wc_bf),))

    @pl.when(s >= N_LOAD)
    def _():
        for h in range(H_X):
            cols = slice(h * DH_X, (h + 1) * DH_X)
            sc = lax.dot_general(q_ref[:, cols], mk_ref[:, cols], (((1,), (1,)), ((), ())),
                                 preferred_element_type=F32) * (DH_X ** -0.5)
            p = _softmax_rows(sc)
            att_ref[:, cols] = jnp.dot(p.astype(BF16), mv_ref[:, cols],
                                       preferred_element_type=F32).astype(BF16)
        co_block(att_ref, h1_ref, h2_ref, xn_ref)

    @pl.when(s == N_LOAD)
    def _():
        co_block(atts_ref, h1s_ref, h2s_ref, xns_ref)


def _attn_co(q, mk, mv, h1, att_s, h1_s, w_co, g3, *, tm, tiles_per_batch):
    t, d = h1.shape
    ts = h1_s.shape[0]
    mem_spec = pl.BlockSpec((N_MEM, d), lambda s: (jnp.maximum(s - N_LOAD, 0) // tiles_per_batch, 0))
    return pl.pallas_call(
        _attn_co_kernel,
        grid=(N_LOAD + t // tm,),
        in_specs=[_tile_spec(tm), mem_spec, mem_spec, _tile_spec(tm), _w_chunk_spec(), _resident((1, d)),
                  _resident((ts, d)), _resident((ts, d))],
        out_specs=[_tile_spec(tm), _tile_spec(tm), _whole_out((ts, d)), _whole_out((ts, d))],
        out_shape=[jax.ShapeDtypeStruct((t, d), F32), jax.ShapeDtypeStruct((t, d), BF16),
                   jax.ShapeDtypeStruct((ts, d), F32), jax.ShapeDtypeStruct((ts, d), BF16)],
        scratch_shapes=[pltpu.VMEM((d, d), BF16), pltpu.VMEM((tm, d), BF16)],
        compiler_params=_params(("arbitrary",)),
        name="attn_co",
    )(q, mk, mv, h1, w_co, g3, att_s, h1_s)


def _mix_prompt_kernel(u_ref, v_ref, q_ref, k_ref, vb_ref, g_ref, ws_ref, bt_ref, gn_ref,
                       o_ref, so_ref, st_ref, wt_ref, bias_ref, dm_ref, qd_ref, kd_ref, *, batch, n_chunks):
    n = pl.program_id(0)

    @pl.when(n == 0)
    def _():
        ii = lax.broadcasted_iota(jnp.int32, (CHUNK, CHUNK), 0)
        jj = lax.broadcasted_iota(jnp.int32, (CHUNK, CHUNK), 1)
        causal = ii >= jj
        diff = jnp.maximum((ii - jj).astype(F32), 0.0)
        ridx = ii.astype(F32)
        for h in range(H_A):
            lg = LOG_G[h]
            wt_ref[h] = (ws_ref[h] * causal.astype(F32)).astype(BF16)
            bias_ref[h] = jnp.broadcast_to(bt_ref[:, h:h + 1], (CHUNK, CH_A))
            dm_ref[h] = jnp.where(causal, jnp.exp(lg * diff), 0.0)
            qd_ref[h] = jnp.exp(lg * (ridx + 1.0))
            kd_ref[h] = jnp.exp(lg * (CHUNK - 1.0 - ridx))
        st_ref[...] = jnp.zeros_like(st_ref)

    for b in range(batch):
        for h in range(H_A):
            cols = slice(h * CH_A, (h + 1) * CH_A)
            mixed = jnp.dot(wt_ref[h], v_ref[b, :, cols], preferred_element_type=F32) + bias_ref[h]
            o_ref[b, :, cols] = (u_ref[b, :, cols].astype(F32) * mixed).astype(BF16)

            kh = k_ref[b, :, cols]
            vh = vb_ref[b, :, cols]
            state = st_ref[b, h]
            state_bf = state.astype(BF16)
            qh = q_ref[b, :, cols]
            scores = lax.dot_general(qh, kh, (((1,), (1,)), ((), ())), preferred_element_type=F32) * dm_ref[h]
            intra = jnp.dot(scores.astype(BF16), vh, preferred_element_type=F32)
            cross = jnp.dot(qh, state_bf, preferred_element_type=F32) * qd_ref[h]
            ret = intra + cross
            mu = jnp.mean(ret, axis=-1, keepdims=True)
            rc = ret - mu
            var = jnp.mean(rc * rc, axis=-1, keepdims=True)
            normed = rc * lax.rsqrt(var + EPS) * gn_ref[:, cols]
            o_ref[b, :, W_A + h * DV_B:W_A + (h + 1) * DV_B] = (
                g_ref[b, :, cols].astype(F32) * normed).astype(BF16)
            kd = (kh.astype(F32) * kd_ref[h]).astype(BF16)
            st_ref[b, h] = state * math.exp(LOG_G[h] * CHUNK) + lax.dot_general(
                kd, vh, (((0,), (0,)), ((), ())), preferred_element_type=F32)

    @pl.when(n == n_chunks - 1)
    def _():
        so_ref[...] = st_ref[...]


def _mix_prompt(u, rest, ws, bt, gn, *, n_chunks):
    batch, seq, _ = u.shape

    def sec(s):
        return pl.BlockSpec((batch, CHUNK, SEC), lambda n, s=s: (0, n, s))

    table = pltpu.VMEM((H_B, CHUNK, CHUNK), F32)
    return pl.pallas_call(
        functools.partial(_mix_prompt_kernel, batch=batch, n_chunks=n_chunks),
        grid=(n_chunks,),
        in_specs=[sec(0), sec(0), sec(1), sec(2), sec(3), sec(4),
                  _resident((H_A, CHUNK, CHUNK)), _resident((CHUNK, H_A)), _resident((1, W_B))],
        out_specs=[pl.BlockSpec((batch, CHUNK, W_A + W_B), lambda n: (0, n, 0)),
                   _whole_out((batch, H_B, DK_B, DV_B))],
        out_shape=[jax.ShapeDtypeStruct((batch, seq, W_A + W_B), BF16),
                   jax.ShapeDtypeStruct((batch, H_B, DK_B, DV_B), F32)],
        scratch_shapes=[pltpu.VMEM((batch, H_B, DK_B, DV_B), F32), pltpu.VMEM((H_A, CHUNK, CHUNK), BF16),
                        table, table, table, table],
        compiler_params=_params(("arbitrary",)),
        name="mix_prompt",
    )(u, rest, rest, rest, rest, rest, ws, bt, gn)


SAMPLE_ROWS = 8


def _mix_sample_kernel(u_ref, v_ref, q_ref, k_ref, vb_ref, g_ref, w0_ref, b0_ref, gn_ref, s_ref,
                       o_ref, so_ref):
    rid = lax.broadcasted_iota(jnp.int32, (SAMPLE_ROWS, 1), 0)
    o_ref[:, :W_A] = u_ref[...] * (w0_ref[...] * v_ref[...] + b0_ref[...])

    for h in range(H_B):
        cols = slice(h * DK_B, (h + 1) * DK_B)
        g_h = math.exp(LOG_G[h])
        q_blk = q_ref[:, cols]
        k_blk = k_ref[:, cols]
        v_blk = vb_ref[:, cols]
        v_bf = v_blk.astype(BF16)
        intra = jnp.sum(q_blk * k_blk, axis=-1, keepdims=True) * v_blk
        cross = jnp.zeros((SAMPLE_ROWS, DV_B), F32)
        for r in range(SAMPLE_ROWS):
            state = s_ref[r, h]
            q_only_r = jnp.where(rid == r, q_blk, 0.0).astype(BF16)
            cross = cross + jnp.dot(q_only_r, state.astype(BF16), preferred_element_type=F32)
            k_only_r = jnp.where(rid == r, k_blk, 0.0).astype(BF16)
            outer = lax.dot_general(k_only_r, v_bf, (((0,), (0,)), ((), ())), preferred_element_type=F32)
            so_ref[r, h] = state * g_h + outer
        ret = intra + cross * g_h
        mu = jnp.mean(ret, axis=-1, keepdims=True)
        rc = ret - mu
        var = jnp.mean(rc * rc, axis=-1, keepdims=True)
        normed = rc * lax.rsqrt(var + EPS) * gn_ref[:, cols]
        o_ref[:, W_A + h * DV_B:W_A + (h + 1) * DV_B] = g_ref[:, cols] * normed


def _mix_sample(u, rest, w0, b0, gn, state):
    t = u.shape[0]
    rb = SAMPLE_ROWS

    def sec(s):
        return pl.BlockSpec((None, rb, SEC), lambda i, s=s: (s, i, 0))

    return pl.pallas_call(
        _mix_sample_kernel,
        grid=(t // rb,),
        in_specs=[pl.BlockSpec((rb, SEC), lambda i: (i, 0)), sec(0), sec(1), sec(2), sec(3), sec(4),
                  _resident((1, W_A)), _resident((1, W_A)), _resident((1, W_B)),
                  pl.BlockSpec((rb, H_B, DK_B, DV_B), lambda i: (i, 0, 0, 0))],
        out_specs=[pl.BlockSpec((rb, W_A + W_B), lambda i: (i, 0)),
                   pl.BlockSpec((rb, H_B, DK_B, DV_B), lambda i: (i, 0, 0, 0))],
        out_shape=[jax.ShapeDtypeStruct((t, W_A + W_B), F32),
                   jax.ShapeDtypeStruct(state.shape, F32)],
        compiler_params=_params(("arbitrary",)),
        name="mix_sample",
    )(u, rest, rest, rest, rest, rest, w0, b0, gn, state)


ATTN_SAMPLE_ROWS = 4
SUBLANES = 8


def _attn_sample_kernel(q_ref, ck_ref, cv_ref, o_ref):
    i = pl.program_id(0)
    pairs = N_MEM * H_X
    fold = lambda t: pltpu.roll(t, H_X, 1)
    for r in range(ATTN_SAMPLE_ROWS):
        row = i * ATTN_SAMPLE_ROWS + r
        q = q_ref[row]
        q8 = jnp.concatenate([q] * (SUBLANES // H_X), axis=0)
        k3 = ck_ref[0, r].reshape(pairs // SUBLANES, SUBLANES, DH_X)
        v3 = cv_ref[0, r].reshape(pairs // SUBLANES, SUBLANES, DH_X)
        s = jnp.sum(k3 * q8[None], axis=-1, keepdims=True) * (DH_X ** -0.5)
        m = jnp.max(s, axis=0, keepdims=True)
        m = jnp.maximum(m, fold(m))
        e = jnp.exp(s - m)
        l = jnp.sum(e, axis=0, keepdims=True)
        acc = jnp.sum(e * v3, axis=0, keepdims=True)
        o_ref[row] = ((acc + fold(acc)) / (l + fold(l)))[0, :H_X, :]


def _attn_sample(q, ck, cv):
    t = q.shape[0]
    rb = ATTN_SAMPLE_ROWS
    cache_spec = pl.BlockSpec((1, rb, N_MEM, H_X, DH_X), lambda i: (0, i, 0, 0, 0))
    return pl.pallas_call(
        _attn_sample_kernel,
        grid=(t // rb,),
        in_specs=[_resident((t, H_X, DH_X)), cache_spec, cache_spec],
        out_specs=_whole_out((t, H_X, DH_X)),
        out_shape=jax.ShapeDtypeStruct((t, H_X, DH_X), F32),
        compiler_params=_params(("arbitrary",)),
        name="attn_sample",
    )(q, ck, cv)


def _rope_tables(pos):
    half = DK_B // 2
    freqs = ROPE_THETA ** (-jnp.arange(half, dtype=F32) / half)
    ang = pos[:, None] * freqs[None, :]
    cos = jnp.cos(ang)
    sin = jnp.sin(ang)
    return jnp.concatenate([cos, cos], axis=-1), jnp.concatenate([-sin, sin], axis=-1)


def kernel(x_prompt, x_sample, mem_prompt, cache_mem_k, cache_mem_v, state_ret, norm1_g, w_in, sgu_norm_g, sgu_w_s, sgu_b, ret_gn_g, w_out, norm2_g, mem_norm_g, w_cq, w_ck, w_cv, w_co, norm3_g, w_ff1, w_ff2, final_norm_g):
    batch, seq, d = x_prompt.shape
    ts = x_sample.shape[0]
    n_chunks = seq // CHUNK
    tp = batch * seq

    g1 = norm1_g[0][None, :]
    g2 = norm2_g[0][None, :]
    g3 = norm3_g[0][None, :]
    gm = mem_norm_g[0][None, :]
    gf = final_norm_g[None, :]
    lng = sgu_norm_g[0][None, :]
    gn = ret_gn_g[0][None, :]
    ws = sgu_w_s[0]
    sb = sgu_b[0]
    xp = x_prompt.reshape(tp, d)
    xs = x_sample.reshape(ts, d)

    rope_p = _rope_tables(jnp.arange(seq, dtype=F32))
    rope_s = _rope_tables(jnp.full((ts,), PAST_LEN, dtype=F32))
    u_p, xn_p, u_s, xn_s = _in_proj_u(xp, xs, g1, w_in[0], tm=1024)
    rest_p, rest_s = _in_proj_rest(xn_p, xn_s, w_in[0], rope_p, rope_s, lng, tm=1024)
    mix_p, state_p = _mix_prompt(u_p.reshape(batch, seq, SEC), rest_p.reshape(batch, seq, (N_SEC - 1) * SEC),
                                 ws, sb.T, gn, n_chunks=n_chunks)
    mix_p = mix_p.reshape(tp, W_A + W_B)
    w0 = jnp.repeat(ws[:, 0, 0], CH_A)[None, :]
    b0 = jnp.repeat(sb[:, 0], CH_A)[None, :]
    mix_s, state_s = _mix_sample(u_s, rest_s, w0, b0, gn, state_ret[0])

    mem = mem_prompt.reshape(batch * N_MEM, d)
    mk5, mk = _mem_proj(mem, gm, w_ck[0], batch=batch, name="mem_k")
    mv5, mv = _mem_proj(mem, gm, w_cv[0], batch=batch, name="mem_v")
    tr = 512
    h1_p, q_p, h1_s, q_s = _out_q(mix_p, xp, mix_s.astype(BF16), xs, w_out[0], w_cq[0], g2, tm=tr)
    att_s = _attn_sample(q_s.reshape(ts, H_X, DH_X), cache_mem_k, cache_mem_v).reshape(ts, d)
    h2_p, xn3_p, h2_s, xn3_s = _attn_co(q_p, mk, mv, h1_p, att_s.astype(BF16), h1_s, w_co[0], g3,
                                        tm=tr, tiles_per_batch=seq // tr)

    hid_p, hid_s = _ff1(xn3_p, xn3_s, w_ff1[0], tm=2048, tn=1024)
    part_p, part_s = _ff2(hid_p, hid_s, w_ff2[0], tm=2048, tn=1024, tk=1024)
    y_p = _add_norm(part_p, h2_p, gf, tm=1024)
    y_s = _add_norm(part_s.transpose(1, 0, 2).reshape(ts, d), h2_s, gf, tm=ts)

    return (y_p.reshape(batch, seq, d),
            y_s.reshape(ts, 1, d),
            mk5,
            mv5,
            state_p[None],
            state_s[None],
            rest_s[0].reshape(1, ts, 1, H_A, CH_A))
```

```python
import functools
import math

import jax
import jax.numpy as jnp
from jax import lax
from jax.experimental import pallas as pl
from jax.experimental.pallas import tpu as pltpu

F32 = jnp.float32
BF16 = jnp.bfloat16

D_MODEL = 2048
H_A = 8
CH_A = 128
W_A = H_A * CH_A
CHUNK = 128
H_B = 8
DK_B = 128
DV_B = 128
W_B = H_B * DV_B
ROPE_THETA = 10000.0
N_MEM = 256
H_X = 4
DH_X = D_MODEL // H_X
D_FF = 4 * D_MODEL
EPS = 1e-6
PAST_LEN = 16384

SEC = 1024
N_SEC = 6
LOG_G = tuple(math.log(1.0 - 2.0 ** (-5.0 - h)) for h in range(H_B))

V7X_VMEM_BYTES = 64 * 1024 * 1024
VMEM_LIMIT = V7X_VMEM_BYTES - 4 * 1024 * 1024
ROW_CHUNK = 256


def _params(sem):
    return pltpu.CompilerParams(dimension_semantics=sem, vmem_limit_bytes=VMEM_LIMIT)


def _resident(shape):
    zeros = (0,) * len(shape)
    return pl.BlockSpec(shape, lambda *_: zeros, pipeline_mode=pl.Buffered(1))


def _whole_out(shape):
    zeros = (0,) * len(shape)
    return pl.BlockSpec(shape, lambda *_: zeros)


def _row_loop(n_rows, fn):
    rc = min(n_rows, ROW_CHUNK)
    assert n_rows % rc == 0

    def body(c, carry):
        fn(pl.ds(pl.multiple_of(c * rc, rc), rc))
        return carry

    lax.fori_loop(0, n_rows // rc, body, 0)


def _rms(h, g_ref):
    ms = jnp.mean(h * h, axis=-1, keepdims=True)
    return h * lax.rsqrt(ms + EPS) * g_ref[...]


def _rms_rows(x_ref, g_ref, xn_ref, n_rows):
    def fn(rows):
        xn_ref[rows, :] = _rms(x_ref[rows, :], g_ref).astype(BF16)

    _row_loop(n_rows, fn)


def _cast_rows(w_ref, wb_ref):
    def fn(rows):
        wb_ref[rows, :] = w_ref[rows, :].astype(BF16)

    _row_loop(w_ref.shape[0], fn)


def _inproj_u_kernel(x_ref, g_ref, w_ref, xs_ref, u_ref, xn_ref, us_ref, xsn_ref, wb_ref, *, tm, ts):
    @pl.when(pl.program_id(0) == 0)
    def _():
        _cast_rows(w_ref, wb_ref)
        _rms_rows(xs_ref, g_ref, xsn_ref, ts)
        us_ref[...] = jax.nn.gelu(jnp.dot(xsn_ref[...], wb_ref[...], preferred_element_type=F32))

    _rms_rows(x_ref, g_ref, xn_ref, tm)
    u_ref[...] = jax.nn.gelu(jnp.dot(xn_ref[...], wb_ref[...], preferred_element_type=F32)).astype(BF16)


def _in_proj_u(x, xs, g, w, *, tm):
    t, d = x.shape
    ts = xs.shape[0]
    return pl.pallas_call(
        functools.partial(_inproj_u_kernel, tm=tm, ts=ts),
        grid=(t // tm,),
        in_specs=[pl.BlockSpec((tm, d), lambda i: (i, 0)), _resident((1, d)),
                  pl.BlockSpec((d, SEC), lambda i: (0, 0), pipeline_mode=pl.Buffered(1)),
                  _resident((ts, d))],
        out_specs=[pl.BlockSpec((tm, SEC), lambda i: (i, 0)), pl.BlockSpec((tm, d), lambda i: (i, 0)),
                   _whole_out((ts, SEC)), _whole_out((ts, d))],
        out_shape=[jax.ShapeDtypeStruct((t, SEC), BF16), jax.ShapeDtypeStruct((t, d), BF16),
                   jax.ShapeDtypeStruct((ts, SEC), F32), jax.ShapeDtypeStruct((ts, d), BF16)],
        scratch_shapes=[pltpu.VMEM((d, SEC), BF16)],
        compiler_params=_params(("arbitrary",)),
        name="in_proj_u",
    )(x, g, w, xs)


def _inproj_rest_kernel(xn_ref, w_ref, cc_ref, ss_ref, lng_ref, xsn_ref, ccs_ref, sss_ref,
                        o_ref, os_ref, wb_ref):
    j = pl.program_id(0)
    i = pl.program_id(1)

    @pl.when(i == 0)
    def _():
        _cast_rows(w_ref, wb_ref)

    def layernorm(acc, cc, ss):
        z = jax.nn.gelu(acc)
        mu = jnp.mean(z, axis=-1, keepdims=True)
        zc = z - mu
        var = jnp.mean(zc * zc, axis=-1, keepdims=True)
        return zc * lax.rsqrt(var + EPS) * lng_ref[...]

    def rotary(acc, cc, ss):
        scale = jnp.where(j == 2, DK_B ** -0.5, 1.0).astype(F32)
        heads = []
        for h in range(H_B):
            blk = acc[:, h * DK_B:(h + 1) * DK_B]
            rot = pltpu.roll(blk, DK_B // 2, 1)
            heads.append((blk * cc + rot * ss) * scale)
        return jnp.concatenate(heads, axis=-1)

    def section(pred, epilogue):
        @pl.when(pred)
        def _():
            acc = jnp.dot(xn_ref[...], wb_ref[...], preferred_element_type=F32)
            o_ref[...] = epilogue(acc, cc_ref[...], ss_ref[...]).astype(o_ref.dtype)

            @pl.when(i == 0)
            def _():
                acc_s = jnp.dot(xsn_ref[...], wb_ref[...], preferred_element_type=F32)
                os_ref[j] = epilogue(acc_s, ccs_ref[...], sss_ref[...])

    section(j == 0, layernorm)
    section((j == 1) | (j == 2), rotary)
    section(j == 3, lambda acc, cc, ss: acc)
    section(j == 4, lambda acc, cc, ss: jax.nn.silu(acc))


def _in_proj_rest(xn, xsn, w, rope_p, rope_s, lng, *, tm):
    t, d = xn.shape
    ts = xsn.shape[0]
    n_rest = N_SEC - 1
    rope_blocks = rope_p[0].shape[0] // tm
    rope_spec = pl.BlockSpec((tm, DK_B), lambda j, i: (i % rope_blocks, 0))
    return pl.pallas_call(
        _inproj_rest_kernel,
        grid=(n_rest, t // tm),
        in_specs=[pl.BlockSpec((tm, d), lambda j, i: (i, 0)),
                  pl.BlockSpec((d, SEC), lambda j, i: (0, j + 1)),
                  rope_spec, rope_spec, _resident((1, SEC)),
                  _resident((ts, d)), _resident((ts, DK_B)), _resident((ts, DK_B))],
        out_specs=[pl.BlockSpec((tm, SEC), lambda j, i: (i, j)), _whole_out((n_rest, ts, SEC))],
        out_shape=[jax.ShapeDtypeStruct((t, n_rest * SEC), BF16),
                   jax.ShapeDtypeStruct((n_rest, ts, SEC), F32)],
        scratch_shapes=[pltpu.VMEM((d, SEC), BF16)],
        compiler_params=_params(("arbitrary", "arbitrary")),
        name="in_proj_rest",
    )(xn, w, rope_p[0], rope_p[1], lng, xsn, rope_s[0], rope_s[1])


def _mem_kernel(x_ref, g_ref, w_ref, o5_ref, o2_ref, wb_ref):
    @pl.when(pl.program_id(0) == 0)
    def _():
        _cast_rows(w_ref, wb_ref)

    res = jnp.dot(_rms(x_ref[...], g_ref).astype(BF16), wb_ref[...], preferred_element_type=F32)
    o2_ref[...] = res.astype(BF16)
    for h in range(H_X):
        o5_ref[0, 0, :, h, :] = res[:, h * DH_X:(h + 1) * DH_X]


def _mem_proj(mem, g, w, *, batch, name):
    d = mem.shape[1]
    return pl.pallas_call(
        _mem_kernel,
        grid=(batch,),
        in_specs=[pl.BlockSpec((N_MEM, d), lambda b: (b, 0)), _resident((1, d)), _resident((d, d))],
        out_specs=[pl.BlockSpec((1, 1, N_MEM, H_X, DH_X), lambda b: (0, b, 0, 0, 0)),
                   pl.BlockSpec((N_MEM, d), lambda b: (b, 0))],
        out_shape=[jax.ShapeDtypeStruct((1, batch, N_MEM, H_X, DH_X), F32),
                   jax.ShapeDtypeStruct((batch * N_MEM, d), BF16)],
        scratch_shapes=[pltpu.VMEM((d, d), BF16)],
        compiler_params=_params(("arbitrary",)),
        name=name,
    )(mem, g, w)


def _relu2(acc):
    return jnp.square(jnp.maximum(acc, 0.0))


def _ff1_kernel(x_ref, w_ref, xs_ref, o_ref, os_ref):
    o_ref[...] = _relu2(jnp.dot(x_ref[...], w_ref[...].astype(BF16), preferred_element_type=F32)).astype(BF16)

    @pl.when(pl.program_id(0) == 0)
    def _():
        os_ref[pl.program_id(1)] = _relu2(jnp.dot(xs_ref[...], w_ref[...].astype(BF16),
                                                  preferred_element_type=F32)).astype(BF16)


def _ff1(x, xs, w, *, tm, tn):
    t, k = x.shape
    ts = xs.shape[0]
    n = w.shape[1]
    return pl.pallas_call(
        _ff1_kernel,
        grid=(t // tm, n // tn),
        in_specs=[pl.BlockSpec((tm, k), lambda i, j: (i, 0)),
                  pl.BlockSpec((k, tn), lambda i, j: (0, j)),
                  _resident((ts, k))],
        out_specs=[pl.BlockSpec((tm, tn), lambda i, j: (i, j)), _whole_out((n // tn, ts, tn))],
        out_shape=[jax.ShapeDtypeStruct((t, n), BF16), jax.ShapeDtypeStruct((n // tn, ts, tn), BF16)],
        compiler_params=_params(("arbitrary", "arbitrary")),
        name="ff1",
    )(x, w, xs)


def _ff2_kernel(x_ref, w_ref, xs_ref, o_ref, os_ref):
    i, j, k = pl.program_id(0), pl.program_id(1), pl.program_id(2)

    @pl.when(k == 0)
    def _():
        o_ref[...] = jnp.zeros_like(o_ref)

    o_ref[...] += jnp.dot(x_ref[...], w_ref[...].astype(BF16), preferred_element_type=F32)

    @pl.when(i == 0)
    def _():
        part = jnp.dot(xs_ref[k], w_ref[...].astype(BF16), preferred_element_type=F32)

        @pl.when(k == 0)
        def _():
            os_ref[j] = part

        @pl.when(k > 0)
        def _():
            os_ref[j] += part


def _ff2(x, xs, w, *, tm, tn, tk):
    t, k = x.shape
    ts = xs.shape[1]
    n = w.shape[1]
    return pl.pallas_call(
        _ff2_kernel,
        grid=(t // tm, n // tn, k // tk),
        in_specs=[pl.BlockSpec((tm, tk), lambda i, j, kk: (i, kk)),
                  pl.BlockSpec((tk, tn), lambda i, j, kk: (kk, j)),
                  _resident((k // tk, ts, tk))],
        out_specs=[pl.BlockSpec((tm, tn), lambda i, j, kk: (i, j)), _whole_out((n // tn, ts, tn))],
        out_shape=[jax.ShapeDtypeStruct((t, n), F32), jax.ShapeDtypeStruct((n // tn, ts, tn), F32)],
        compiler_params=_params(("arbitrary", "arbitrary", "arbitrary")),
        name="ff2",
    )(x, w, xs)


def _add_norm_kernel(x_ref, r_ref, g_ref, o_ref, *, tm):
    def fn(rows):
        o_ref[rows, :] = _rms(r_ref[rows, :] + x_ref[rows, :], g_ref)

    _row_loop(tm, fn)


def _add_norm(x, r, g, *, tm):
    t, d = x.shape
    row = pl.BlockSpec((tm, d), lambda i: (i, 0))
    return pl.pallas_call(
        functools.partial(_add_norm_kernel, tm=tm),
        grid=(t // tm,),
        in_specs=[row, row, _resident((1, d))],
        out_specs=row,
        out_shape=jax.ShapeDtypeStruct((t, d), F32),
        compiler_params=_params(("arbitrary",)),
        name="final_norm",
    )(x, r, g)


W_CHUNK = 256
N_LOAD = D_MODEL // W_CHUNK


def _load_weight_chunk(s, pairs):
    rows = pl.ds(pl.multiple_of(s * W_CHUNK, W_CHUNK), W_CHUNK)
    for src, dst in pairs:
        dst[rows, :] = src[...].astype(BF16)


def _w_chunk_spec():
    return pl.BlockSpec((W_CHUNK, D_MODEL), lambda s: (jnp.minimum(s, N_LOAD - 1), 0))


def _tile_spec(tm, width=D_MODEL):
    return pl.BlockSpec((tm, width), lambda s: (jnp.maximum(s - N_LOAD, 0), 0))


def _outq_kernel(mix_ref, x_ref, wo_ref, wq_ref, g2_ref, mixs_ref, xs_ref,
                 h1_ref, q_ref, h1s_ref, qs_ref, wo_bf, wq_bf):
    s = pl.program_id(0)

    def block(mix_r, x_r, h1_r, q_r):
        h1 = x_r[...] + jnp.dot(mix_r[...], wo_bf[...], preferred_element_type=F32)
        h1_r[...] = h1
        q_r[...] = jnp.dot(_rms(h1, g2_ref).astype(BF16), wq_bf[...],
                           preferred_element_type=F32).astype(q_r.dtype)

    @pl.when(s < N_LOAD)
    def _():
        _load_weight_chunk(s, ((wo_ref, wo_bf), (wq_ref, wq_bf)))

    @pl.when(s >= N_LOAD)
    def _():
        block(mix_ref, x_ref, h1_ref, q_ref)

    @pl.when(s == N_LOAD)
    def _():
        block(mixs_ref, xs_ref, h1s_ref, qs_ref)


def _out_q(mix, x, mix_s, xs, w_out, w_cq, g2, *, tm):
    t, d = x.shape
    ts = xs.shape[0]
    return pl.pallas_call(
        _outq_kernel,
        grid=(N_LOAD + t // tm,),
        in_specs=[_tile_spec(tm), _tile_spec(tm), _w_chunk_spec(), _w_chunk_spec(), _resident((1, d)),
                  _resident((ts, d)), _resident((ts, d))],
        out_specs=[_tile_spec(tm), _tile_spec(tm), _whole_out((ts, d)), _whole_out((ts, d))],
        out_shape=[jax.ShapeDtypeStruct((t, d), F32), jax.ShapeDtypeStruct((t, d), BF16),
                   jax.ShapeDtypeStruct((ts, d), F32), jax.ShapeDtypeStruct((ts, d), F32)],
        scratch_shapes=[pltpu.VMEM((d, d), BF16), pltpu.VMEM((d, d), BF16)],
        compiler_params=_params(("arbitrary",)),
        name="out_q",
    )(mix, x, w_out, w_cq, g2, mix_s, xs)


def _softmax_rows(s):
    m = jnp.max(s, axis=-1, keepdims=True)
    e = jnp.exp(s - m)
    return e / jnp.sum(e, axis=-1, keepdims=True)


def _attn_co_kernel(q_ref, mk_ref, mv_ref, h1_ref, wc_ref, g3_ref, atts_ref, h1s_ref,
                    h2_ref, xn_ref, h2s_ref, xns_ref, wc_bf, att_ref):
    s = pl.program_id(0)

    def co_block(att_r, h1_r, h2_r, xn_r):
        h2 = h1_r[...] + jnp.dot(att_r[...].astype(BF16), wc_bf[...], preferred_element_type=F32)
        h2_r[...] = h2
        xn_r[...] = _rms(h2, g3_ref).astype(BF16)

    @pl.when(s < N_LOAD)
    def _():
        _load_weight_chunk(s, ((wc_ref, wc_bf),))

    @pl.when(s >= N_LOAD)
    def _():
        for h in range(H_X):
            cols = slice(h * DH_X, (h + 1) * DH_X)
            sc = lax.dot_general(q_ref[:, cols], mk_ref[:, cols], (((1,), (1,)), ((), ())),
                                 preferred_element_type=F32) * (DH_X ** -0.5)
            p = _softmax_rows(sc)
            att_ref[:, cols] = jnp.dot(p.astype(BF16), mv_ref[:, cols],
                                       preferred_element_type=F32).astype(BF16)
        co_block(att_ref, h1_ref, h2_ref, xn_ref)

    @pl.when(s == N_LOAD)
    def _():
        co_block(atts_ref, h1s_ref, h2s_ref, xns_ref)


def _attn_co(q, mk, mv, h1, att_s, h1_s, w_co, g3, *, tm, tiles_per_batch):
    t, d = h1.shape
    ts = h1_s.shape[0]
    mem_spec = pl.BlockSpec((N_MEM, d), lambda s: (jnp.maximum(s - N_LOAD, 0) // tiles_per_batch, 0))
    return pl.pallas_call(
        _attn_co_kernel,
        grid=(N_LOAD + t // tm,),
        in_specs=[_tile_spec(tm), mem_spec, mem_spec, _tile_spec(tm), _w_chunk_spec(), _resident((1, d)),
                  _resident((ts, d)), _resident((ts, d))],
        out_specs=[_tile_spec(tm), _tile_spec(tm), _whole_out((ts, d)), _whole_out((ts, d))],
        out_shape=[jax.ShapeDtypeStruct((t, d), F32), jax.ShapeDtypeStruct((t, d), BF16),
                   jax.ShapeDtypeStruct((ts, d), F32), jax.ShapeDtypeStruct((ts, d), BF16)],
        scratch_shapes=[pltpu.VMEM((d, d), BF16), pltpu.VMEM((tm, d), BF16)],
        compiler_params=_params(("arbitrary",)),
        name="attn_co",
    )(q, mk, mv, h1, w_co, g3, att_s, h1_s)


def _mix_prompt_kernel(u_ref, v_ref, q_ref, k_ref, vb_ref, g_ref, ws_ref, bt_ref, gn_ref,
                       o_ref, so_ref, st_ref, wt_ref, bias_ref, dm_ref, qd_ref, kd_ref, *, batch, n_chunks):
    n = pl.program_id(0)

    @pl.when(n == 0)
    def _():
        ii = lax.broadcasted_iota(jnp.int32, (CHUNK, CHUNK), 0)
        jj = lax.broadcasted_iota(jnp.int32, (CHUNK, CHUNK), 1)
        causal = ii >= jj
        diff = jnp.maximum((ii - jj).astype(F32), 0.0)
        ridx = ii.astype(F32)
        for h in range(H_A):
            lg = LOG_G[h]
            wt_ref[h] = (ws_ref[h] * causal.astype(F32)).astype(BF16)
            bias_ref[h] = jnp.broadcast_to(bt_ref[:, h:h + 1], (CHUNK, CH_A))
            dm_ref[h] = jnp.where(causal, jnp.exp(lg * diff), 0.0)
            qd_ref[h] = jnp.exp(lg * (ridx + 1.0))
            kd_ref[h] = jnp.exp(lg * (CHUNK - 1.0 - ridx))
        st_ref[...] = jnp.zeros_like(st_ref)

    for b in range(batch):
        for h in range(H_A):
            cols = slice(h * CH_A, (h + 1) * CH_A)
            mixed = jnp.dot(wt_ref[h], v_ref[b, :, cols], preferred_element_type=F32) + bias_ref[h]
            o_ref[b, :, cols] = (u_ref[b, :, cols].astype(F32) * mixed).astype(BF16)

            kh = k_ref[b, :, cols]
            vh = vb_ref[b, :, cols]
            state = st_ref[b, h]
            state_bf = state.astype(BF16)
            qh = q_ref[b, :, cols]
            scores = lax.dot_general(qh, kh, (((1,), (1,)), ((), ())), preferred_element_type=F32) * dm_ref[h]
            intra = jnp.dot(scores.astype(BF16), vh, preferred_element_type=F32)
            cross = jnp.dot(qh, state_bf, preferred_element_type=F32) * qd_ref[h]
            ret = intra + cross
            mu = jnp.mean(ret, axis=-1, keepdims=True)
            rc = ret - mu
            var = jnp.mean(rc * rc, axis=-1, keepdims=True)
            normed = rc * lax.rsqrt(var + EPS) * gn_ref[:, cols]
            o_ref[b, :, W_A + h * DV_B:W_A + (h + 1) * DV_B] = (
                g_ref[b, :, cols].astype(F32) * normed).astype(BF16)
            kd = (kh.astype(F32) * kd_ref[h]).astype(BF16)
            st_ref[b, h] = state * math.exp(LOG_G[h] * CHUNK) + lax.dot_general(
                kd, vh, (((0,), (0,)), ((), ())), preferred_element_type=F32)

    @pl.when(n == n_chunks - 1)
    def _():
        so_ref[...] = st_ref[...]


def _mix_prompt(u, rest, ws, bt, gn, *, n_chunks):
    batch, seq, _ = u.shape

    def sec(s):
        return pl.BlockSpec((batch, CHUNK, SEC), lambda n, s=s: (0, n, s))

    table = pltpu.VMEM((H_B, CHUNK, CHUNK), F32)
    return pl.pallas_call(
        functools.partial(_mix_prompt_kernel, batch=batch, n_chunks=n_chunks),
        grid=(n_chunks,),
        in_specs=[sec(0), sec(0), sec(1), sec(2), sec(3), sec(4),
                  _resident((H_A, CHUNK, CHUNK)), _resident((CHUNK, H_A)), _resident((1, W_B))],
        out_specs=[pl.BlockSpec((batch, CHUNK, W_A + W_B), lambda n: (0, n, 0)),
                   _whole_out((batch, H_B, DK_B, DV_B))],
        out_shape=[jax.ShapeDtypeStruct((batch, seq, W_A + W_B), BF16),
                   jax.ShapeDtypeStruct((batch, H_B, DK_B, DV_B), F32)],
        scratch_shapes=[pltpu.VMEM((batch, H_B, DK_B, DV_B), F32), pltpu.VMEM((H_A, CHUNK, CHUNK), BF16),
                        table, table, table, table],
        compiler_params=_params(("arbitrary",)),
        name="mix_prompt",
    )(u, rest, rest, rest, rest, rest, ws, bt, gn)


SAMPLE_ROWS = 16


def _mix_sample_kernel(u_ref, v_ref, q_ref, k_ref, vb_ref, g_ref, w0_ref, b0_ref, gn_ref, s_ref,
                       o_ref, so_ref):
    rid = lax.broadcasted_iota(jnp.int32, (SAMPLE_ROWS, 1), 0)
    o_ref[:, :W_A] = (u_ref[...] * (w0_ref[...] * v_ref[...] + b0_ref[...])).astype(BF16)

    for h in range(H_B):
        cols = slice(h * DK_B, (h + 1) * DK_B)
        g_h = math.exp(LOG_G[h])
        q_blk = q_ref[:, cols]
        k_blk = k_ref[:, cols]
        v_blk = vb_ref[:, cols]
        v_bf = v_blk.astype(BF16)
        intra = jnp.sum(q_blk * k_blk, axis=-1, keepdims=True) * v_blk
        cross = jnp.zeros((SAMPLE_ROWS, DV_B), F32)
        for r in range(SAMPLE_ROWS):
            state = s_ref[r, h]
            q_only_r = jnp.where(rid == r, q_blk, 0.0).astype(BF16)
            cross = cross + jnp.dot(q_only_r, state.astype(BF16), preferred_element_type=F32)
            k_only_r = jnp.where(rid == r, k_blk, 0.0).astype(BF16)
            outer = lax.dot_general(k_only_r, v_bf, (((0,), (0,)), ((), ())), preferred_element_type=F32)
            so_ref[r, h] = state * g_h + outer
        ret = intra + cross * g_h
        mu = jnp.mean(ret, axis=-1, keepdims=True)
        rc = ret - mu
        var = jnp.mean(rc * rc, axis=-1, keepdims=True)
        normed = rc * lax.rsqrt(var + EPS) * gn_ref[:, cols]
        o_ref[:, W_A + h * DV_B:W_A + (h + 1) * DV_B] = (g_ref[:, cols] * normed).astype(BF16)


def _mix_sample(u, rest, w0, b0, gn, state):
    t = u.shape[0]
    rb = SAMPLE_ROWS

    def sec(s):
        return pl.BlockSpec((None, rb, SEC), lambda i, s=s: (s, i, 0))

    return pl.pallas_call(
        _mix_sample_kernel,
        grid=(t // rb,),
        in_specs=[pl.BlockSpec((rb, SEC), lambda i: (i, 0)), sec(0), sec(1), sec(2), sec(3), sec(4),
                  _resident((1, W_A)), _resident((1, W_A)), _resident((1, W_B)),
                  pl.BlockSpec((rb, H_B, DK_B, DV_B), lambda i: (i, 0, 0, 0))],
        out_specs=[pl.BlockSpec((rb, W_A + W_B), lambda i: (i, 0)),
                   pl.BlockSpec((rb, H_B, DK_B, DV_B), lambda i: (i, 0, 0, 0))],
        out_shape=[jax.ShapeDtypeStruct((t, W_A + W_B), BF16),
                   jax.ShapeDtypeStruct(state.shape, F32)],
        compiler_params=_params(("arbitrary",)),
        name="mix_sample",
    )(u, rest, rest, rest, rest, rest, w0, b0, gn, state)


ATTN_SAMPLE_ROWS = 4
SUBLANES = 8


def _attn_sample_kernel(q_ref, ck_ref, cv_ref, o_ref):
    i = pl.program_id(0)
    pairs = N_MEM * H_X
    fold = lambda t: pltpu.roll(t, H_X, 1)
    for r in range(ATTN_SAMPLE_ROWS):
        row = pl.ds(i * ATTN_SAMPLE_ROWS + r, 1)
        q = q_ref[row, :]
        q8 = jnp.concatenate([q[:, (s % H_X) * DH_X:(s % H_X + 1) * DH_X] for s in range(SUBLANES)], axis=0)
        k3 = ck_ref[0, r].reshape(pairs // SUBLANES, SUBLANES, DH_X)
        v3 = cv_ref[0, r].reshape(pairs // SUBLANES, SUBLANES, DH_X)
        s = jnp.sum(k3 * q8[None], axis=-1, keepdims=True) * (DH_X ** -0.5)
        m = jnp.max(s, axis=0, keepdims=True)
        m = jnp.maximum(m, fold(m))
        e = jnp.exp(s - m)
        l = jnp.sum(e, axis=0, keepdims=True)
        acc = jnp.sum(e * v3, axis=0, keepdims=True)
        out = ((acc + fold(acc)) / (l + fold(l)))[0]
        for h in range(H_X):
            o_ref[row, h * DH_X:(h + 1) * DH_X] = out[h:h + 1, :]


def _attn_sample(q, ck, cv):
    t, d = q.shape
    rb = ATTN_SAMPLE_ROWS
    cache_spec = pl.BlockSpec((1, rb, N_MEM, H_X, DH_X), lambda i: (0, i, 0, 0, 0))
    return pl.pallas_call(
        _attn_sample_kernel,
        grid=(t // rb,),
        in_specs=[_resident((t, d)), cache_spec, cache_spec],
        out_specs=_whole_out((t, d)),
        out_shape=jax.ShapeDtypeStruct((t, d), F32),
        compiler_params=_params(("arbitrary",)),
        name="attn_sample",
    )(q, ck, cv)


def _rope_tables(pos):
    half = DK_B // 2
    freqs = ROPE_THETA ** (-jnp.arange(half, dtype=F32) / half)
    ang = pos[:, None] * freqs[None, :]
    cos = jnp.cos(ang)
    sin = jnp.sin(ang)
    return jnp.concatenate([cos, cos], axis=-1), jnp.concatenate([-sin, sin], axis=-1)


def kernel(x_prompt, x_sample, mem_prompt, cache_mem_k, cache_mem_v, state_ret, norm1_g, w_in, sgu_norm_g, sgu_w_s, sgu_b, ret_gn_g, w_out, norm2_g, mem_norm_g, w_cq, w_ck, w_cv, w_co, norm3_g, w_ff1, w_ff2, final_norm_g):
    batch, seq, d = x_prompt.shape
    ts = x_sample.shape[0]
    n_chunks = seq // CHUNK
    tp = batch * seq

    g1 = norm1_g[0][None, :]
    g2 = norm2_g[0][None, :]
    g3 = norm3_g[0][None, :]
    gm = mem_norm_g[0][None, :]
    gf = final_norm_g[None, :]
    lng = sgu_norm_g[0][None, :]
    gn = ret_gn_g[0][None, :]
    ws = sgu_w_s[0]
    sb = sgu_b[0]
    xp = x_prompt.reshape(tp, d)
    xs = x_sample.reshape(ts, d)

    rope_p = _rope_tables(jnp.arange(seq, dtype=F32))
    rope_s = _rope_tables(jnp.full((ts,), PAST_LEN, dtype=F32))
    u_p, xn_p, u_s, xn_s = _in_proj_u(xp, xs, g1, w_in[0], tm=1024)
    rest_p, rest_s = _in_proj_rest(xn_p, xn_s, w_in[0], rope_p, rope_s, lng, tm=1024)
    mix_p, state_p = _mix_prompt(u_p.reshape(batch, seq, SEC), rest_p.reshape(batch, seq, (N_SEC - 1) * SEC),
                                 ws, sb.T, gn, n_chunks=n_chunks)
    mix_p = mix_p.reshape(tp, W_A + W_B)
    w0 = jnp.repeat(ws[:, 0, 0], CH_A)[None, :]
    b0 = jnp.repeat(sb[:, 0], CH_A)[None, :]
    mix_s, state_s = _mix_sample(u_s, rest_s, w0, b0, gn, state_ret[0])

    mem = mem_prompt.reshape(batch * N_MEM, d)
    mk5, mk = _mem_proj(mem, gm, w_ck[0], batch=batch, name="mem_k")
    mv5, mv = _mem_proj(mem, gm, w_cv[0], batch=batch, name="mem_v")
    tr = 512
    h1_p, q_p, h1_s, q_s = _out_q(mix_p, xp, mix_s, xs, w_out[0], w_cq[0], g2, tm=tr)
    att_s = _attn_sample(q_s, cache_mem_k, cache_mem_v)
    h2_p, xn3_p, h2_s, xn3_s = _attn_co(q_p, mk, mv, h1_p, att_s, h1_s, w_co[0], g3,
                                        tm=tr, tiles_per_batch=seq // tr)

    hid_p, hid_s = _ff1(xn3_p, xn3_s, w_ff1[0], tm=2048, tn=1024)
    part_p, part_s = _ff2(hid_p, hid_s, w_ff2[0], tm=2048, tn=1024, tk=1024)
    y_p = _add_norm(part_p, h2_p, gf, tm=1024)
    y_s = _add_norm(part_s.transpose(1, 0, 2).reshape(ts, d), h2_s, gf, tm=ts)

    return (y_p.reshape(batch, seq, d),
            y_s.reshape(ts, 1, d),
            mk5,
            mv5,
            state_p[None],
            state_s[None],
            rest_s[0].reshape(1, ts, 1, H_A, CH_A))
```

```python
import functools
import math

import jax
import jax.numpy as jnp
from jax import lax
from jax.experimental import pallas as pl
from jax.experimental.pallas import tpu as pltpu

F32 = jnp.float32
BF16 = jnp.bfloat16

D_MODEL = 2048
H_A = 8
CH_A = 128
W_A = H_A * CH_A
CHUNK = 128
H_B = 8
DK_B = 128
DV_B = 128
W_B = H_B * DV_B
ROPE_THETA = 10000.0
N_MEM = 256
H_X = 4
DH_X = D_MODEL // H_X
D_FF = 4 * D_MODEL
EPS = 1e-6
PAST_LEN = 16384

SEC = 1024
N_SEC = 6
LOG_G = tuple(math.log(1.0 - 2.0 ** (-5.0 - h)) for h in range(H_B))

V7X_VMEM_BYTES = 64 * 1024 * 1024
VMEM_LIMIT = V7X_VMEM_BYTES - 4 * 1024 * 1024
ROW_CHUNK = 256


def _params(sem):
    return pltpu.CompilerParams(dimension_semantics=sem, vmem_limit_bytes=VMEM_LIMIT)


def _resident(shape):
    zeros = (0,) * len(shape)
    return pl.BlockSpec(shape, lambda *_: zeros, pipeline_mode=pl.Buffered(1))


def _whole_out(shape):
    zeros = (0,) * len(shape)
    return pl.BlockSpec(shape, lambda *_: zeros)


def _row_loop(n_rows, fn):
    rc = min(n_rows, ROW_CHUNK)
    assert n_rows % rc == 0

    def body(c, carry):
        fn(pl.ds(pl.multiple_of(c * rc, rc), rc))
        return carry

    lax.fori_loop(0, n_rows // rc, body, 0)


def _rms(h, g_ref):
    ms = jnp.mean(h * h, axis=-1, keepdims=True)
    return h * lax.rsqrt(ms + EPS) * g_ref[...]


def _rms_rows(x_ref, g_ref, xn_ref, n_rows):
    def fn(rows):
        xn_ref[rows, :] = _rms(x_ref[rows, :], g_ref).astype(BF16)

    _row_loop(n_rows, fn)


def _cast_rows(w_ref, wb_ref):
    def fn(rows):
        wb_ref[rows, :] = w_ref[rows, :].astype(BF16)

    _row_loop(w_ref.shape[0], fn)


def _inproj_u_kernel(x_ref, g_ref, w_ref, xs_ref, u_ref, xn_ref, us_ref, xsn_ref, wb_ref, *, tm, ts):
    @pl.when(pl.program_id(0) == 0)
    def _():
        _cast_rows(w_ref, wb_ref)
        _rms_rows(xs_ref, g_ref, xsn_ref, ts)
        us_ref[...] = jax.nn.gelu(jnp.dot(xsn_ref[...], wb_ref[...], preferred_element_type=F32))

    _rms_rows(x_ref, g_ref, xn_ref, tm)
    u_ref[...] = jax.nn.gelu(jnp.dot(xn_ref[...], wb_ref[...], preferred_element_type=F32)).astype(BF16)


def _in_proj_u(x, xs, g, w, *, tm):
    t, d = x.shape
    ts = xs.shape[0]
    return pl.pallas_call(
        functools.partial(_inproj_u_kernel, tm=tm, ts=ts),
        grid=(t // tm,),
        in_specs=[pl.BlockSpec((tm, d), lambda i: (i, 0)), _resident((1, d)),
                  pl.BlockSpec((d, SEC), lambda i: (0, 0), pipeline_mode=pl.Buffered(1)),
                  _resident((ts, d))],
        out_specs=[pl.BlockSpec((tm, SEC), lambda i: (i, 0)), pl.BlockSpec((tm, d), lambda i: (i, 0)),
                   _whole_out((ts, SEC)), _whole_out((ts, d))],
        out_shape=[jax.ShapeDtypeStruct((t, SEC), BF16), jax.ShapeDtypeStruct((t, d), BF16),
                   jax.ShapeDtypeStruct((ts, SEC), F32), jax.ShapeDtypeStruct((ts, d), BF16)],
        scratch_shapes=[pltpu.VMEM((d, SEC), BF16)],
        compiler_params=_params(("arbitrary",)),
        name="in_proj_u",
    )(x, g, w, xs)


def _inproj_rest_kernel(xn_ref, w_ref, cc_ref, ss_ref, lng_ref, xsn_ref, ccs_ref, sss_ref,
                        o_ref, os_ref, vrows_ref, wb_ref):
    j = pl.program_id(0)
    i = pl.program_id(1)

    @pl.when(i == 0)
    def _():
        _cast_rows(w_ref, wb_ref)

    def layernorm(acc, cc, ss):
        z = jax.nn.gelu(acc)
        mu = jnp.mean(z, axis=-1, keepdims=True)
        zc = z - mu
        var = jnp.mean(zc * zc, axis=-1, keepdims=True)
        return zc * lax.rsqrt(var + EPS) * lng_ref[...]

    def rotary(acc, cc, ss):
        scale = jnp.where(j == 2, DK_B ** -0.5, 1.0).astype(F32)
        heads = []
        for h in range(H_B):
            blk = acc[:, h * DK_B:(h + 1) * DK_B]
            rot = pltpu.roll(blk, DK_B // 2, 1)
            heads.append((blk * cc + rot * ss) * scale)
        return jnp.concatenate(heads, axis=-1)

    def section(pred, epilogue):
        @pl.when(pred)
        def _():
            acc = jnp.dot(xn_ref[...], wb_ref[...], preferred_element_type=F32)
            o_ref[...] = epilogue(acc, cc_ref[...], ss_ref[...]).astype(o_ref.dtype)

            @pl.when(i == 0)
            def _():
                acc_s = jnp.dot(xsn_ref[...], wb_ref[...], preferred_element_type=F32)
                os_ref[j] = epilogue(acc_s, ccs_ref[...], sss_ref[...])

    section(j == 0, layernorm)
    section((j == 1) | (j == 2), rotary)
    section(j == 3, lambda acc, cc, ss: acc)
    section(j == 4, lambda acc, cc, ss: jax.nn.silu(acc))

    @pl.when((j == 0) & (i == 0))
    def _():
        for h in range(H_A):
            vrows_ref[:, 0, h, :] = os_ref[0, :, h * CH_A:(h + 1) * CH_A]


def _in_proj_rest(xn, xsn, w, rope_p, rope_s, lng, *, tm):
    t, d = xn.shape
    ts = xsn.shape[0]
    n_rest = N_SEC - 1
    rope_blocks = rope_p[0].shape[0] // tm
    rope_spec = pl.BlockSpec((tm, DK_B), lambda j, i: (i % rope_blocks, 0))
    return pl.pallas_call(
        _inproj_rest_kernel,
        grid=(n_rest, t // tm),
        in_specs=[pl.BlockSpec((tm, d), lambda j, i: (i, 0)),
                  pl.BlockSpec((d, SEC), lambda j, i: (0, j + 1)),
                  rope_spec, rope_spec, _resident((1, SEC)),
                  _resident((ts, d)), _resident((ts, DK_B)), _resident((ts, DK_B))],
        out_specs=[pl.BlockSpec((tm, SEC), lambda j, i: (i, j)), _whole_out((n_rest, ts, SEC)),
                   _whole_out((ts, 1, H_A, CH_A))],
        out_shape=[jax.ShapeDtypeStruct((t, n_rest * SEC), BF16),
                   jax.ShapeDtypeStruct((n_rest, ts, SEC), F32),
                   jax.ShapeDtypeStruct((ts, 1, H_A, CH_A), F32)],
        scratch_shapes=[pltpu.VMEM((d, SEC), BF16)],
        compiler_params=_params(("arbitrary", "arbitrary")),
        name="in_proj_rest",
    )(xn, w, rope_p[0], rope_p[1], lng, xsn, rope_s[0], rope_s[1])


def _mem_kernel(x_ref, g_ref, w_ref, o5_ref, o2_ref, wb_ref):
    @pl.when(pl.program_id(0) == 0)
    def _():
        _cast_rows(w_ref, wb_ref)

    res = jnp.dot(_rms(x_ref[...], g_ref).astype(BF16), wb_ref[...], preferred_element_type=F32)
    o2_ref[...] = res.astype(BF16)
    for h in range(H_X):
        o5_ref[0, 0, :, h, :] = res[:, h * DH_X:(h + 1) * DH_X]


def _mem_proj(mem, g, w, *, batch, name):
    d = mem.shape[1]
    return pl.pallas_call(
        _mem_kernel,
        grid=(batch,),
        in_specs=[pl.BlockSpec((N_MEM, d), lambda b: (b, 0)), _resident((1, d)), _resident((d, d))],
        out_specs=[pl.BlockSpec((1, 1, N_MEM, H_X, DH_X), lambda b: (0, b, 0, 0, 0)),
                   pl.BlockSpec((N_MEM, d), lambda b: (b, 0))],
        out_shape=[jax.ShapeDtypeStruct((1, batch, N_MEM, H_X, DH_X), F32),
                   jax.ShapeDtypeStruct((batch * N_MEM, d), BF16)],
        scratch_shapes=[pltpu.VMEM((d, d), BF16)],
        compiler_params=_params(("arbitrary",)),
        name=name,
    )(mem, g, w)


def _relu2(acc):
    return jnp.square(jnp.maximum(acc, 0.0))


def _ff1_kernel(x_ref, w_ref, xs_ref, o_ref, os_ref):
    o_ref[...] = _relu2(jnp.dot(x_ref[...], w_ref[...].astype(BF16), preferred_element_type=F32)).astype(BF16)

    @pl.when(pl.program_id(0) == 0)
    def _():
        os_ref[pl.program_id(1)] = _relu2(jnp.dot(xs_ref[...], w_ref[...].astype(BF16),
                                                  preferred_element_type=F32)).astype(BF16)


def _ff1(x, xs, w, *, tm, tn):
    t, k = x.shape
    ts = xs.shape[0]
    n = w.shape[1]
    return pl.pallas_call(
        _ff1_kernel,
        grid=(t // tm, n // tn),
        in_specs=[pl.BlockSpec((tm, k), lambda i, j: (i, 0)),
                  pl.BlockSpec((k, tn), lambda i, j: (0, j)),
                  _resident((ts, k))],
        out_specs=[pl.BlockSpec((tm, tn), lambda i, j: (i, j)), _whole_out((n // tn, ts, tn))],
        out_shape=[jax.ShapeDtypeStruct((t, n), BF16), jax.ShapeDtypeStruct((n // tn, ts, tn), BF16)],
        compiler_params=_params(("arbitrary", "arbitrary")),
        name="ff1",
    )(x, w, xs)


def _ff2_kernel(x_ref, w_ref, xs_ref, o_ref, os_ref):
    i, j, k = pl.program_id(0), pl.program_id(1), pl.program_id(2)

    @pl.when(k == 0)
    def _():
        o_ref[...] = jnp.zeros_like(o_ref)

    o_ref[...] += jnp.dot(x_ref[...], w_ref[...].astype(BF16), preferred_element_type=F32)

    @pl.when(i == 0)
    def _():
        part = jnp.dot(xs_ref[k], w_ref[...].astype(BF16), preferred_element_type=F32)

        @pl.when(k == 0)
        def _():
            os_ref[j] = part

        @pl.when(k > 0)
        def _():
            os_ref[j] += part


def _ff2(x, xs, w, *, tm, tn, tk):
    t, k = x.shape
    ts = xs.shape[1]
    n = w.shape[1]
    return pl.pallas_call(
        _ff2_kernel,
        grid=(t // tm, n // tn, k // tk),
        in_specs=[pl.BlockSpec((tm, tk), lambda i, j, kk: (i, kk)),
                  pl.BlockSpec((tk, tn), lambda i, j, kk: (kk, j)),
                  _resident((k // tk, ts, tk))],
        out_specs=[pl.BlockSpec((tm, tn), lambda i, j, kk: (i, j)), _whole_out((n // tn, ts, tn))],
        out_shape=[jax.ShapeDtypeStruct((t, n), F32), jax.ShapeDtypeStruct((n // tn, ts, tn), F32)],
        compiler_params=_params(("arbitrary", "arbitrary", "arbitrary")),
        name="ff2",
    )(x, w, xs)


def _add_norm_kernel(x_ref, r_ref, g_ref, o_ref, *, tm):
    def fn(rows):
        o_ref[rows, :] = _rms(r_ref[rows, :] + x_ref[rows, :], g_ref)

    _row_loop(tm, fn)


def _add_norm(x, r, g, *, tm):
    t, d = x.shape
    row = pl.BlockSpec((tm, d), lambda i: (i, 0))
    return pl.pallas_call(
        functools.partial(_add_norm_kernel, tm=tm),
        grid=(t // tm,),
        in_specs=[row, row, _resident((1, d))],
        out_specs=row,
        out_shape=jax.ShapeDtypeStruct((t, d), F32),
        compiler_params=_params(("arbitrary",)),
        name="final_norm",
    )(x, r, g)


def _add_norm_sample_kernel(x_ref, r_ref, g_ref, o_ref):
    y = r_ref[...] + jnp.concatenate([x_ref[j] for j in range(x_ref.shape[0])], axis=-1)
    o_ref[:, 0, :] = _rms(y, g_ref)


def _add_norm_sample(x, r, g):
    ts, d = r.shape
    return pl.pallas_call(
        _add_norm_sample_kernel,
        grid=(1,),
        in_specs=[_resident(x.shape), _resident((ts, d)), _resident((1, d))],
        out_specs=_whole_out((ts, 1, d)),
        out_shape=jax.ShapeDtypeStruct((ts, 1, d), F32),
        compiler_params=_params(("arbitrary",)),
        name="final_norm_sample",
    )(x, r, g)


W_CHUNK = 256
N_LOAD = D_MODEL // W_CHUNK


def _load_weight_chunk(s, pairs):
    rows = pl.ds(pl.multiple_of(s * W_CHUNK, W_CHUNK), W_CHUNK)
    for src, dst in pairs:
        dst[rows, :] = src[...].astype(BF16)


def _w_chunk_spec():
    return pl.BlockSpec((W_CHUNK, D_MODEL), lambda s: (jnp.minimum(s, N_LOAD - 1), 0))


def _tile_spec(tm, width=D_MODEL):
    return pl.BlockSpec((tm, width), lambda s: (jnp.maximum(s - N_LOAD, 0), 0))


def _outq_kernel(mix_ref, x_ref, wo_ref, wq_ref, g2_ref, mixs_ref, xs_ref,
                 h1_ref, q_ref, h1s_ref, qs_ref, wo_bf, wq_bf):
    s = pl.program_id(0)

    def block(mix_r, x_r, h1_r, q_r):
        h1 = x_r[...] + jnp.dot(mix_r[...], wo_bf[...], preferred_element_type=F32)
        h1_r[...] = h1
        q_r[...] = jnp.dot(_rms(h1, g2_ref).astype(BF16), wq_bf[...],
                           preferred_element_type=F32).astype(q_r.dtype)

    @pl.when(s < N_LOAD)
    def _():
        _load_weight_chunk(s, ((wo_ref, wo_bf), (wq_ref, wq_bf)))

    @pl.when(s >= N_LOAD)
    def _():
        block(mix_ref, x_ref, h1_ref, q_ref)

    @pl.when(s == N_LOAD)
    def _():
        block(mixs_ref, xs_ref, h1s_ref, qs_ref)


def _out_q(mix, x, mix_s, xs, w_out, w_cq, g2, *, tm):
    t, d = x.shape
    ts = xs.shape[0]
    return pl.pallas_call(
        _outq_kernel,
        grid=(N_LOAD + t // tm,),
        in_specs=[_tile_spec(tm), _tile_spec(tm), _w_chunk_spec(), _w_chunk_spec(), _resident((1, d)),
                  _resident((ts, d)), _resident((ts, d))],
        out_specs=[_tile_spec(tm), _tile_spec(tm), _whole_out((ts, d)), _whole_out((ts, d))],
        out_shape=[jax.ShapeDtypeStruct((t, d), F32), jax.ShapeDtypeStruct((t, d), BF16),
                   jax.ShapeDtypeStruct((ts, d), F32), jax.ShapeDtypeStruct((ts, d), F32)],
        scratch_shapes=[pltpu.VMEM((d, d), BF16), pltpu.VMEM((d, d), BF16)],
        compiler_params=_params(("arbitrary",)),
        name="out_q",
    )(mix, x, w_out, w_cq, g2, mix_s, xs)


def _softmax_rows(s):
    m = jnp.max(s, axis=-1, keepdims=True)
    e = jnp.exp(s - m)
    return e / jnp.sum(e, axis=-1, keepdims=True)


def _attn_co_kernel(q_ref, mk_ref, mv_ref, h1_ref, wc_ref, g3_ref, atts_ref, h1s_ref,
                    h2_ref, xn_ref, h2s_ref, xns_ref, wc_bf, att_ref):
    s = pl.program_id(0)

    def co_block(att_r, h1_r, h2_r, xn_r):
        h2 = h1_r[...] + jnp.dot(att_r[...].astype(BF16), wc_bf[...], preferred_element_type=F32)
        h2_r[...] = h2
        xn_r[...] = _rms(h2, g3_ref).astype(BF16)

    @pl.when(s < N_LOAD)
    def _():
        _load_weight_chunk(s, ((wc_ref, wc_bf),))

    @pl.when(s >= N_LOAD)
    def _():
        for h in range(H_X):
            cols = slice(h * DH_X, (h + 1) * DH_X)
            sc = lax.dot_general(q_ref[:, cols], mk_ref[:, cols], (((1,), (1,)), ((), ())),
                                 preferred_element_type=F32) * (DH_X ** -0.5)
            p = _softmax_rows(sc)
            att_ref[:, cols] = jnp.dot(p.astype(BF16), mv_ref[:, cols],
                                       preferred_element_type=F32).astype(BF16)
        co_block(att_ref, h1_ref, h2_ref, xn_ref)

    @pl.when(s == N_LOAD)
    def _():
        co_block(atts_ref, h1s_ref, h2s_ref, xns_ref)


def _attn_co(q, mk, mv, h1, att_s, h1_s, w_co, g3, *, tm, tiles_per_batch):
    t, d = h1.shape
    ts = h1_s.shape[0]
    mem_spec = pl.BlockSpec((N_MEM, d), lambda s: (jnp.maximum(s - N_LOAD, 0) // tiles_per_batch, 0))
    return pl.pallas_call(
        _attn_co_kernel,
        grid=(N_LOAD + t // tm,),
        in_specs=[_tile_spec(tm), mem_spec, mem_spec, _tile_spec(tm), _w_chunk_spec(), _resident((1, d)),
                  _resident((ts, d)), _resident((ts, d))],
        out_specs=[_tile_spec(tm), _tile_spec(tm), _whole_out((ts, d)), _whole_out((ts, d))],
        out_shape=[jax.ShapeDtypeStruct((t, d), F32), jax.ShapeDtypeStruct((t, d), BF16),
                   jax.ShapeDtypeStruct((ts, d), F32), jax.ShapeDtypeStruct((ts, d), BF16)],
        scratch_shapes=[pltpu.VMEM((d, d), BF16), pltpu.VMEM((tm, d), BF16)],
        compiler_params=_params(("arbitrary",)),
        name="attn_co",
    )(q, mk, mv, h1, w_co, g3, att_s, h1_s)


def _mix_prompt_kernel(u_ref, v_ref, q_ref, k_ref, vb_ref, g_ref, ws_ref, bt_ref, gn_ref,
                       o_ref, so_ref, st_ref, wt_ref, bias_ref, dm_ref, qd_ref, kd_ref, *, batch, n_chunks):
    n = pl.program_id(0)

    @pl.when(n == 0)
    def _():
        ii = lax.broadcasted_iota(jnp.int32, (CHUNK, CHUNK), 0)
        jj = lax.broadcasted_iota(jnp.int32, (CHUNK, CHUNK), 1)
        causal = ii >= jj
        diff = jnp.maximum((ii - jj).astype(F32), 0.0)
        ridx = ii.astype(F32)
        for h in range(H_A):
            lg = LOG_G[h]
            wt_ref[h] = (ws_ref[h] * causal.astype(F32)).astype(BF16)
            bias_ref[h] = jnp.broadcast_to(bt_ref[:, h:h + 1], (CHUNK, CH_A))
            dm_ref[h] = jnp.where(causal, jnp.exp(lg * diff), 0.0)
            qd_ref[h] = jnp.exp(lg * (ridx + 1.0))
            kd_ref[h] = jnp.exp(lg * (CHUNK - 1.0 - ridx))
        st_ref[...] = jnp.zeros_like(st_ref)

    for b in range(batch):
        for h in range(H_A):
            cols = slice(h * CH_A, (h + 1) * CH_A)
            mixed = jnp.dot(wt_ref[h], v_ref[b, :, cols], preferred_element_type=F32) + bias_ref[h]
            o_ref[b, :, cols] = (u_ref[b, :, cols].astype(F32) * mixed).astype(BF16)

            kh = k_ref[b, :, cols]
            vh = vb_ref[b, :, cols]
            state = st_ref[b, h]
            state_bf = state.astype(BF16)
            qh = q_ref[b, :, cols]
            scores = lax.dot_general(qh, kh, (((1,), (1,)), ((), ())), preferred_element_type=F32) * dm_ref[h]
            intra = jnp.dot(scores.astype(BF16), vh, preferred_element_type=F32)
            cross = jnp.dot(qh, state_bf, preferred_element_type=F32) * qd_ref[h]
            ret = intra + cross
            mu = jnp.mean(ret, axis=-1, keepdims=True)
            rc = ret - mu
            var = jnp.mean(rc * rc, axis=-1, keepdims=True)
            normed = rc * lax.rsqrt(var + EPS) * gn_ref[:, cols]
            o_ref[b, :, W_A + h * DV_B:W_A + (h + 1) * DV_B] = (
                g_ref[b, :, cols].astype(F32) * normed).astype(BF16)
            kd = (kh.astype(F32) * kd_ref[h]).astype(BF16)
            st_ref[b, h] = state * math.exp(LOG_G[h] * CHUNK) + lax.dot_general(
                kd, vh, (((0,), (0,)), ((), ())), preferred_element_type=F32)

    @pl.when(n == n_chunks - 1)
    def _():
        so_ref[...] = st_ref[...]


def _mix_prompt(u, rest, ws, bt, gn, *, n_chunks):
    batch, seq, _ = u.shape

    def sec(s):
        return pl.BlockSpec((batch, CHUNK, SEC), lambda n, s=s: (0, n, s))

    table = pltpu.VMEM((H_B, CHUNK, CHUNK), F32)
    return pl.pallas_call(
        functools.partial(_mix_prompt_kernel, batch=batch, n_chunks=n_chunks),
        grid=(n_chunks,),
        in_specs=[sec(0), sec(0), sec(1), sec(2), sec(3), sec(4),
                  _resident((H_A, CHUNK, CHUNK)), _resident((CHUNK, H_A)), _resident((1, W_B))],
        out_specs=[pl.BlockSpec((batch, CHUNK, W_A + W_B), lambda n: (0, n, 0)),
                   _whole_out((batch, H_B, DK_B, DV_B))],
        out_shape=[jax.ShapeDtypeStruct((batch, seq, W_A + W_B), BF16),
                   jax.ShapeDtypeStruct((batch, H_B, DK_B, DV_B), F32)],
        scratch_shapes=[pltpu.VMEM((batch, H_B, DK_B, DV_B), F32), pltpu.VMEM((H_A, CHUNK, CHUNK), BF16),
                        table, table, table, table],
        compiler_params=_params(("arbitrary",)),
        name="mix_prompt",
    )(u, rest, rest, rest, rest, rest, ws, bt, gn)


SAMPLE_ROWS = 16


def _mix_sample_kernel(u_ref, v_ref, q_ref, k_ref, vb_ref, g_ref, w0_ref, b0_ref, gn_ref, s_ref,
                       o_ref, so_ref):
    rid = lax.broadcasted_iota(jnp.int32, (SAMPLE_ROWS, 1), 0)
    o_ref[:, :W_A] = (u_ref[...] * (w0_ref[...] * v_ref[...] + b0_ref[...])).astype(BF16)

    for h in range(H_B):
        cols = slice(h * DK_B, (h + 1) * DK_B)
        g_h = math.exp(LOG_G[h])
        q_blk = q_ref[:, cols]
        k_blk = k_ref[:, cols]
        v_blk = vb_ref[:, cols]
        v_bf = v_blk.astype(BF16)
        intra = jnp.sum(q_blk * k_blk, axis=-1, keepdims=True) * v_blk
        cross = jnp.zeros((SAMPLE_ROWS, DV_B), F32)
        for r in range(SAMPLE_ROWS):
            state = s_ref[r, h]
            q_only_r = jnp.where(rid == r, q_blk, 0.0).astype(BF16)
            cross = cross + jnp.dot(q_only_r, state.astype(BF16), preferred_element_type=F32)
            k_only_r = jnp.where(rid == r, k_blk, 0.0).astype(BF16)
            outer = lax.dot_general(k_only_r, v_bf, (((0,), (0,)), ((), ())), preferred_element_type=F32)
            so_ref[r, h] = state * g_h + outer
        ret = intra + cross * g_h
        mu = jnp.mean(ret, axis=-1, keepdims=True)
        rc = ret - mu
        var = jnp.mean(rc * rc, axis=-1, keepdims=True)
        normed = rc * lax.rsqrt(var + EPS) * gn_ref[:, cols]
        o_ref[:, W_A + h * DV_B:W_A + (h + 1) * DV_B] = (g_ref[:, cols] * normed).astype(BF16)


def _mix_sample(u, rest, w0, b0, gn, state):
    t = u.shape[0]
    rb = SAMPLE_ROWS

    def sec(s):
        return pl.BlockSpec((None, rb, SEC), lambda i, s=s: (s, i, 0))

    return pl.pallas_call(
        _mix_sample_kernel,
        grid=(t // rb,),
        in_specs=[pl.BlockSpec((rb, SEC), lambda i: (i, 0)), sec(0), sec(1), sec(2), sec(3), sec(4),
                  _resident((1, W_A)), _resident((1, W_A)), _resident((1, W_B)),
                  pl.BlockSpec((rb, H_B, DK_B, DV_B), lambda i: (i, 0, 0, 0))],
        out_specs=[pl.BlockSpec((rb, W_A + W_B), lambda i: (i, 0)),
                   pl.BlockSpec((rb, H_B, DK_B, DV_B), lambda i: (i, 0, 0, 0))],
        out_shape=[jax.ShapeDtypeStruct((t, W_A + W_B), BF16),
                   jax.ShapeDtypeStruct(state.shape, F32)],
        compiler_params=_params(("arbitrary",)),
        name="mix_sample",
    )(u, rest, rest, rest, rest, rest, w0, b0, gn, state)


ATTN_SAMPLE_ROWS = 4
SUBLANES = 8


def _attn_sample_kernel(q_ref, ck_ref, cv_ref, o_ref):
    i = pl.program_id(0)
    pairs = N_MEM * H_X
    fold = lambda t: pltpu.roll(t, H_X, 1)
    for r in range(ATTN_SAMPLE_ROWS):
        row = pl.ds(i * ATTN_SAMPLE_ROWS + r, 1)
        q = q_ref[row, :]
        q8 = jnp.concatenate([q[:, (s % H_X) * DH_X:(s % H_X + 1) * DH_X] for s in range(SUBLANES)], axis=0)
        k3 = ck_ref[0, r].reshape(pairs // SUBLANES, SUBLANES, DH_X)
        v3 = cv_ref[0, r].reshape(pairs // SUBLANES, SUBLANES, DH_X)
        s = jnp.sum(k3 * q8[None], axis=-1, keepdims=True) * (DH_X ** -0.5)
        m = jnp.max(s, axis=0, keepdims=True)
        m = jnp.maximum(m, fold(m))
        e = jnp.exp(s - m)
        l = jnp.sum(e, axis=0, keepdims=True)
        acc = jnp.sum(e * v3, axis=0, keepdims=True)
        out = ((acc + fold(acc)) / (l + fold(l)))[0]
        for h in range(H_X):
            o_ref[row, h * DH_X:(h + 1) * DH_X] = out[h:h + 1, :]


def _attn_sample(q, ck, cv):
    t, d = q.shape
    rb = ATTN_SAMPLE_ROWS
    cache_spec = pl.BlockSpec((1, rb, N_MEM, H_X, DH_X), lambda i: (0, i, 0, 0, 0))
    return pl.pallas_call(
        _attn_sample_kernel,
        grid=(t // rb,),
        in_specs=[_resident((t, d)), cache_spec, cache_spec],
        out_specs=_whole_out((t, d)),
        out_shape=jax.ShapeDtypeStruct((t, d), F32),
        compiler_params=_params(("arbitrary",)),
        name="attn_sample",
    )(q, ck, cv)


def _rope_tables(pos):
    half = DK_B // 2
    freqs = ROPE_THETA ** (-jnp.arange(half, dtype=F32) / half)
    ang = pos[:, None] * freqs[None, :]
    cos = jnp.cos(ang)
    sin = jnp.sin(ang)
    return jnp.concatenate([cos, cos], axis=-1), jnp.concatenate([-sin, sin], axis=-1)


def kernel(x_prompt, x_sample, mem_prompt, cache_mem_k, cache_mem_v, state_ret, norm1_g, w_in, sgu_norm_g, sgu_w_s, sgu_b, ret_gn_g, w_out, norm2_g, mem_norm_g, w_cq, w_ck, w_cv, w_co, norm3_g, w_ff1, w_ff2, final_norm_g):
    batch, seq, d = x_prompt.shape
    ts = x_sample.shape[0]
    n_chunks = seq // CHUNK
    tp = batch * seq

    g1 = norm1_g[0][None, :]
    g2 = norm2_g[0][None, :]
    g3 = norm3_g[0][None, :]
    gm = mem_norm_g[0][None, :]
    gf = final_norm_g[None, :]
    lng = sgu_norm_g[0][None, :]
    gn = ret_gn_g[0][None, :]
    ws = sgu_w_s[0]
    sb = sgu_b[0]
    xp = x_prompt.reshape(tp, d)
    xs = x_sample.reshape(ts, d)

    rope_p = _rope_tables(jnp.arange(seq, dtype=F32))
    rope_s = _rope_tables(jnp.full((ts,), PAST_LEN, dtype=F32))
    u_p, xn_p, u_s, xn_s = _in_proj_u(xp, xs, g1, w_in[0], tm=1024)
    rest_p, rest_s, v_rows = _in_proj_rest(xn_p, xn_s, w_in[0], rope_p, rope_s, lng, tm=1024)
    mix_p, state_p = _mix_prompt(u_p.reshape(batch, seq, SEC), rest_p.reshape(batch, seq, (N_SEC - 1) * SEC),
                                 ws, sb.T, gn, n_chunks=n_chunks)
    mix_p = mix_p.reshape(tp, W_A + W_B)
    w0 = jnp.repeat(ws[:, 0, 0], CH_A)[None, :]
    b0 = jnp.repeat(sb[:, 0], CH_A)[None, :]
    mix_s, state_s = _mix_sample(u_s, rest_s, w0, b0, gn, state_ret[0])

    mem = mem_prompt.reshape(batch * N_MEM, d)
    mk5, mk = _mem_proj(mem, gm, w_ck[0], batch=batch, name="mem_k")
    mv5, mv = _mem_proj(mem, gm, w_cv[0], batch=batch, name="mem_v")
    tr = 512
    h1_p, q_p, h1_s, q_s = _out_q(mix_p, xp, mix_s, xs, w_out[0], w_cq[0], g2, tm=tr)
    att_s = _attn_sample(q_s, cache_mem_k, cache_mem_v)
    h2_p, xn3_p, h2_s, xn3_s = _attn_co(q_p, mk, mv, h1_p, att_s, h1_s, w_co[0], g3,
                                        tm=tr, tiles_per_batch=seq // tr)

    hid_p, hid_s = _ff1(xn3_p, xn3_s, w_ff1[0], tm=2048, tn=1024)
    part_p, part_s = _ff2(hid_p, hid_s, w_ff2[0], tm=2048, tn=1024, tk=1024)
    y_p = _add_norm(part_p, h2_p, gf, tm=1024)
    y_s = _add_norm_sample(part_s, h2_s, gf)

    return (y_p.reshape(batch, seq, d),
            y_s,
            mk5,
            mv5,
            state_p[None],
            state_s[None],
            v_rows[None])
```

```python
import functools
import math

import jax
import jax.numpy as jnp
from jax import lax
from jax.experimental import pallas as pl
from jax.experimental.pallas import tpu as pltpu

F32 = jnp.float32
BF16 = jnp.bfloat16

D_MODEL = 2048
H_A = 8
CH_A = 128
W_A = H_A * CH_A
CHUNK = 128
H_B = 8
DK_B = 128
DV_B = 128
W_B = H_B * DV_B
ROPE_THETA = 10000.0
N_MEM = 256
H_X = 4
DH_X = D_MODEL // H_X
EPS = 1e-6
PAST_LEN = 16384

SEC = 1024
N_SEC = 6
LOG_G = tuple(math.log(1.0 - 2.0 ** (-5.0 - h)) for h in range(H_B))

V7X_VMEM_BYTES = 64 * 1024 * 1024
COMPILER_SCRATCH_BYTES = 4 * 1024 * 1024
VMEM_LIMIT = V7X_VMEM_BYTES - COMPILER_SCRATCH_BYTES
ROW_CHUNK = 256

TM_IN_PROJ = 1024
TM_RESIDENT = 512
TM_MLP, TN_MLP, TK_MLP = 2048, 1024, 1024
TM_NORM = 1024


def _params(sem):
    return pltpu.CompilerParams(dimension_semantics=sem, vmem_limit_bytes=VMEM_LIMIT)


def _resident(shape):
    zeros = (0,) * len(shape)
    return pl.BlockSpec(shape, lambda *_: zeros, pipeline_mode=pl.Buffered(1))


def _whole_out(shape):
    zeros = (0,) * len(shape)
    return pl.BlockSpec(shape, lambda *_: zeros)


def _row_loop(n_rows, fn):
    rc = min(n_rows, ROW_CHUNK)
    assert n_rows % rc == 0

    def body(c, carry):
        fn(pl.ds(pl.multiple_of(c * rc, rc), rc))
        return carry

    lax.fori_loop(0, n_rows // rc, body, 0)


def _rms(h, g_ref):
    ms = jnp.mean(h * h, axis=-1, keepdims=True)
    return h * lax.rsqrt(ms + EPS) * g_ref[...]


def _rms_rows(x_ref, g_ref, xn_ref, n_rows):
    def fn(rows):
        xn_ref[rows, :] = _rms(x_ref[rows, :], g_ref).astype(BF16)

    _row_loop(n_rows, fn)


def _cast_rows(w_ref, wb_ref):
    def fn(rows):
        wb_ref[rows, :] = w_ref[rows, :].astype(BF16)

    _row_loop(w_ref.shape[0], fn)


def _inproj_u_kernel(x_ref, g_ref, w_ref, xs_ref, u_ref, xn_ref, us_ref, xsn_ref, wb_ref, *, tm, ts):
    @pl.when(pl.program_id(0) == 0)
    def _():
        _cast_rows(w_ref, wb_ref)
        _rms_rows(xs_ref, g_ref, xsn_ref, ts)
        us_ref[...] = jax.nn.gelu(jnp.dot(xsn_ref[...], wb_ref[...], preferred_element_type=F32))

    _rms_rows(x_ref, g_ref, xn_ref, tm)
    u_ref[...] = jax.nn.gelu(jnp.dot(xn_ref[...], wb_ref[...], preferred_element_type=F32)).astype(BF16)


def _in_proj_u(x, xs, g, w, *, tm):
    t, d = x.shape
    ts = xs.shape[0]
    return pl.pallas_call(
        functools.partial(_inproj_u_kernel, tm=tm, ts=ts),
        grid=(t // tm,),
        in_specs=[pl.BlockSpec((tm, d), lambda i: (i, 0)), _resident((1, d)),
                  pl.BlockSpec((d, SEC), lambda i: (0, 0), pipeline_mode=pl.Buffered(1)),
                  _resident((ts, d))],
        out_specs=[pl.BlockSpec((tm, SEC), lambda i: (i, 0)), pl.BlockSpec((tm, d), lambda i: (i, 0)),
                   _whole_out((ts, SEC)), _whole_out((ts, d))],
        out_shape=[jax.ShapeDtypeStruct((t, SEC), BF16), jax.ShapeDtypeStruct((t, d), BF16),
                   jax.ShapeDtypeStruct((ts, SEC), F32), jax.ShapeDtypeStruct((ts, d), BF16)],
        scratch_shapes=[pltpu.VMEM((d, SEC), BF16)],
        compiler_params=_params(("arbitrary",)),
        name="in_proj_u",
    )(x, g, w, xs)


def _inproj_rest_kernel(xn_ref, w_ref, cc_ref, ss_ref, lng_ref, xsn_ref, ccs_ref, sss_ref,
                        o_ref, os_ref, vrows_ref, wb_ref):
    j = pl.program_id(0)
    i = pl.program_id(1)

    @pl.when(i == 0)
    def _():
        _cast_rows(w_ref, wb_ref)

    def layernorm(acc, cc, ss):
        z = jax.nn.gelu(acc)
        mu = jnp.mean(z, axis=-1, keepdims=True)
        zc = z - mu
        var = jnp.mean(zc * zc, axis=-1, keepdims=True)
        return zc * lax.rsqrt(var + EPS) * lng_ref[...]

    def rotary(acc, cc, ss):
        scale = jnp.where(j == 2, DK_B ** -0.5, 1.0).astype(F32)
        heads = []
        for h in range(H_B):
            blk = acc[:, h * DK_B:(h + 1) * DK_B]
            rot = pltpu.roll(blk, DK_B // 2, 1)
            heads.append((blk * cc + rot * ss) * scale)
        return jnp.concatenate(heads, axis=-1)

    def section(pred, epilogue):
        @pl.when(pred)
        def _():
            acc = jnp.dot(xn_ref[...], wb_ref[...], preferred_element_type=F32)
            o_ref[...] = epilogue(acc, cc_ref[...], ss_ref[...]).astype(o_ref.dtype)

            @pl.when(i == 0)
            def _():
                acc_s = jnp.dot(xsn_ref[...], wb_ref[...], preferred_element_type=F32)
                os_ref[j] = epilogue(acc_s, ccs_ref[...], sss_ref[...])

    section(j == 0, layernorm)
    section((j == 1) | (j == 2), rotary)
    section(j == 3, lambda acc, cc, ss: acc)
    section(j == 4, lambda acc, cc, ss: jax.nn.silu(acc))

    @pl.when((j == 0) & (i == 0))
    def _():
        for h in range(H_A):
            vrows_ref[:, 0, h, :] = os_ref[0, :, h * CH_A:(h + 1) * CH_A]


def _in_proj_rest(xn, xsn, w, rope_p, rope_s, lng, *, tm):
    t, d = xn.shape
    ts = xsn.shape[0]
    n_rest = N_SEC - 1
    rope_blocks = rope_p[0].shape[0] // tm
    rope_spec = pl.BlockSpec((tm, DK_B), lambda j, i: (i % rope_blocks, 0))
    return pl.pallas_call(
        _inproj_rest_kernel,
        grid=(n_rest, t // tm),
        in_specs=[pl.BlockSpec((tm, d), lambda j, i: (i, 0)),
                  pl.BlockSpec((d, SEC), lambda j, i: (0, j + 1)),
                  rope_spec, rope_spec, _resident((1, SEC)),
                  _resident((ts, d)), _resident((ts, DK_B)), _resident((ts, DK_B))],
        out_specs=[pl.BlockSpec((tm, SEC), lambda j, i: (i, j)), _whole_out((n_rest, ts, SEC)),
                   _whole_out((ts, 1, H_A, CH_A))],
        out_shape=[jax.ShapeDtypeStruct((t, n_rest * SEC), BF16),
                   jax.ShapeDtypeStruct((n_rest, ts, SEC), F32),
                   jax.ShapeDtypeStruct((ts, 1, H_A, CH_A), F32)],
        scratch_shapes=[pltpu.VMEM((d, SEC), BF16)],
        compiler_params=_params(("arbitrary", "arbitrary")),
        name="in_proj_rest",
    )(xn, w, rope_p[0], rope_p[1], lng, xsn, rope_s[0], rope_s[1])


def _mem_kernel(x_ref, g_ref, w_ref, o5_ref, o2_ref, wb_ref):
    @pl.when(pl.program_id(0) == 0)
    def _():
        _cast_rows(w_ref, wb_ref)

    res = jnp.dot(_rms(x_ref[...], g_ref).astype(BF16), wb_ref[...], preferred_element_type=F32)
    o2_ref[...] = res.astype(BF16)
    for h in range(H_X):
        o5_ref[0, 0, :, h, :] = res[:, h * DH_X:(h + 1) * DH_X]


def _mem_proj(mem, g, w, *, batch, name):
    d = mem.shape[1]
    return pl.pallas_call(
        _mem_kernel,
        grid=(batch,),
        in_specs=[pl.BlockSpec((N_MEM, d), lambda b: (b, 0)), _resident((1, d)), _resident((d, d))],
        out_specs=[pl.BlockSpec((1, 1, N_MEM, H_X, DH_X), lambda b: (0, b, 0, 0, 0)),
                   pl.BlockSpec((N_MEM, d), lambda b: (b, 0))],
        out_shape=[jax.ShapeDtypeStruct((1, batch, N_MEM, H_X, DH_X), F32),
                   jax.ShapeDtypeStruct((batch * N_MEM, d), BF16)],
        scratch_shapes=[pltpu.VMEM((d, d), BF16)],
        compiler_params=_params(("arbitrary",)),
        name=name,
    )(mem, g, w)


def _relu2(acc):
    return jnp.square(jnp.maximum(acc, 0.0))


def _ff1_kernel(x_ref, w_ref, xs_ref, o_ref, os_ref):
    o_ref[...] = _relu2(jnp.dot(x_ref[...], w_ref[...].astype(BF16), preferred_element_type=F32)).astype(BF16)

    @pl.when(pl.program_id(0) == 0)
    def _():
        os_ref[pl.program_id(1)] = _relu2(jnp.dot(xs_ref[...], w_ref[...].astype(BF16),
                                                  preferred_element_type=F32)).astype(BF16)


def _ff1(x, xs, w, *, tm, tn):
    t, k = x.shape
    ts = xs.shape[0]
    n = w.shape[1]
    return pl.pallas_call(
        _ff1_kernel,
        grid=(t // tm, n // tn),
        in_specs=[pl.BlockSpec((tm, k), lambda i, j: (i, 0)),
                  pl.BlockSpec((k, tn), lambda i, j: (0, j)),
                  _resident((ts, k))],
        out_specs=[pl.BlockSpec((tm, tn), lambda i, j: (i, j)), _whole_out((n // tn, ts, tn))],
        out_shape=[jax.ShapeDtypeStruct((t, n), BF16), jax.ShapeDtypeStruct((n // tn, ts, tn), BF16)],
        compiler_params=_params(("arbitrary", "arbitrary")),
        name="ff1",
    )(x, w, xs)


def _ff2_kernel(x_ref, w_ref, xs_ref, o_ref, os_ref):
    i, j, k = pl.program_id(0), pl.program_id(1), pl.program_id(2)

    @pl.when(k == 0)
    def _():
        o_ref[...] = jnp.zeros_like(o_ref)

    o_ref[...] += jnp.dot(x_ref[...], w_ref[...].astype(BF16), preferred_element_type=F32)

    @pl.when(i == 0)
    def _():
        part = jnp.dot(xs_ref[k], w_ref[...].astype(BF16), preferred_element_type=F32)

        @pl.when(k == 0)
        def _():
            os_ref[j] = part

        @pl.when(k > 0)
        def _():
            os_ref[j] += part


def _ff2(x, xs, w, *, tm, tn, tk):
    t, k = x.shape
    ts = xs.shape[1]
    n = w.shape[1]
    assert xs.shape == (k // tk, ts, tk), "sample tiles must be stacked in contraction slices of tk"
    return pl.pallas_call(
        _ff2_kernel,
        grid=(t // tm, n // tn, k // tk),
        in_specs=[pl.BlockSpec((tm, tk), lambda i, j, kk: (i, kk)),
                  pl.BlockSpec((tk, tn), lambda i, j, kk: (kk, j)),
                  _resident((k // tk, ts, tk))],
        out_specs=[pl.BlockSpec((tm, tn), lambda i, j, kk: (i, j)), _whole_out((n // tn, ts, tn))],
        out_shape=[jax.ShapeDtypeStruct((t, n), F32), jax.ShapeDtypeStruct((n // tn, ts, tn), F32)],
        compiler_params=_params(("arbitrary", "arbitrary", "arbitrary")),
        name="ff2",
    )(x, w, xs)


def _add_norm_kernel(x_ref, r_ref, g_ref, o_ref, *, tm):
    def fn(rows):
        o_ref[rows, :] = _rms(r_ref[rows, :] + x_ref[rows, :], g_ref)

    _row_loop(tm, fn)


def _add_norm(x, r, g, *, tm):
    t, d = x.shape
    row = pl.BlockSpec((tm, d), lambda i: (i, 0))
    return pl.pallas_call(
        functools.partial(_add_norm_kernel, tm=tm),
        grid=(t // tm,),
        in_specs=[row, row, _resident((1, d))],
        out_specs=row,
        out_shape=jax.ShapeDtypeStruct((t, d), F32),
        compiler_params=_params(("arbitrary",)),
        name="final_norm",
    )(x, r, g)


def _add_norm_sample_kernel(x_ref, r_ref, g_ref, o_ref):
    y = r_ref[...] + jnp.concatenate([x_ref[j] for j in range(x_ref.shape[0])], axis=-1)
    o_ref[:, 0, :] = _rms(y, g_ref)


def _add_norm_sample(x, r, g):
    ts, d = r.shape
    return pl.pallas_call(
        _add_norm_sample_kernel,
        grid=(1,),
        in_specs=[_resident(x.shape), _resident((ts, d)), _resident((1, d))],
        out_specs=_whole_out((ts, 1, d)),
        out_shape=jax.ShapeDtypeStruct((ts, 1, d), F32),
        compiler_params=_params(("arbitrary",)),
        name="final_norm_sample",
    )(x, r, g)


W_CHUNK = 256
N_LOAD = D_MODEL // W_CHUNK


def _load_weight_chunk(s, pairs):
    rows = pl.ds(pl.multiple_of(s * W_CHUNK, W_CHUNK), W_CHUNK)
    for src, dst in pairs:
        dst[rows, :] = src[...].astype(BF16)


def _w_chunk_spec():
    return pl.BlockSpec((W_CHUNK, D_MODEL), lambda s: (jnp.minimum(s, N_LOAD - 1), 0))


def _tile_spec(tm, width=D_MODEL):
    return pl.BlockSpec((tm, width), lambda s: (jnp.maximum(s - N_LOAD, 0), 0))


def _outq_kernel(mix_ref, x_ref, wo_ref, wq_ref, g2_ref, mixs_ref, xs_ref,
                 h1_ref, q_ref, h1s_ref, qs_ref, wo_bf, wq_bf):
    s = pl.program_id(0)

    def block(mix_r, x_r, h1_r, q_r):
        h1 = x_r[...] + jnp.dot(mix_r[...], wo_bf[...], preferred_element_type=F32)
        h1_r[...] = h1
        q_r[...] = jnp.dot(_rms(h1, g2_ref).astype(BF16), wq_bf[...],
                           preferred_element_type=F32).astype(q_r.dtype)

    @pl.when(s < N_LOAD)
    def _():
        _load_weight_chunk(s, ((wo_ref, wo_bf), (wq_ref, wq_bf)))

    @pl.when(s >= N_LOAD)
    def _():
        block(mix_ref, x_ref, h1_ref, q_ref)

    @pl.when(s == N_LOAD)
    def _():
        block(mixs_ref, xs_ref, h1s_ref, qs_ref)


def _out_q(mix, x, mix_s, xs, w_out, w_cq, g2, *, tm):
    t, d = x.shape
    ts = xs.shape[0]
    return pl.pallas_call(
        _outq_kernel,
        grid=(N_LOAD + t // tm,),
        in_specs=[_tile_spec(tm), _tile_spec(tm), _w_chunk_spec(), _w_chunk_spec(), _resident((1, d)),
                  _resident((ts, d)), _resident((ts, d))],
        out_specs=[_tile_spec(tm), _tile_spec(tm), _whole_out((ts, d)), _whole_out((ts, d))],
        out_shape=[jax.ShapeDtypeStruct((t, d), F32), jax.ShapeDtypeStruct((t, d), BF16),
                   jax.ShapeDtypeStruct((ts, d), F32), jax.ShapeDtypeStruct((ts, d), F32)],
        scratch_shapes=[pltpu.VMEM((d, d), BF16), pltpu.VMEM((d, d), BF16)],
        compiler_params=_params(("arbitrary",)),
        name="out_q",
    )(mix, x, w_out, w_cq, g2, mix_s, xs)


def _softmax_rows(s):
    m = jnp.max(s, axis=-1, keepdims=True)
    e = jnp.exp(s - m)
    return e / jnp.sum(e, axis=-1, keepdims=True)


def _attn_co_kernel(q_ref, mk_ref, mv_ref, h1_ref, wc_ref, g3_ref, atts_ref, h1s_ref,
                    h2_ref, xn_ref, h2s_ref, xns_ref, wc_bf, att_ref):
    s = pl.program_id(0)

    def co_block(att_r, h1_r, h2_r, xn_r):
        h2 = h1_r[...] + jnp.dot(att_r[...].astype(BF16), wc_bf[...], preferred_element_type=F32)
        h2_r[...] = h2
        xn_r[...] = _rms(h2, g3_ref).astype(BF16)

    @pl.when(s < N_LOAD)
    def _():
        _load_weight_chunk(s, ((wc_ref, wc_bf),))

    @pl.when(s >= N_LOAD)
    def _():
        for h in range(H_X):
            cols = slice(h * DH_X, (h + 1) * DH_X)
            sc = lax.dot_general(q_ref[:, cols], mk_ref[:, cols], (((1,), (1,)), ((), ())),
                                 preferred_element_type=F32) * (DH_X ** -0.5)
            p = _softmax_rows(sc)
            att_ref[:, cols] = jnp.dot(p.astype(BF16), mv_ref[:, cols],
                                       preferred_element_type=F32).astype(BF16)
        co_block(att_ref, h1_ref, h2_ref, xn_ref)

    @pl.when(s == N_LOAD)
    def _():
        co_block(atts_ref, h1s_ref, h2s_ref, xns_ref)


def _attn_co(q, mk, mv, h1, att_s, h1_s, w_co, g3, *, tm, tiles_per_batch):
    t, d = h1.shape
    ts = h1_s.shape[0]
    mem_spec = pl.BlockSpec((N_MEM, d), lambda s: (jnp.maximum(s - N_LOAD, 0) // tiles_per_batch, 0))
    return pl.pallas_call(
        _attn_co_kernel,
        grid=(N_LOAD + t // tm,),
        in_specs=[_tile_spec(tm), mem_spec, mem_spec, _tile_spec(tm), _w_chunk_spec(), _resident((1, d)),
                  _resident((ts, d)), _resident((ts, d))],
        out_specs=[_tile_spec(tm), _tile_spec(tm), _whole_out((ts, d)), _whole_out((ts, d))],
        out_shape=[jax.ShapeDtypeStruct((t, d), F32), jax.ShapeDtypeStruct((t, d), BF16),
                   jax.ShapeDtypeStruct((ts, d), F32), jax.ShapeDtypeStruct((ts, d), BF16)],
        scratch_shapes=[pltpu.VMEM((d, d), BF16), pltpu.VMEM((tm, d), BF16)],
        compiler_params=_params(("arbitrary",)),
        name="attn_co",
    )(q, mk, mv, h1, w_co, g3, att_s, h1_s)


def _mix_prompt_kernel(u_ref, v_ref, q_ref, k_ref, vb_ref, g_ref, ws_ref, bt_ref, gn_ref,
                       o_ref, so_ref, st_ref, wt_ref, bias_ref, dm_ref, qd_ref, kd_ref, *, batch, n_chunks):
    n = pl.program_id(0)

    @pl.when(n == 0)
    def _():
        ii = lax.broadcasted_iota(jnp.int32, (CHUNK, CHUNK), 0)
        jj = lax.broadcasted_iota(jnp.int32, (CHUNK, CHUNK), 1)
        causal = ii >= jj
        diff = jnp.maximum((ii - jj).astype(F32), 0.0)
        ridx = ii.astype(F32)
        for h in range(H_A):
            lg = LOG_G[h]
            wt_ref[h] = (ws_ref[h] * causal.astype(F32)).astype(BF16)
            bias_ref[h] = jnp.broadcast_to(bt_ref[:, h:h + 1], (CHUNK, CH_A))
            dm_ref[h] = jnp.where(causal, jnp.exp(lg * diff), 0.0)
            qd_ref[h] = jnp.exp(lg * (ridx + 1.0))
            kd_ref[h] = jnp.exp(lg * (CHUNK - 1.0 - ridx))
        st_ref[...] = jnp.zeros_like(st_ref)

    for b in range(batch):
        for h in range(H_A):
            cols = slice(h * CH_A, (h + 1) * CH_A)
            mixed = jnp.dot(wt_ref[h], v_ref[b, :, cols], preferred_element_type=F32) + bias_ref[h]
            o_ref[b, :, cols] = (u_ref[b, :, cols].astype(F32) * mixed).astype(BF16)

            kh = k_ref[b, :, cols]
            vh = vb_ref[b, :, cols]
            state = st_ref[b, h]
            state_bf = state.astype(BF16)
            qh = q_ref[b, :, cols]
            scores = lax.dot_general(qh, kh, (((1,), (1,)), ((), ())), preferred_element_type=F32) * dm_ref[h]
            intra = jnp.dot(scores.astype(BF16), vh, preferred_element_type=F32)
            cross = jnp.dot(qh, state_bf, preferred_element_type=F32) * qd_ref[h]
            ret = intra + cross
            mu = jnp.mean(ret, axis=-1, keepdims=True)
            rc = ret - mu
            var = jnp.mean(rc * rc, axis=-1, keepdims=True)
            normed = rc * lax.rsqrt(var + EPS) * gn_ref[:, cols]
            o_ref[b, :, W_A + h * DV_B:W_A + (h + 1) * DV_B] = (
                g_ref[b, :, cols].astype(F32) * normed).astype(BF16)
            kd = (kh.astype(F32) * kd_ref[h]).astype(BF16)
            st_ref[b, h] = state * math.exp(LOG_G[h] * CHUNK) + lax.dot_general(
                kd, vh, (((0,), (0,)), ((), ())), preferred_element_type=F32)

    @pl.when(n == n_chunks - 1)
    def _():
        so_ref[...] = st_ref[...]


def _mix_prompt(u, rest, ws, bt, gn, *, n_chunks):
    batch, seq, _ = u.shape

    def sec(s):
        return pl.BlockSpec((batch, CHUNK, SEC), lambda n, s=s: (0, n, s))

    table = pltpu.VMEM((H_B, CHUNK, CHUNK), F32)
    return pl.pallas_call(
        functools.partial(_mix_prompt_kernel, batch=batch, n_chunks=n_chunks),
        grid=(n_chunks,),
        in_specs=[sec(0), sec(0), sec(1), sec(2), sec(3), sec(4),
                  _resident((H_A, CHUNK, CHUNK)), _resident((CHUNK, H_A)), _resident((1, W_B))],
        out_specs=[pl.BlockSpec((batch, CHUNK, W_A + W_B), lambda n: (0, n, 0)),
                   _whole_out((batch, H_B, DK_B, DV_B))],
        out_shape=[jax.ShapeDtypeStruct((batch, seq, W_A + W_B), BF16),
                   jax.ShapeDtypeStruct((batch, H_B, DK_B, DV_B), F32)],
        scratch_shapes=[pltpu.VMEM((batch, H_B, DK_B, DV_B), F32), pltpu.VMEM((H_A, CHUNK, CHUNK), BF16),
                        table, table, table, table],
        compiler_params=_params(("arbitrary",)),
        name="mix_prompt",
    )(u, rest, rest, rest, rest, rest, ws, bt, gn)


SAMPLE_ROWS = 16


def _mix_sample_kernel(u_ref, v_ref, q_ref, k_ref, vb_ref, g_ref, w0_ref, b0_ref, gn_ref, s_ref,
                       o_ref, so_ref):
    rid = lax.broadcasted_iota(jnp.int32, (SAMPLE_ROWS, 1), 0)
    o_ref[:, :W_A] = (u_ref[...] * (w0_ref[...] * v_ref[...] + b0_ref[...])).astype(BF16)

    for h in range(H_B):
        cols = slice(h * DK_B, (h + 1) * DK_B)
        g_h = math.exp(LOG_G[h])
        q_blk = q_ref[:, cols]
        k_blk = k_ref[:, cols]
        v_blk = vb_ref[:, cols]
        v_bf = v_blk.astype(BF16)
        intra = jnp.sum(q_blk * k_blk, axis=-1, keepdims=True) * v_blk
        cross = jnp.zeros((SAMPLE_ROWS, DV_B), F32)
        for r in range(SAMPLE_ROWS):
            state = s_ref[r, h]
            q_only_r = jnp.where(rid == r, q_blk, 0.0).astype(BF16)
            cross = cross + jnp.dot(q_only_r, state.astype(BF16), preferred_element_type=F32)
            k_only_r = jnp.where(rid == r, k_blk, 0.0).astype(BF16)
            outer = lax.dot_general(k_only_r, v_bf, (((0,), (0,)), ((), ())), preferred_element_type=F32)
            so_ref[r, h] = state * g_h + outer
        ret = intra + cross * g_h
        mu = jnp.mean(ret, axis=-1, keepdims=True)
        rc = ret - mu
        var = jnp.mean(rc * rc, axis=-1, keepdims=True)
        normed = rc * lax.rsqrt(var + EPS) * gn_ref[:, cols]
        o_ref[:, W_A + h * DV_B:W_A + (h + 1) * DV_B] = (g_ref[:, cols] * normed).astype(BF16)


def _mix_sample(u, rest, w0, b0, gn, state):
    t = u.shape[0]
    rb = SAMPLE_ROWS

    def sec(s):
        return pl.BlockSpec((None, rb, SEC), lambda i, s=s: (s, i, 0))

    return pl.pallas_call(
        _mix_sample_kernel,
        grid=(t // rb,),
        in_specs=[pl.BlockSpec((rb, SEC), lambda i: (i, 0)), sec(0), sec(1), sec(2), sec(3), sec(4),
                  _resident((1, W_A)), _resident((1, W_A)), _resident((1, W_B)),
                  pl.BlockSpec((rb, H_B, DK_B, DV_B), lambda i: (i, 0, 0, 0))],
        out_specs=[pl.BlockSpec((rb, W_A + W_B), lambda i: (i, 0)),
                   pl.BlockSpec((rb, H_B, DK_B, DV_B), lambda i: (i, 0, 0, 0))],
        out_shape=[jax.ShapeDtypeStruct((t, W_A + W_B), BF16),
                   jax.ShapeDtypeStruct(state.shape, F32)],
        compiler_params=_params(("arbitrary",)),
        name="mix_sample",
    )(u, rest, rest, rest, rest, rest, w0, b0, gn, state)


ATTN_SAMPLE_ROWS = 4
SUBLANES = 8


def _attn_sample_kernel(q_ref, ck_ref, cv_ref, o_ref):
    i = pl.program_id(0)
    pairs = N_MEM * H_X
    fold = lambda t: pltpu.roll(t, H_X, 1)
    for r in range(ATTN_SAMPLE_ROWS):
        row = pl.ds(i * ATTN_SAMPLE_ROWS + r, 1)
        q = q_ref[row, :]
        q8 = jnp.concatenate([q[:, (s % H_X) * DH_X:(s % H_X + 1) * DH_X] for s in range(SUBLANES)], axis=0)
        k3 = ck_ref[0, r].reshape(pairs // SUBLANES, SUBLANES, DH_X)
        v3 = cv_ref[0, r].reshape(pairs // SUBLANES, SUBLANES, DH_X)
        s = jnp.sum(k3 * q8[None], axis=-1, keepdims=True) * (DH_X ** -0.5)
        m = jnp.max(s, axis=0, keepdims=True)
        m = jnp.maximum(m, fold(m))
        e = jnp.exp(s - m)
        l = jnp.sum(e, axis=0, keepdims=True)
        acc = jnp.sum(e * v3, axis=0, keepdims=True)
        out = ((acc + fold(acc)) / (l + fold(l)))[0]
        for h in range(H_X):
            o_ref[row, h * DH_X:(h + 1) * DH_X] = out[h:h + 1, :]


def _attn_sample(q, ck, cv):
    t, d = q.shape
    rb = ATTN_SAMPLE_ROWS
    cache_spec = pl.BlockSpec((1, rb, N_MEM, H_X, DH_X), lambda i: (0, i, 0, 0, 0))
    return pl.pallas_call(
        _attn_sample_kernel,
        grid=(t // rb,),
        in_specs=[_resident((t, d)), cache_spec, cache_spec],
        out_specs=_whole_out((t, d)),
        out_shape=jax.ShapeDtypeStruct((t, d), F32),
        compiler_params=_params(("arbitrary",)),
        name="attn_sample",
    )(q, ck, cv)


def _rope_tables(pos):
    half = DK_B // 2
    freqs = ROPE_THETA ** (-jnp.arange(half, dtype=F32) / half)
    ang = pos[:, None] * freqs[None, :]
    cos = jnp.cos(ang)
    sin = jnp.sin(ang)
    return jnp.concatenate([cos, cos], axis=-1), jnp.concatenate([-sin, sin], axis=-1)


def kernel(x_prompt, x_sample, mem_prompt, cache_mem_k, cache_mem_v, state_ret, norm1_g, w_in, sgu_norm_g, sgu_w_s, sgu_b, ret_gn_g, w_out, norm2_g, mem_norm_g, w_cq, w_ck, w_cv, w_co, norm3_g, w_ff1, w_ff2, final_norm_g):
    batch, seq, d = x_prompt.shape
    ts = x_sample.shape[0]
    n_chunks = seq // CHUNK
    tp = batch * seq

    g1 = norm1_g[0][None, :]
    g2 = norm2_g[0][None, :]
    g3 = norm3_g[0][None, :]
    gm = mem_norm_g[0][None, :]
    gf = final_norm_g[None, :]
    lng = sgu_norm_g[0][None, :]
    gn = ret_gn_g[0][None, :]
    ws = sgu_w_s[0]
    sb = sgu_b[0]
    xp = x_prompt.reshape(tp, d)
    xs = x_sample.reshape(ts, d)

    rope_p = _rope_tables(jnp.arange(seq, dtype=F32))
    rope_s = _rope_tables(jnp.full((ts,), PAST_LEN, dtype=F32))
    u_p, xn_p, u_s, xn_s = _in_proj_u(xp, xs, g1, w_in[0], tm=TM_IN_PROJ)
    rest_p, rest_s, v_rows = _in_proj_rest(xn_p, xn_s, w_in[0], rope_p, rope_s, lng, tm=TM_IN_PROJ)
    mix_p, state_p = _mix_prompt(u_p.reshape(batch, seq, SEC), rest_p.reshape(batch, seq, (N_SEC - 1) * SEC),
                                 ws, sb.T, gn, n_chunks=n_chunks)
    mix_p = mix_p.reshape(tp, W_A + W_B)
    w0 = jnp.repeat(ws[:, 0, 0], CH_A)[None, :]
    b0 = jnp.repeat(sb[:, 0], CH_A)[None, :]
    mix_s, state_s = _mix_sample(u_s, rest_s, w0, b0, gn, state_ret[0])

    mem = mem_prompt.reshape(batch * N_MEM, d)
    mk5, mk = _mem_proj(mem, gm, w_ck[0], batch=batch, name="mem_k")
    mv5, mv = _mem_proj(mem, gm, w_cv[0], batch=batch, name="mem_v")
    h1_p, q_p, h1_s, q_s = _out_q(mix_p, xp, mix_s, xs, w_out[0], w_cq[0], g2, tm=TM_RESIDENT)
    att_s = _attn_sample(q_s, cache_mem_k, cache_mem_v)
    h2_p, xn3_p, h2_s, xn3_s = _attn_co(q_p, mk, mv, h1_p, att_s, h1_s, w_co[0], g3,
                                        tm=TM_RESIDENT, tiles_per_batch=seq // TM_RESIDENT)

    hid_p, hid_s = _ff1(xn3_p, xn3_s, w_ff1[0], tm=TM_MLP, tn=TN_MLP)
    part_p, part_s = _ff2(hid_p, hid_s, w_ff2[0], tm=TM_MLP, tn=TN_MLP, tk=TK_MLP)
    y_p = _add_norm(part_p, h2_p, gf, tm=TM_NORM)
    y_s = _add_norm_sample(part_s, h2_s, gf)

    return (y_p.reshape(batch, seq, d),
            y_s,
            mk5,
            mv5,
            state_p[None],
            state_s[None],
            v_rows[None])
```

```python
import functools
import math

import jax
import jax.numpy as jnp
from jax import lax
from jax.experimental import pallas as pl
from jax.experimental.pallas import tpu as pltpu

F32 = jnp.float32
BF16 = jnp.bfloat16

D_MODEL = 2048
H_A = 8
CH_A = 128
W_A = H_A * CH_A
CHUNK = 128
H_B = 8
DK_B = 128
DV_B = 128
W_B = H_B * DV_B
ROPE_THETA = 10000.0
N_MEM = 256
H_X = 4
DH_X = D_MODEL // H_X
EPS = 1e-6
PAST_LEN = 16384

SEC = 1024
N_SEC = 6
LOG_G = tuple(math.log(1.0 - 2.0 ** (-5.0 - h)) for h in range(H_B))

V7X_VMEM_BYTES = 64 * 1024 * 1024
COMPILER_SCRATCH_BYTES = 4 * 1024 * 1024
VMEM_LIMIT = V7X_VMEM_BYTES - COMPILER_SCRATCH_BYTES
ROW_CHUNK = 256

TM_IN_PROJ = 1024
TM_RESIDENT = 512
TM_MLP, TN_MLP, TK_MLP = 2048, 1024, 1024
TM_NORM = 1024


def _params(sem):
    return pltpu.CompilerParams(dimension_semantics=sem, vmem_limit_bytes=VMEM_LIMIT)


def _resident(shape):
    zeros = (0,) * len(shape)
    return pl.BlockSpec(shape, lambda *_: zeros, pipeline_mode=pl.Buffered(1))


def _whole_out(shape):
    zeros = (0,) * len(shape)
    return pl.BlockSpec(shape, lambda *_: zeros)


def _row_loop(n_rows, fn):
    rc = min(n_rows, ROW_CHUNK)
    assert n_rows % rc == 0

    def body(c, carry):
        fn(pl.ds(pl.multiple_of(c * rc, rc), rc))
        return carry

    lax.fori_loop(0, n_rows // rc, body, 0)


def _rms(h, g_ref):
    ms = jnp.mean(h * h, axis=-1, keepdims=True)
    return h * lax.rsqrt(ms + EPS) * g_ref[...]


def _rms_rows(x_ref, g_ref, xn_ref, n_rows):
    def fn(rows):
        xn_ref[rows, :] = _rms(x_ref[rows, :], g_ref).astype(BF16)

    _row_loop(n_rows, fn)


def _cast_rows(w_ref, wb_ref):
    def fn(rows):
        wb_ref[rows, :] = w_ref[rows, :].astype(BF16)

    _row_loop(w_ref.shape[0], fn)


def _inproj_u_kernel(x_ref, g_ref, w_ref, xs_ref, u_ref, xn_ref, us_ref, xsn_ref, wb_ref, *, tm, ts):
    @pl.when(pl.program_id(0) == 0)
    def _():
        _cast_rows(w_ref, wb_ref)
        _rms_rows(xs_ref, g_ref, xsn_ref, ts)
        us_ref[...] = jax.nn.gelu(jnp.dot(xsn_ref[...], wb_ref[...], preferred_element_type=F32))

    _rms_rows(x_ref, g_ref, xn_ref, tm)
    u_ref[...] = jax.nn.gelu(jnp.dot(xn_ref[...], wb_ref[...], preferred_element_type=F32)).astype(BF16)


def _in_proj_u(x, xs, g, w, *, tm):
    t, d = x.shape
    ts = xs.shape[0]
    return pl.pallas_call(
        functools.partial(_inproj_u_kernel, tm=tm, ts=ts),
        grid=(t // tm,),
        in_specs=[pl.BlockSpec((tm, d), lambda i: (i, 0)), _resident((1, d)),
                  pl.BlockSpec((d, SEC), lambda i: (0, 0), pipeline_mode=pl.Buffered(1)),
                  _resident((ts, d))],
        out_specs=[pl.BlockSpec((tm, SEC), lambda i: (i, 0)), pl.BlockSpec((tm, d), lambda i: (i, 0)),
                   _whole_out((ts, SEC)), _whole_out((ts, d))],
        out_shape=[jax.ShapeDtypeStruct((t, SEC), BF16), jax.ShapeDtypeStruct((t, d), BF16),
                   jax.ShapeDtypeStruct((ts, SEC), F32), jax.ShapeDtypeStruct((ts, d), BF16)],
        scratch_shapes=[pltpu.VMEM((d, SEC), BF16)],
        compiler_params=_params(("arbitrary",)),
        name="in_proj_u",
    )(x, g, w, xs)


def _inproj_rest_kernel(xn_ref, w_ref, cc_ref, ss_ref, lng_ref, xsn_ref, ccs_ref, sss_ref,
                        o_ref, os_ref, vrows_ref, wb_ref):
    j = pl.program_id(0)
    i = pl.program_id(1)

    @pl.when(i == 0)
    def _():
        _cast_rows(w_ref, wb_ref)

    def layernorm(acc, cc, ss):
        z = jax.nn.gelu(acc)
        mu = jnp.mean(z, axis=-1, keepdims=True)
        zc = z - mu
        var = jnp.mean(zc * zc, axis=-1, keepdims=True)
        return zc * lax.rsqrt(var + EPS) * lng_ref[...]

    def rotary(acc, cc, ss):
        scale = jnp.where(j == 2, DK_B ** -0.5, 1.0).astype(F32)
        heads = []
        for h in range(H_B):
            blk = acc[:, h * DK_B:(h + 1) * DK_B]
            rot = pltpu.roll(blk, DK_B // 2, 1)
            heads.append((blk * cc + rot * ss) * scale)
        return jnp.concatenate(heads, axis=-1)

    def section(pred, epilogue):
        @pl.when(pred)
        def _():
            acc = jnp.dot(xn_ref[...], wb_ref[...], preferred_element_type=F32)
            o_ref[...] = epilogue(acc, cc_ref[...], ss_ref[...]).astype(o_ref.dtype)

            @pl.when(i == 0)
            def _():
                acc_s = jnp.dot(xsn_ref[...], wb_ref[...], preferred_element_type=F32)
                os_ref[j] = epilogue(acc_s, ccs_ref[...], sss_ref[...])

    section(j == 0, layernorm)
    section((j == 1) | (j == 2), rotary)
    section(j == 3, lambda acc, cc, ss: acc)
    section(j == 4, lambda acc, cc, ss: jax.nn.silu(acc))

    @pl.when((j == 0) & (i == 0))
    def _():
        for h in range(H_A):
            vrows_ref[:, 0, h, :] = os_ref[0, :, h * CH_A:(h + 1) * CH_A]


def _in_proj_rest(xn, xsn, w, rope_p, rope_s, lng, *, tm):
    t, d = xn.shape
    ts = xsn.shape[0]
    n_rest = N_SEC - 1
    rope_blocks = rope_p[0].shape[0] // tm
    rope_spec = pl.BlockSpec((tm, DK_B), lambda j, i: (i % rope_blocks, 0))
    return pl.pallas_call(
        _inproj_rest_kernel,
        grid=(n_rest, t // tm),
        in_specs=[pl.BlockSpec((tm, d), lambda j, i: (i, 0)),
                  pl.BlockSpec((d, SEC), lambda j, i: (0, j + 1)),
                  rope_spec, rope_spec, _resident((1, SEC)),
                  _resident((ts, d)), _resident((ts, DK_B)), _resident((ts, DK_B))],
        out_specs=[pl.BlockSpec((tm, SEC), lambda j, i: (i, j)), _whole_out((n_rest, ts, SEC)),
                   _whole_out((ts, 1, H_A, CH_A))],
        out_shape=[jax.ShapeDtypeStruct((t, n_rest * SEC), BF16),
                   jax.ShapeDtypeStruct((n_rest, ts, SEC), F32),
                   jax.ShapeDtypeStruct((ts, 1, H_A, CH_A), F32)],
        scratch_shapes=[pltpu.VMEM((d, SEC), BF16)],
        compiler_params=_params(("arbitrary", "arbitrary")),
        name="in_proj_rest",
    )(xn, w, rope_p[0], rope_p[1], lng, xsn, rope_s[0], rope_s[1])


def _mem_kernel(x_ref, g_ref, w_ref, o5_ref, o2_ref, wb_ref):
    @pl.when(pl.program_id(0) == 0)
    def _():
        _cast_rows(w_ref, wb_ref)

    res = jnp.dot(_rms(x_ref[...], g_ref).astype(BF16), wb_ref[...], preferred_element_type=F32)
    o2_ref[...] = res.astype(BF16)
    for h in range(H_X):
        o5_ref[0, 0, :, h, :] = res[:, h * DH_X:(h + 1) * DH_X]


def _mem_proj(mem, g, w, *, batch, name):
    d = mem.shape[1]
    return pl.pallas_call(
        _mem_kernel,
        grid=(batch,),
        in_specs=[pl.BlockSpec((N_MEM, d), lambda b: (b, 0)), _resident((1, d)), _resident((d, d))],
        out_specs=[pl.BlockSpec((1, 1, N_MEM, H_X, DH_X), lambda b: (0, b, 0, 0, 0)),
                   pl.BlockSpec((N_MEM, d), lambda b: (b, 0))],
        out_shape=[jax.ShapeDtypeStruct((1, batch, N_MEM, H_X, DH_X), F32),
                   jax.ShapeDtypeStruct((batch * N_MEM, d), BF16)],
        scratch_shapes=[pltpu.VMEM((d, d), BF16)],
        compiler_params=_params(("arbitrary",)),
        name=name,
    )(mem, g, w)


def _relu2(acc):
    return jnp.square(jnp.maximum(acc, 0.0))


def _ff1_kernel(x_ref, w_ref, xs_ref, o_ref, os_ref):
    o_ref[...] = _relu2(jnp.dot(x_ref[...], w_ref[...].astype(BF16), preferred_element_type=F32)).astype(BF16)

    @pl.when(pl.program_id(0) == 0)
    def _():
        os_ref[pl.program_id(1)] = _relu2(jnp.dot(xs_ref[...], w_ref[...].astype(BF16),
                                                  preferred_element_type=F32)).astype(BF16)


def _ff1(x, xs, w, *, tm, tn):
    t, k = x.shape
    ts = xs.shape[0]
    n = w.shape[1]
    return pl.pallas_call(
        _ff1_kernel,
        grid=(t // tm, n // tn),
        in_specs=[pl.BlockSpec((tm, k), lambda i, j: (i, 0)),
                  pl.BlockSpec((k, tn), lambda i, j: (0, j)),
                  _resident((ts, k))],
        out_specs=[pl.BlockSpec((tm, tn), lambda i, j: (i, j)), _whole_out((n // tn, ts, tn))],
        out_shape=[jax.ShapeDtypeStruct((t, n), BF16), jax.ShapeDtypeStruct((n // tn, ts, tn), BF16)],
        compiler_params=_params(("arbitrary", "arbitrary")),
        name="ff1",
    )(x, w, xs)


def _ff2_kernel(x_ref, w_ref, r_ref, xs_ref, o_ref, os_ref):
    i, j, k = pl.program_id(0), pl.program_id(1), pl.program_id(2)

    @pl.when(k == 0)
    def _():
        o_ref[...] = r_ref[...]

    o_ref[...] += jnp.dot(x_ref[...], w_ref[...].astype(BF16), preferred_element_type=F32)

    @pl.when(i == 0)
    def _():
        part = jnp.dot(xs_ref[k], w_ref[...].astype(BF16), preferred_element_type=F32)

        @pl.when(k == 0)
        def _():
            os_ref[j] = part

        @pl.when(k > 0)
        def _():
            os_ref[j] += part


def _ff2(x, xs, w, r, *, tm, tn, tk):
    t, k = x.shape
    ts = xs.shape[1]
    n = w.shape[1]
    assert xs.shape == (k // tk, ts, tk), "sample tiles must be stacked in contraction slices of tk"
    return pl.pallas_call(
        _ff2_kernel,
        grid=(t // tm, n // tn, k // tk),
        in_specs=[pl.BlockSpec((tm, tk), lambda i, j, kk: (i, kk)),
                  pl.BlockSpec((tk, tn), lambda i, j, kk: (kk, j)),
                  pl.BlockSpec((tm, tn), lambda i, j, kk: (i, j)),
                  _resident((k // tk, ts, tk))],
        out_specs=[pl.BlockSpec((tm, tn), lambda i, j, kk: (i, j)), _whole_out((n // tn, ts, tn))],
        out_shape=[jax.ShapeDtypeStruct((t, n), F32), jax.ShapeDtypeStruct((n // tn, ts, tn), F32)],
        compiler_params=_params(("arbitrary", "arbitrary", "arbitrary")),
        name="ff2",
    )(x, w, r, xs)


def _final_norm_kernel(x_ref, g_ref, o_ref, *, tm):
    def fn(rows):
        o_ref[rows, :] = _rms(x_ref[rows, :], g_ref)

    _row_loop(tm, fn)


def _final_norm(x, g, *, tm):
    t, d = x.shape
    row = pl.BlockSpec((tm, d), lambda i: (i, 0))
    return pl.pallas_call(
        functools.partial(_final_norm_kernel, tm=tm),
        grid=(t // tm,),
        in_specs=[row, _resident((1, d))],
        out_specs=row,
        out_shape=jax.ShapeDtypeStruct((t, d), F32),
        compiler_params=_params(("arbitrary",)),
        name="final_norm",
    )(x, g)


def _add_norm_sample_kernel(x_ref, r_ref, g_ref, o_ref):
    y = r_ref[...] + jnp.concatenate([x_ref[j] for j in range(x_ref.shape[0])], axis=-1)
    o_ref[:, 0, :] = _rms(y, g_ref)


def _add_norm_sample(x, r, g):
    ts, d = r.shape
    return pl.pallas_call(
        _add_norm_sample_kernel,
        grid=(1,),
        in_specs=[_resident(x.shape), _resident((ts, d)), _resident((1, d))],
        out_specs=_whole_out((ts, 1, d)),
        out_shape=jax.ShapeDtypeStruct((ts, 1, d), F32),
        compiler_params=_params(("arbitrary",)),
        name="final_norm_sample",
    )(x, r, g)


W_CHUNK = 256
N_LOAD = D_MODEL // W_CHUNK


def _load_weight_chunk(s, pairs):
    rows = pl.ds(pl.multiple_of(s * W_CHUNK, W_CHUNK), W_CHUNK)
    for src, dst in pairs:
        dst[rows, :] = src[...].astype(BF16)


def _w_chunk_spec():
    return pl.BlockSpec((W_CHUNK, D_MODEL), lambda s: (jnp.minimum(s, N_LOAD - 1), 0))


def _tile_spec(tm, width=D_MODEL):
    return pl.BlockSpec((tm, width), lambda s: (jnp.maximum(s - N_LOAD, 0), 0))


def _outq_kernel(mix_ref, x_ref, wo_ref, wq_ref, g2_ref, mixs_ref, xs_ref,
                 h1_ref, q_ref, h1s_ref, qs_ref, wo_bf, wq_bf):
    s = pl.program_id(0)

    def block(mix_r, x_r, h1_r, q_r):
        h1 = x_r[...] + jnp.dot(mix_r[...], wo_bf[...], preferred_element_type=F32)
        h1_r[...] = h1
        q_r[...] = jnp.dot(_rms(h1, g2_ref).astype(BF16), wq_bf[...],
                           preferred_element_type=F32).astype(q_r.dtype)

    @pl.when(s < N_LOAD)
    def _():
        _load_weight_chunk(s, ((wo_ref, wo_bf), (wq_ref, wq_bf)))

    @pl.when(s >= N_LOAD)
    def _():
        block(mix_ref, x_ref, h1_ref, q_ref)

    @pl.when(s == N_LOAD)
    def _():
        block(mixs_ref, xs_ref, h1s_ref, qs_ref)


def _out_q(mix, x, mix_s, xs, w_out, w_cq, g2, *, tm):
    t, d = x.shape
    ts = xs.shape[0]
    return pl.pallas_call(
        _outq_kernel,
        grid=(N_LOAD + t // tm,),
        in_specs=[_tile_spec(tm), _tile_spec(tm), _w_chunk_spec(), _w_chunk_spec(), _resident((1, d)),
                  _resident((ts, d)), _resident((ts, d))],
        out_specs=[_tile_spec(tm), _tile_spec(tm), _whole_out((ts, d)), _whole_out((ts, d))],
        out_shape=[jax.ShapeDtypeStruct((t, d), F32), jax.ShapeDtypeStruct((t, d), BF16),
                   jax.ShapeDtypeStruct((ts, d), F32), jax.ShapeDtypeStruct((ts, d), F32)],
        scratch_shapes=[pltpu.VMEM((d, d), BF16), pltpu.VMEM((d, d), BF16)],
        compiler_params=_params(("arbitrary",)),
        name="out_q",
    )(mix, x, w_out, w_cq, g2, mix_s, xs)


def _softmax_rows(s):
    m = jnp.max(s, axis=-1, keepdims=True)
    e = jnp.exp(s - m)
    return e / jnp.sum(e, axis=-1, keepdims=True)


def _attn_co_kernel(q_ref, mk_ref, mv_ref, h1_ref, wc_ref, g3_ref, atts_ref, h1s_ref,
                    h2_ref, xn_ref, h2s_ref, xns_ref, wc_bf, att_ref):
    s = pl.program_id(0)

    def co_block(att_r, h1_r, h2_r, xn_r):
        h2 = h1_r[...] + jnp.dot(att_r[...].astype(BF16), wc_bf[...], preferred_element_type=F32)
        h2_r[...] = h2
        xn_r[...] = _rms(h2, g3_ref).astype(BF16)

    @pl.when(s < N_LOAD)
    def _():
        _load_weight_chunk(s, ((wc_ref, wc_bf),))

    @pl.when(s >= N_LOAD)
    def _():
        for h in range(H_X):
            cols = slice(h * DH_X, (h + 1) * DH_X)
            sc = lax.dot_general(q_ref[:, cols], mk_ref[:, cols], (((1,), (1,)), ((), ())),
                                 preferred_element_type=F32) * (DH_X ** -0.5)
            p = _softmax_rows(sc)
            att_ref[:, cols] = jnp.dot(p.astype(BF16), mv_ref[:, cols],
                                       preferred_element_type=F32).astype(BF16)
        co_block(att_ref, h1_ref, h2_ref, xn_ref)

    @pl.when(s == N_LOAD)
    def _():
        co_block(atts_ref, h1s_ref, h2s_ref, xns_ref)


def _attn_co(q, mk, mv, h1, att_s, h1_s, w_co, g3, *, tm, tiles_per_batch):
    t, d = h1.shape
    ts = h1_s.shape[0]
    mem_spec = pl.BlockSpec((N_MEM, d), lambda s: (jnp.maximum(s - N_LOAD, 0) // tiles_per_batch, 0))
    return pl.pallas_call(
        _attn_co_kernel,
        grid=(N_LOAD + t // tm,),
        in_specs=[_tile_spec(tm), mem_spec, mem_spec, _tile_spec(tm), _w_chunk_spec(), _resident((1, d)),
                  _resident((ts, d)), _resident((ts, d))],
        out_specs=[_tile_spec(tm), _tile_spec(tm), _whole_out((ts, d)), _whole_out((ts, d))],
        out_shape=[jax.ShapeDtypeStruct((t, d), F32), jax.ShapeDtypeStruct((t, d), BF16),
                   jax.ShapeDtypeStruct((ts, d), F32), jax.ShapeDtypeStruct((ts, d), BF16)],
        scratch_shapes=[pltpu.VMEM((d, d), BF16), pltpu.VMEM((tm, d), BF16)],
        compiler_params=_params(("arbitrary",)),
        name="attn_co",
    )(q, mk, mv, h1, w_co, g3, att_s, h1_s)


def _mix_prompt_kernel(u_ref, v_ref, q_ref, k_ref, vb_ref, g_ref, ws_ref, bt_ref, gn_ref,
                       o_ref, so_ref, st_ref, wt_ref, bias_ref, dm_ref, qd_ref, kd_ref, *, batch, n_chunks):
    n = pl.program_id(0)

    @pl.when(n == 0)
    def _():
        ii = lax.broadcasted_iota(jnp.int32, (CHUNK, CHUNK), 0)
        jj = lax.broadcasted_iota(jnp.int32, (CHUNK, CHUNK), 1)
        causal = ii >= jj
        diff = jnp.maximum((ii - jj).astype(F32), 0.0)
        ridx = ii.astype(F32)
        for h in range(H_A):
            lg = LOG_G[h]
            wt_ref[h] = (ws_ref[h] * causal.astype(F32)).astype(BF16)
            bias_ref[h] = jnp.broadcast_to(bt_ref[:, h:h + 1], (CHUNK, CH_A))
            dm_ref[h] = jnp.where(causal, jnp.exp(lg * diff), 0.0)
            qd_ref[h] = jnp.exp(lg * (ridx + 1.0))
            kd_ref[h] = jnp.exp(lg * (CHUNK - 1.0 - ridx))
        st_ref[...] = jnp.zeros_like(st_ref)

    for b in range(batch):
        for h in range(H_A):
            cols = slice(h * CH_A, (h + 1) * CH_A)
            mixed = jnp.dot(wt_ref[h], v_ref[b, :, cols], preferred_element_type=F32) + bias_ref[h]
            o_ref[b, :, cols] = (u_ref[b, :, cols].astype(F32) * mixed).astype(BF16)

            kh = k_ref[b, :, cols]
            vh = vb_ref[b, :, cols]
            state = st_ref[b, h]
            state_bf = state.astype(BF16)
            qh = q_ref[b, :, cols]
            scores = lax.dot_general(qh, kh, (((1,), (1,)), ((), ())), preferred_element_type=F32) * dm_ref[h]
            intra = jnp.dot(scores.astype(BF16), vh, preferred_element_type=F32)
            cross = jnp.dot(qh, state_bf, preferred_element_type=F32) * qd_ref[h]
            ret = intra + cross
            mu = jnp.mean(ret, axis=-1, keepdims=True)
            rc = ret - mu
            var = jnp.mean(rc * rc, axis=-1, keepdims=True)
            normed = rc * lax.rsqrt(var + EPS) * gn_ref[:, cols]
            o_ref[b, :, W_A + h * DV_B:W_A + (h + 1) * DV_B] = (
                g_ref[b, :, cols].astype(F32) * normed).astype(BF16)
            kd = (kh.astype(F32) * kd_ref[h]).astype(BF16)
            st_ref[b, h] = state * math.exp(LOG_G[h] * CHUNK) + lax.dot_general(
                kd, vh, (((0,), (0,)), ((), ())), preferred_element_type=F32)

    @pl.when(n == n_chunks - 1)
    def _():
        so_ref[...] = st_ref[...]


def _mix_prompt(u, rest, ws, bt, gn, *, n_chunks):
    batch, seq, _ = u.shape

    def sec(s):
        return pl.BlockSpec((batch, CHUNK, SEC), lambda n, s=s: (0, n, s))

    table = pltpu.VMEM((H_B, CHUNK, CHUNK), F32)
    return pl.pallas_call(
        functools.partial(_mix_prompt_kernel, batch=batch, n_chunks=n_chunks),
        grid=(n_chunks,),
        in_specs=[sec(0), sec(0), sec(1), sec(2), sec(3), sec(4),
                  _resident((H_A, CHUNK, CHUNK)), _resident((CHUNK, H_A)), _resident((1, W_B))],
        out_specs=[pl.BlockSpec((batch, CHUNK, W_A + W_B), lambda n: (0, n, 0)),
                   _whole_out((batch, H_B, DK_B, DV_B))],
        out_shape=[jax.ShapeDtypeStruct((batch, seq, W_A + W_B), BF16),
                   jax.ShapeDtypeStruct((batch, H_B, DK_B, DV_B), F32)],
        scratch_shapes=[pltpu.VMEM((batch, H_B, DK_B, DV_B), F32), pltpu.VMEM((H_A, CHUNK, CHUNK), BF16),
                        table, table, table, table],
        compiler_params=_params(("arbitrary",)),
        name="mix_prompt",
    )(u, rest, rest, rest, rest, rest, ws, bt, gn)


SAMPLE_ROWS = 16


def _mix_sample_kernel(u_ref, v_ref, q_ref, k_ref, vb_ref, g_ref, w0_ref, b0_ref, gn_ref, s_ref,
                       o_ref, so_ref):
    rid = lax.broadcasted_iota(jnp.int32, (SAMPLE_ROWS, 1), 0)
    o_ref[:, :W_A] = (u_ref[...] * (w0_ref[...] * v_ref[...] + b0_ref[...])).astype(BF16)

    for h in range(H_B):
        cols = slice(h * DK_B, (h + 1) * DK_B)
        g_h = math.exp(LOG_G[h])
        q_blk = q_ref[:, cols]
        k_blk = k_ref[:, cols]
        v_blk = vb_ref[:, cols]
        v_bf = v_blk.astype(BF16)
        intra = jnp.sum(q_blk * k_blk, axis=-1, keepdims=True) * v_blk
        cross = jnp.zeros((SAMPLE_ROWS, DV_B), F32)
        for r in range(SAMPLE_ROWS):
            state = s_ref[r, h]
            q_only_r = jnp.where(rid == r, q_blk, 0.0).astype(BF16)
            cross = cross + jnp.dot(q_only_r, state.astype(BF16), preferred_element_type=F32)
            k_only_r = jnp.where(rid == r, k_blk, 0.0).astype(BF16)
            outer = lax.dot_general(k_only_r, v_bf, (((0,), (0,)), ((), ())), preferred_element_type=F32)
            so_ref[r, h] = state * g_h + outer
        ret = intra + cross * g_h
        mu = jnp.mean(ret, axis=-1, keepdims=True)
        rc = ret - mu
        var = jnp.mean(rc * rc, axis=-1, keepdims=True)
        normed = rc * lax.rsqrt(var + EPS) * gn_ref[:, cols]
        o_ref[:, W_A + h * DV_B:W_A + (h + 1) * DV_B] = (g_ref[:, cols] * normed).astype(BF16)


def _mix_sample(u, rest, w0, b0, gn, state):
    t = u.shape[0]
    rb = SAMPLE_ROWS

    def sec(s):
        return pl.BlockSpec((None, rb, SEC), lambda i, s=s: (s, i, 0))

    return pl.pallas_call(
        _mix_sample_kernel,
        grid=(t // rb,),
        in_specs=[pl.BlockSpec((rb, SEC), lambda i: (i, 0)), sec(0), sec(1), sec(2), sec(3), sec(4),
                  _resident((1, W_A)), _resident((1, W_A)), _resident((1, W_B)),
                  pl.BlockSpec((rb, H_B, DK_B, DV_B), lambda i: (i, 0, 0, 0))],
        out_specs=[pl.BlockSpec((rb, W_A + W_B), lambda i: (i, 0)),
                   pl.BlockSpec((rb, H_B, DK_B, DV_B), lambda i: (i, 0, 0, 0))],
        out_shape=[jax.ShapeDtypeStruct((t, W_A + W_B), BF16),
                   jax.ShapeDtypeStruct(state.shape, F32)],
        compiler_params=_params(("arbitrary",)),
        name="mix_sample",
    )(u, rest, rest, rest, rest, rest, w0, b0, gn, state)


ATTN_SAMPLE_ROWS = 4
SUBLANES = 8


def _attn_sample_kernel(q_ref, ck_ref, cv_ref, o_ref):
    i = pl.program_id(0)
    pairs = N_MEM * H_X
    fold = lambda t: pltpu.roll(t, H_X, 1)
    for r in range(ATTN_SAMPLE_ROWS):
        row = pl.ds(i * ATTN_SAMPLE_ROWS + r, 1)
        q = q_ref[row, :]
        q8 = jnp.concatenate([q[:, (s % H_X) * DH_X:(s % H_X + 1) * DH_X] for s in range(SUBLANES)], axis=0)
        k3 = ck_ref[0, r].reshape(pairs // SUBLANES, SUBLANES, DH_X)
        v3 = cv_ref[0, r].reshape(pairs // SUBLANES, SUBLANES, DH_X)
        s = jnp.sum(k3 * q8[None], axis=-1, keepdims=True) * (DH_X ** -0.5)
        m = jnp.max(s, axis=0, keepdims=True)
        m = jnp.maximum(m, fold(m))
        e = jnp.exp(s - m)
        l = jnp.sum(e, axis=0, keepdims=True)
        acc = jnp.sum(e * v3, axis=0, keepdims=True)
        out = ((acc + fold(acc)) / (l + fold(l)))[0]
        for h in range(H_X):
            o_ref[row, h * DH_X:(h + 1) * DH_X] = out[h:h + 1, :]


def _attn_sample(q, ck, cv):
    t, d = q.shape
    rb = ATTN_SAMPLE_ROWS
    cache_spec = pl.BlockSpec((1, rb, N_MEM, H_X, DH_X), lambda i: (0, i, 0, 0, 0))
    return pl.pallas_call(
        _attn_sample_kernel,
        grid=(t // rb,),
        in_specs=[_resident((t, d)), cache_spec, cache_spec],
        out_specs=_whole_out((t, d)),
        out_shape=jax.ShapeDtypeStruct((t, d), F32),
        compiler_params=_params(("arbitrary",)),
        name="attn_sample",
    )(q, ck, cv)


def _rope_tables(pos):
    half = DK_B // 2
    freqs = ROPE_THETA ** (-jnp.arange(half, dtype=F32) / half)
    ang = pos[:, None] * freqs[None, :]
    cos = jnp.cos(ang)
    sin = jnp.sin(ang)
    return jnp.concatenate([cos, cos], axis=-1), jnp.concatenate([-sin, sin], axis=-1)


def kernel(x_prompt, x_sample, mem_prompt, cache_mem_k, cache_mem_v, state_ret, norm1_g, w_in, sgu_norm_g, sgu_w_s, sgu_b, ret_gn_g, w_out, norm2_g, mem_norm_g, w_cq, w_ck, w_cv, w_co, norm3_g, w_ff1, w_ff2, final_norm_g):
    batch, seq, d = x_prompt.shape
    ts = x_sample.shape[0]
    n_chunks = seq // CHUNK
    tp = batch * seq

    g1 = norm1_g[0][None, :]
    g2 = norm2_g[0][None, :]
    g3 = norm3_g[0][None, :]
    gm = mem_norm_g[0][None, :]
    gf = final_norm_g[None, :]
    lng = sgu_norm_g[0][None, :]
    gn = ret_gn_g[0][None, :]
    ws = sgu_w_s[0]
    sb = sgu_b[0]
    xp = x_prompt.reshape(tp, d)
    xs = x_sample.reshape(ts, d)

    rope_p = _rope_tables(jnp.arange(seq, dtype=F32))
    rope_s = _rope_tables(jnp.full((ts,), PAST_LEN, dtype=F32))
    u_p, xn_p, u_s, xn_s = _in_proj_u(xp, xs, g1, w_in[0], tm=TM_IN_PROJ)
    rest_p, rest_s, v_rows = _in_proj_rest(xn_p, xn_s, w_in[0], rope_p, rope_s, lng, tm=TM_IN_PROJ)
    mix_p, state_p = _mix_prompt(u_p.reshape(batch, seq, SEC), rest_p.reshape(batch, seq, (N_SEC - 1) * SEC),
                                 ws, sb.T, gn, n_chunks=n_chunks)
    mix_p = mix_p.reshape(tp, W_A + W_B)
    w0 = jnp.repeat(ws[:, 0, 0], CH_A)[None, :]
    b0 = jnp.repeat(sb[:, 0], CH_A)[None, :]
    mix_s, state_s = _mix_sample(u_s, rest_s, w0, b0, gn, state_ret[0])

    mem = mem_prompt.reshape(batch * N_MEM, d)
    mk5, mk = _mem_proj(mem, gm, w_ck[0], batch=batch, name="mem_k")
    mv5, mv = _mem_proj(mem, gm, w_cv[0], batch=batch, name="mem_v")
    h1_p, q_p, h1_s, q_s = _out_q(mix_p, xp, mix_s, xs, w_out[0], w_cq[0], g2, tm=TM_RESIDENT)
    att_s = _attn_sample(q_s, cache_mem_k, cache_mem_v)
    h2_p, xn3_p, h2_s, xn3_s = _attn_co(q_p, mk, mv, h1_p, att_s, h1_s, w_co[0], g3,
                                        tm=TM_RESIDENT, tiles_per_batch=seq // TM_RESIDENT)

    hid_p, hid_s = _ff1(xn3_p, xn3_s, w_ff1[0], tm=TM_MLP, tn=TN_MLP)
    h3_p, part_s = _ff2(hid_p, hid_s, w_ff2[0], h2_p, tm=TM_MLP, tn=TN_MLP, tk=TK_MLP)
    y_p = _final_norm(h3_p, gf, tm=TM_NORM)
    y_s = _add_norm_sample(part_s, h2_s, gf)

    return (y_p.reshape(batch, seq, d),
            y_s,
            mk5,
            mv5,
            state_p[None],
            state_s[None],
            v_rows[None])
```

```python
import functools
import math

import jax
import jax.numpy as jnp
from jax import lax
from jax.experimental import pallas as pl
from jax.experimental.pallas import tpu as pltpu

F32 = jnp.float32
BF16 = jnp.bfloat16

D_MODEL = 2048
H_A = 8
CH_A = 128
W_A = H_A * CH_A
CHUNK = 128
H_B = 8
DK_B = 128
DV_B = 128
W_B = H_B * DV_B
ROPE_THETA = 10000.0
N_MEM = 256
H_X = 4
DH_X = D_MODEL // H_X
EPS = 1e-6
PAST_LEN = 16384

SEC = 1024
N_SEC = 6
LOG_G = tuple(math.log(1.0 - 2.0 ** (-5.0 - h)) for h in range(H_B))

V7X_VMEM_BYTES = 64 * 1024 * 1024
COMPILER_SCRATCH_BYTES = 4 * 1024 * 1024
VMEM_LIMIT = V7X_VMEM_BYTES - COMPILER_SCRATCH_BYTES
ROW_CHUNK = 256

TM_IN_PROJ = 1024
TM_RESIDENT = 512
TM_MLP, TN_MLP, TK_MLP = 2048, 1024, 1024
TM_NORM = 1024


def _params(sem):
    return pltpu.CompilerParams(dimension_semantics=sem, vmem_limit_bytes=VMEM_LIMIT)


def _resident(shape):
    zeros = (0,) * len(shape)
    return pl.BlockSpec(shape, lambda *_: zeros, pipeline_mode=pl.Buffered(1))


def _whole_out(shape):
    zeros = (0,) * len(shape)
    return pl.BlockSpec(shape, lambda *_: zeros)


def _row_loop(n_rows, fn):
    rc = min(n_rows, ROW_CHUNK)
    assert n_rows % rc == 0

    def body(c, carry):
        fn(pl.ds(pl.multiple_of(c * rc, rc), rc))
        return carry

    lax.fori_loop(0, n_rows // rc, body, 0)


def _rms(h, g_ref):
    ms = jnp.mean(h * h, axis=-1, keepdims=True)
    return h * lax.rsqrt(ms + EPS) * g_ref[...]


def _rms_rows(x_ref, g_ref, xn_ref, n_rows):
    def fn(rows):
        xn_ref[rows, :] = _rms(x_ref[rows, :], g_ref).astype(BF16)

    _row_loop(n_rows, fn)


def _cast_rows(w_ref, wb_ref):
    def fn(rows):
        wb_ref[rows, :] = w_ref[rows, :].astype(BF16)

    _row_loop(w_ref.shape[0], fn)


def _inproj_u_kernel(x_ref, g_ref, w_ref, xs_ref, u_ref, xn_ref, us_ref, xsn_ref, wb_ref, *, tm, ts):
    @pl.when(pl.program_id(0) == 0)
    def _():
        _cast_rows(w_ref, wb_ref)
        _rms_rows(xs_ref, g_ref, xsn_ref, ts)
        us_ref[...] = jax.nn.gelu(jnp.dot(xsn_ref[...], wb_ref[...], preferred_element_type=F32))

    _rms_rows(x_ref, g_ref, xn_ref, tm)
    u_ref[...] = jax.nn.gelu(jnp.dot(xn_ref[...], wb_ref[...], preferred_element_type=F32)).astype(BF16)


def _in_proj_u(x, xs, g, w, *, tm):
    t, d = x.shape
    ts = xs.shape[0]
    return pl.pallas_call(
        functools.partial(_inproj_u_kernel, tm=tm, ts=ts),
        grid=(t // tm,),
        in_specs=[pl.BlockSpec((tm, d), lambda i: (i, 0)), _resident((1, d)),
                  pl.BlockSpec((d, SEC), lambda i: (0, 0), pipeline_mode=pl.Buffered(1)),
                  _resident((ts, d))],
        out_specs=[pl.BlockSpec((tm, SEC), lambda i: (i, 0)), pl.BlockSpec((tm, d), lambda i: (i, 0)),
                   _whole_out((ts, SEC)), _whole_out((ts, d))],
        out_shape=[jax.ShapeDtypeStruct((t, SEC), BF16), jax.ShapeDtypeStruct((t, d), BF16),
                   jax.ShapeDtypeStruct((ts, SEC), F32), jax.ShapeDtypeStruct((ts, d), BF16)],
        scratch_shapes=[pltpu.VMEM((d, SEC), BF16)],
        compiler_params=_params(("arbitrary",)),
        name="in_proj_u",
    )(x, g, w, xs)


def _inproj_rest_kernel(xn_ref, w_ref, cc_ref, ss_ref, lng_ref, xsn_ref, ccs_ref, sss_ref,
                        o_ref, os_ref, vrows_ref, wb_ref):
    j = pl.program_id(0)
    i = pl.program_id(1)

    @pl.when(i == 0)
    def _():
        _cast_rows(w_ref, wb_ref)

    def layernorm(acc, cc, ss):
        z = jax.nn.gelu(acc)
        mu = jnp.mean(z, axis=-1, keepdims=True)
        zc = z - mu
        var = jnp.mean(zc * zc, axis=-1, keepdims=True)
        return zc * lax.rsqrt(var + EPS) * lng_ref[...]

    def rotary(acc, cc, ss):
        scale = jnp.where(j == 2, DK_B ** -0.5, 1.0).astype(F32)
        heads = []
        for h in range(H_B):
            blk = acc[:, h * DK_B:(h + 1) * DK_B]
            rot = pltpu.roll(blk, DK_B // 2, 1)
            heads.append((blk * cc + rot * ss) * scale)
        return jnp.concatenate(heads, axis=-1)

    def section(pred, epilogue):
        @pl.when(pred)
        def _():
            acc = jnp.dot(xn_ref[...], wb_ref[...], preferred_element_type=F32)
            o_ref[...] = epilogue(acc, cc_ref[...], ss_ref[...]).astype(o_ref.dtype)

            @pl.when(i == 0)
            def _():
                acc_s = jnp.dot(xsn_ref[...], wb_ref[...], preferred_element_type=F32)
                os_ref[j] = epilogue(acc_s, ccs_ref[...], sss_ref[...])

    section(j == 0, layernorm)
    section((j == 1) | (j == 2), rotary)
    section(j == 3, lambda acc, cc, ss: acc)
    section(j == 4, lambda acc, cc, ss: jax.nn.silu(acc))

    @pl.when((j == 0) & (i == 0))
    def _():
        for h in range(H_A):
            vrows_ref[:, 0, h, :] = os_ref[0, :, h * CH_A:(h + 1) * CH_A]


def _in_proj_rest(xn, xsn, w, rope_p, rope_s, lng, *, tm):
    t, d = xn.shape
    ts = xsn.shape[0]
    n_rest = N_SEC - 1
    rope_blocks = rope_p[0].shape[0] // tm
    rope_spec = pl.BlockSpec((tm, DK_B), lambda j, i: (i % rope_blocks, 0))
    return pl.pallas_call(
        _inproj_rest_kernel,
        grid=(n_rest, t // tm),
        in_specs=[pl.BlockSpec((tm, d), lambda j, i: (i, 0)),
                  pl.BlockSpec((d, SEC), lambda j, i: (0, j + 1)),
                  rope_spec, rope_spec, _resident((1, SEC)),
                  _resident((ts, d)), _resident((ts, DK_B)), _resident((ts, DK_B))],
        out_specs=[pl.BlockSpec((tm, SEC), lambda j, i: (i, j)), _whole_out((n_rest, ts, SEC)),
                   _whole_out((ts, 1, H_A, CH_A))],
        out_shape=[jax.ShapeDtypeStruct((t, n_rest * SEC), BF16),
                   jax.ShapeDtypeStruct((n_rest, ts, SEC), F32),
                   jax.ShapeDtypeStruct((ts, 1, H_A, CH_A), F32)],
        scratch_shapes=[pltpu.VMEM((d, SEC), BF16)],
        compiler_params=_params(("arbitrary", "arbitrary")),
        name="in_proj_rest",
    )(xn, w, rope_p[0], rope_p[1], lng, xsn, rope_s[0], rope_s[1])


def _mem_kernel(x_ref, g_ref, w_ref, o5_ref, o2_ref, wb_ref):
    @pl.when(pl.program_id(0) == 0)
    def _():
        _cast_rows(w_ref, wb_ref)

    res = jnp.dot(_rms(x_ref[...], g_ref).astype(BF16), wb_ref[...], preferred_element_type=F32)
    o2_ref[...] = res.astype(BF16)
    for h in range(H_X):
        o5_ref[0, 0, :, h, :] = res[:, h * DH_X:(h + 1) * DH_X]


def _mem_proj(mem, g, w, *, batch, name):
    d = mem.shape[1]
    return pl.pallas_call(
        _mem_kernel,
        grid=(batch,),
        in_specs=[pl.BlockSpec((N_MEM, d), lambda b: (b, 0)), _resident((1, d)), _resident((d, d))],
        out_specs=[pl.BlockSpec((1, 1, N_MEM, H_X, DH_X), lambda b: (0, b, 0, 0, 0)),
                   pl.BlockSpec((N_MEM, d), lambda b: (b, 0))],
        out_shape=[jax.ShapeDtypeStruct((1, batch, N_MEM, H_X, DH_X), F32),
                   jax.ShapeDtypeStruct((batch * N_MEM, d), BF16)],
        scratch_shapes=[pltpu.VMEM((d, d), BF16)],
        compiler_params=_params(("arbitrary",)),
        name=name,
    )(mem, g, w)


def _relu2(acc):
    return jnp.square(jnp.maximum(acc, 0.0))


def _ff1_kernel(x_ref, w_ref, xs_ref, o_ref, os_ref):
    o_ref[...] = _relu2(jnp.dot(x_ref[...], w_ref[...].astype(BF16), preferred_element_type=F32)).astype(BF16)

    @pl.when(pl.program_id(0) == 0)
    def _():
        os_ref[pl.program_id(1)] = _relu2(jnp.dot(xs_ref[...], w_ref[...].astype(BF16),
                                                  preferred_element_type=F32)).astype(BF16)


def _ff1(x, xs, w, *, tm, tn):
    t, k = x.shape
    ts = xs.shape[0]
    n = w.shape[1]
    return pl.pallas_call(
        _ff1_kernel,
        grid=(t // tm, n // tn),
        in_specs=[pl.BlockSpec((tm, k), lambda i, j: (i, 0)),
                  pl.BlockSpec((k, tn), lambda i, j: (0, j)),
                  _resident((ts, k))],
        out_specs=[pl.BlockSpec((tm, tn), lambda i, j: (i, j)), _whole_out((n // tn, ts, tn))],
        out_shape=[jax.ShapeDtypeStruct((t, n), BF16), jax.ShapeDtypeStruct((n // tn, ts, tn), BF16)],
        compiler_params=_params(("arbitrary", "arbitrary")),
        name="ff1",
    )(x, w, xs)


def _ff2_kernel(x_ref, w_ref, r_ref, xs_ref, o_ref, os_ref):
    i, j, k = pl.program_id(0), pl.program_id(1), pl.program_id(2)

    @pl.when(k == 0)
    def _():
        o_ref[...] = r_ref[...]

    o_ref[...] += jnp.dot(x_ref[...], w_ref[...].astype(BF16), preferred_element_type=F32)

    @pl.when(i == 0)
    def _():
        part = jnp.dot(xs_ref[k], w_ref[...].astype(BF16), preferred_element_type=F32)

        @pl.when(k == 0)
        def _():
            os_ref[j] = part

        @pl.when(k > 0)
        def _():
            os_ref[j] += part


def _ff2(x, xs, w, r, *, tm, tn, tk):
    t, k = x.shape
    ts = xs.shape[1]
    n = w.shape[1]
    assert xs.shape == (k // tk, ts, tk), "sample tiles must be stacked in contraction slices of tk"
    return pl.pallas_call(
        _ff2_kernel,
        grid=(t // tm, n // tn, k // tk),
        in_specs=[pl.BlockSpec((tm, tk), lambda i, j, kk: (i, kk)),
                  pl.BlockSpec((tk, tn), lambda i, j, kk: (kk, j)),
                  pl.BlockSpec((tm, tn), lambda i, j, kk: (i, j)),
                  _resident((k // tk, ts, tk))],
        out_specs=[pl.BlockSpec((tm, tn), lambda i, j, kk: (i, j)), _whole_out((n // tn, ts, tn))],
        out_shape=[jax.ShapeDtypeStruct((t, n), F32), jax.ShapeDtypeStruct((n // tn, ts, tn), F32)],
        compiler_params=_params(("arbitrary", "arbitrary", "arbitrary")),
        name="ff2",
    )(x, w, r, xs)


def _final_norm_kernel(x_ref, g_ref, o_ref, *, tm):
    def fn(rows):
        o_ref[rows, :] = _rms(x_ref[rows, :], g_ref)

    _row_loop(tm, fn)


def _final_norm(x, g, *, tm):
    t, d = x.shape
    row = pl.BlockSpec((tm, d), lambda i: (i, 0))
    return pl.pallas_call(
        functools.partial(_final_norm_kernel, tm=tm),
        grid=(t // tm,),
        in_specs=[row, _resident((1, d))],
        out_specs=row,
        out_shape=jax.ShapeDtypeStruct((t, d), F32),
        compiler_params=_params(("arbitrary",)),
        name="final_norm",
    )(x, g)


def _add_norm_sample_kernel(x_ref, r_ref, g_ref, o_ref):
    y = r_ref[...] + jnp.concatenate([x_ref[j] for j in range(x_ref.shape[0])], axis=-1)
    o_ref[:, 0, :] = _rms(y, g_ref)


def _add_norm_sample(x, r, g):
    ts, d = r.shape
    return pl.pallas_call(
        _add_norm_sample_kernel,
        grid=(1,),
        in_specs=[_resident(x.shape), _resident((ts, d)), _resident((1, d))],
        out_specs=_whole_out((ts, 1, d)),
        out_shape=jax.ShapeDtypeStruct((ts, 1, d), F32),
        compiler_params=_params(("arbitrary",)),
        name="final_norm_sample",
    )(x, r, g)


W_CHUNK = 256
N_LOAD = D_MODEL // W_CHUNK


def _load_weight_chunk(s, pairs):
    rows = pl.ds(pl.multiple_of(s * W_CHUNK, W_CHUNK), W_CHUNK)
    for src, dst in pairs:
        dst[rows, :] = src[...].astype(BF16)


def _w_chunk_spec():
    return pl.BlockSpec((W_CHUNK, D_MODEL), lambda s: (jnp.minimum(s, N_LOAD - 1), 0))


def _tile_spec(tm, width=D_MODEL):
    return pl.BlockSpec((tm, width), lambda s: (jnp.maximum(s - N_LOAD, 0), 0))


def _outq_kernel(mix_ref, x_ref, wo_ref, wq_ref, g2_ref, mixs_ref, xs_ref,
                 h1_ref, q_ref, h1s_ref, qs_ref, wo_bf, wq_bf):
    s = pl.program_id(0)

    def block(mix_r, x_r, h1_r, q_r):
        h1 = x_r[...] + jnp.dot(mix_r[...].astype(BF16), wo_bf[...], preferred_element_type=F32)
        h1_r[...] = h1
        q_r[...] = jnp.dot(_rms(h1, g2_ref).astype(BF16), wq_bf[...],
                           preferred_element_type=F32).astype(q_r.dtype)

    @pl.when(s < N_LOAD)
    def _():
        _load_weight_chunk(s, ((wo_ref, wo_bf), (wq_ref, wq_bf)))

    @pl.when(s >= N_LOAD)
    def _():
        block(mix_ref, x_ref, h1_ref, q_ref)

    @pl.when(s == N_LOAD)
    def _():
        block(mixs_ref, xs_ref, h1s_ref, qs_ref)


def _out_q(mix, x, mix_s, xs, w_out, w_cq, g2, *, tm):
    t, d = x.shape
    ts = xs.shape[0]
    return pl.pallas_call(
        _outq_kernel,
        grid=(N_LOAD + t // tm,),
        in_specs=[_tile_spec(tm), _tile_spec(tm), _w_chunk_spec(), _w_chunk_spec(), _resident((1, d)),
                  _resident((ts, d)), _resident((ts, d))],
        out_specs=[_tile_spec(tm), _tile_spec(tm), _whole_out((ts, d)), _whole_out((ts, d))],
        out_shape=[jax.ShapeDtypeStruct((t, d), F32), jax.ShapeDtypeStruct((t, d), BF16),
                   jax.ShapeDtypeStruct((ts, d), F32), jax.ShapeDtypeStruct((ts, d), F32)],
        scratch_shapes=[pltpu.VMEM((d, d), BF16), pltpu.VMEM((d, d), BF16)],
        compiler_params=_params(("arbitrary",)),
        name="out_q",
    )(mix, x, w_out, w_cq, g2, mix_s, xs)


def _softmax_rows(s):
    m = jnp.max(s, axis=-1, keepdims=True)
    e = jnp.exp(s - m)
    return e / jnp.sum(e, axis=-1, keepdims=True)


def _attn_co_kernel(q_ref, mk_ref, mv_ref, h1_ref, wc_ref, g3_ref, atts_ref, h1s_ref,
                    h2_ref, xn_ref, h2s_ref, xns_ref, wc_bf, att_ref):
    s = pl.program_id(0)

    def co_block(att_r, h1_r, h2_r, xn_r):
        h2 = h1_r[...] + jnp.dot(att_r[...].astype(BF16), wc_bf[...], preferred_element_type=F32)
        h2_r[...] = h2
        xn_r[...] = _rms(h2, g3_ref).astype(BF16)

    @pl.when(s < N_LOAD)
    def _():
        _load_weight_chunk(s, ((wc_ref, wc_bf),))

    @pl.when(s >= N_LOAD)
    def _():
        for h in range(H_X):
            cols = slice(h * DH_X, (h + 1) * DH_X)
            sc = lax.dot_general(q_ref[:, cols], mk_ref[:, cols], (((1,), (1,)), ((), ())),
                                 preferred_element_type=F32) * (DH_X ** -0.5)
            p = _softmax_rows(sc)
            att_ref[:, cols] = jnp.dot(p.astype(BF16), mv_ref[:, cols],
                                       preferred_element_type=F32).astype(BF16)
        co_block(att_ref, h1_ref, h2_ref, xn_ref)

    @pl.when(s == N_LOAD)
    def _():
        co_block(atts_ref, h1s_ref, h2s_ref, xns_ref)


def _attn_co(q, mk, mv, h1, att_s, h1_s, w_co, g3, *, tm, tiles_per_batch):
    t, d = h1.shape
    ts = h1_s.shape[0]
    mem_spec = pl.BlockSpec((N_MEM, d), lambda s: (jnp.maximum(s - N_LOAD, 0) // tiles_per_batch, 0))
    return pl.pallas_call(
        _attn_co_kernel,
        grid=(N_LOAD + t // tm,),
        in_specs=[_tile_spec(tm), mem_spec, mem_spec, _tile_spec(tm), _w_chunk_spec(), _resident((1, d)),
                  _resident((ts, d)), _resident((ts, d))],
        out_specs=[_tile_spec(tm), _tile_spec(tm), _whole_out((ts, d)), _whole_out((ts, d))],
        out_shape=[jax.ShapeDtypeStruct((t, d), F32), jax.ShapeDtypeStruct((t, d), BF16),
                   jax.ShapeDtypeStruct((ts, d), F32), jax.ShapeDtypeStruct((ts, d), BF16)],
        scratch_shapes=[pltpu.VMEM((d, d), BF16), pltpu.VMEM((tm, d), BF16)],
        compiler_params=_params(("arbitrary",)),
        name="attn_co",
    )(q, mk, mv, h1, w_co, g3, att_s, h1_s)


def _mix_prompt_kernel(u_ref, v_ref, q_ref, k_ref, vb_ref, g_ref, ws_ref, bt_ref, gn_ref,
                       us_ref, vs_ref, qs_ref, ks_ref, vbs_ref, gs_ref, w0_ref, b0_ref, ss_ref,
                       o_ref, so_ref, os_ref, sso_ref,
                       st_ref, wt_ref, bias_ref, dm_ref, qd_ref, kd_ref, *, batch, n_chunks):
    n = pl.program_id(0)
    _mix_sample_rows(us_ref, vs_ref, qs_ref, ks_ref, vbs_ref, gs_ref, w0_ref, b0_ref, gn_ref, ss_ref,
                     os_ref, sso_ref)

    @pl.when(n == 0)
    def _():
        ii = lax.broadcasted_iota(jnp.int32, (CHUNK, CHUNK), 0)
        jj = lax.broadcasted_iota(jnp.int32, (CHUNK, CHUNK), 1)
        causal = ii >= jj
        diff = jnp.maximum((ii - jj).astype(F32), 0.0)
        ridx = ii.astype(F32)
        for h in range(H_A):
            lg = LOG_G[h]
            wt_ref[h] = (ws_ref[h] * causal.astype(F32)).astype(BF16)
            bias_ref[h] = jnp.broadcast_to(bt_ref[:, h:h + 1], (CHUNK, CH_A))
            dm_ref[h] = jnp.where(causal, jnp.exp(lg * diff), 0.0)
            qd_ref[h] = jnp.exp(lg * (ridx + 1.0))
            kd_ref[h] = jnp.exp(lg * (CHUNK - 1.0 - ridx))
        st_ref[...] = jnp.zeros_like(st_ref)

    for b in range(batch):
        for h in range(H_A):
            cols = slice(h * CH_A, (h + 1) * CH_A)
            mixed = jnp.dot(wt_ref[h], v_ref[b, :, cols], preferred_element_type=F32) + bias_ref[h]
            o_ref[b, :, cols] = (u_ref[b, :, cols].astype(F32) * mixed).astype(BF16)

            kh = k_ref[b, :, cols]
            vh = vb_ref[b, :, cols]
            state = st_ref[b, h]
            state_bf = state.astype(BF16)
            qh = q_ref[b, :, cols]
            scores = lax.dot_general(qh, kh, (((1,), (1,)), ((), ())), preferred_element_type=F32) * dm_ref[h]
            intra = jnp.dot(scores.astype(BF16), vh, preferred_element_type=F32)
            cross = jnp.dot(qh, state_bf, preferred_element_type=F32) * qd_ref[h]
            ret = intra + cross
            mu = jnp.mean(ret, axis=-1, keepdims=True)
            rc = ret - mu
            var = jnp.mean(rc * rc, axis=-1, keepdims=True)
            normed = rc * lax.rsqrt(var + EPS) * gn_ref[:, cols]
            o_ref[b, :, W_A + h * DV_B:W_A + (h + 1) * DV_B] = (
                g_ref[b, :, cols].astype(F32) * normed).astype(BF16)
            kd = (kh.astype(F32) * kd_ref[h]).astype(BF16)
            st_ref[b, h] = state * math.exp(LOG_G[h] * CHUNK) + lax.dot_general(
                kd, vh, (((0,), (0,)), ((), ())), preferred_element_type=F32)

    @pl.when(n == n_chunks - 1)
    def _():
        so_ref[...] = st_ref[...]


def _mixers(u, rest, u_s, rest_s, ws, bt, gn, w0, b0, state_s, *, n_chunks):
    batch, seq, _ = u.shape
    ts = u_s.shape[0]
    rb = ts // n_chunks
    assert rb * n_chunks == ts and rb % 8 == 0

    def sec(s):
        return pl.BlockSpec((batch, CHUNK, SEC), lambda n, s=s: (0, n, s))

    def sec_s(s):
        return pl.BlockSpec((None, rb, SEC), lambda n, s=s: (s, n, 0))

    state_spec = pl.BlockSpec((rb, H_B, DK_B, DV_B), lambda n: (n, 0, 0, 0))
    table = pltpu.VMEM((H_B, CHUNK, CHUNK), F32)
    return pl.pallas_call(
        functools.partial(_mix_prompt_kernel, batch=batch, n_chunks=n_chunks),
        grid=(n_chunks,),
        in_specs=[sec(0), sec(0), sec(1), sec(2), sec(3), sec(4),
                  _resident((H_A, CHUNK, CHUNK)), _resident((CHUNK, H_A)), _resident((1, W_B)),
                  pl.BlockSpec((rb, SEC), lambda n: (n, 0)), sec_s(0), sec_s(1), sec_s(2), sec_s(3), sec_s(4),
                  _resident((1, W_A)), _resident((1, W_A)), state_spec],
        out_specs=[pl.BlockSpec((batch, CHUNK, W_A + W_B), lambda n: (0, n, 0)),
                   _whole_out((batch, H_B, DK_B, DV_B)),
                   pl.BlockSpec((rb, W_A + W_B), lambda n: (n, 0)), state_spec],
        out_shape=[jax.ShapeDtypeStruct((batch, seq, W_A + W_B), BF16),
                   jax.ShapeDtypeStruct((batch, H_B, DK_B, DV_B), F32),
                   jax.ShapeDtypeStruct((ts, W_A + W_B), F32),
                   jax.ShapeDtypeStruct(state_s.shape, F32)],
        scratch_shapes=[pltpu.VMEM((batch, H_B, DK_B, DV_B), F32), pltpu.VMEM((H_A, CHUNK, CHUNK), BF16),
                        table, table, table, table],
        compiler_params=_params(("arbitrary",)),
        name="mixers",
    )(u, rest, rest, rest, rest, rest, ws, bt, gn,
      u_s, rest_s, rest_s, rest_s, rest_s, rest_s, w0, b0, state_s)


def _mix_sample_rows(u_ref, v_ref, q_ref, k_ref, vb_ref, g_ref, w0_ref, b0_ref, gn_ref, s_ref,
                     o_ref, so_ref):
    n_rows = u_ref.shape[0]
    rid = lax.broadcasted_iota(jnp.int32, (n_rows, 1), 0)
    o_ref[:, :W_A] = u_ref[...] * (w0_ref[...] * v_ref[...] + b0_ref[...])

    for h in range(H_B):
        cols = slice(h * DK_B, (h + 1) * DK_B)
        g_h = math.exp(LOG_G[h])
        q_blk = q_ref[:, cols]
        k_blk = k_ref[:, cols]
        v_blk = vb_ref[:, cols]
        v_bf = v_blk.astype(BF16)
        intra = jnp.sum(q_blk * k_blk, axis=-1, keepdims=True) * v_blk
        cross = jnp.zeros((n_rows, DV_B), F32)
        for r in range(n_rows):
            state = s_ref[r, h]
            q_only_r = jnp.where(rid == r, q_blk, 0.0).astype(BF16)
            cross = cross + jnp.dot(q_only_r, state.astype(BF16), preferred_element_type=F32)
            k_only_r = jnp.where(rid == r, k_blk, 0.0).astype(BF16)
            outer = lax.dot_general(k_only_r, v_bf, (((0,), (0,)), ((), ())), preferred_element_type=F32)
            so_ref[r, h] = state * g_h + outer
        ret = intra + cross * g_h
        mu = jnp.mean(ret, axis=-1, keepdims=True)
        rc = ret - mu
        var = jnp.mean(rc * rc, axis=-1, keepdims=True)
        normed = rc * lax.rsqrt(var + EPS) * gn_ref[:, cols]
        o_ref[:, W_A + h * DV_B:W_A + (h + 1) * DV_B] = g_ref[:, cols] * normed


ATTN_SAMPLE_ROWS = 4
SUBLANES = 8


def _attn_sample_kernel(q_ref, ck_ref, cv_ref, o_ref):
    i = pl.program_id(0)
    pairs = N_MEM * H_X
    fold = lambda t: pltpu.roll(t, H_X, 1)
    for r in range(ATTN_SAMPLE_ROWS):
        row = pl.ds(i * ATTN_SAMPLE_ROWS + r, 1)
        q = q_ref[row, :]
        q8 = jnp.concatenate([q[:, (s % H_X) * DH_X:(s % H_X + 1) * DH_X] for s in range(SUBLANES)], axis=0)
        k3 = ck_ref[0, r].reshape(pairs // SUBLANES, SUBLANES, DH_X)
        v3 = cv_ref[0, r].reshape(pairs // SUBLANES, SUBLANES, DH_X)
        s = jnp.sum(k3 * q8[None], axis=-1, keepdims=True) * (DH_X ** -0.5)
        m = jnp.max(s, axis=0, keepdims=True)
        m = jnp.maximum(m, fold(m))
        e = jnp.exp(s - m)
        l = jnp.sum(e, axis=0, keepdims=True)
        acc = jnp.sum(e * v3, axis=0, keepdims=True)
        out = ((acc + fold(acc)) / (l + fold(l)))[0]
        for h in range(H_X):
            o_ref[row, h * DH_X:(h + 1) * DH_X] = out[h:h + 1, :]


def _attn_sample(q, ck, cv):
    t, d = q.shape
    rb = ATTN_SAMPLE_ROWS
    cache_spec = pl.BlockSpec((1, rb, N_MEM, H_X, DH_X), lambda i: (0, i, 0, 0, 0))
    return pl.pallas_call(
        _attn_sample_kernel,
        grid=(t // rb,),
        in_specs=[_resident((t, d)), cache_spec, cache_spec],
        out_specs=_whole_out((t, d)),
        out_shape=jax.ShapeDtypeStruct((t, d), F32),
        compiler_params=_params(("arbitrary",)),
        name="attn_sample",
    )(q, ck, cv)


def _rope_tables(pos):
    half = DK_B // 2
    freqs = ROPE_THETA ** (-jnp.arange(half, dtype=F32) / half)
    ang = pos[:, None] * freqs[None, :]
    cos = jnp.cos(ang)
    sin = jnp.sin(ang)
    return jnp.concatenate([cos, cos], axis=-1), jnp.concatenate([-sin, sin], axis=-1)


def kernel(x_prompt, x_sample, mem_prompt, cache_mem_k, cache_mem_v, state_ret, norm1_g, w_in, sgu_norm_g, sgu_w_s, sgu_b, ret_gn_g, w_out, norm2_g, mem_norm_g, w_cq, w_ck, w_cv, w_co, norm3_g, w_ff1, w_ff2, final_norm_g):
    batch, seq, d = x_prompt.shape
    ts = x_sample.shape[0]
    n_chunks = seq // CHUNK
    tp = batch * seq

    g1 = norm1_g[0][None, :]
    g2 = norm2_g[0][None, :]
    g3 = norm3_g[0][None, :]
    gm = mem_norm_g[0][None, :]
    gf = final_norm_g[None, :]
    lng = sgu_norm_g[0][None, :]
    gn = ret_gn_g[0][None, :]
    ws = sgu_w_s[0]
    sb = sgu_b[0]
    xp = x_prompt.reshape(tp, d)
    xs = x_sample.reshape(ts, d)

    rope_p = _rope_tables(jnp.arange(seq, dtype=F32))
    rope_s = _rope_tables(jnp.full((ts,), PAST_LEN, dtype=F32))
    u_p, xn_p, u_s, xn_s = _in_proj_u(xp, xs, g1, w_in[0], tm=TM_IN_PROJ)
    rest_p, rest_s, v_rows = _in_proj_rest(xn_p, xn_s, w_in[0], rope_p, rope_s, lng, tm=TM_IN_PROJ)
    w0 = jnp.repeat(ws[:, 0, 0], CH_A)[None, :]
    b0 = jnp.repeat(sb[:, 0], CH_A)[None, :]
    mix_p, state_p, mix_s, state_s = _mixers(
        u_p.reshape(batch, seq, SEC), rest_p.reshape(batch, seq, (N_SEC - 1) * SEC), u_s, rest_s,
        ws, sb.T, gn, w0, b0, state_ret[0], n_chunks=n_chunks)
    mix_p = mix_p.reshape(tp, W_A + W_B)

    mem = mem_prompt.reshape(batch * N_MEM, d)
    mk5, mk = _mem_proj(mem, gm, w_ck[0], batch=batch, name="mem_k")
    mv5, mv = _mem_proj(mem, gm, w_cv[0], batch=batch, name="mem_v")
    h1_p, q_p, h1_s, q_s = _out_q(mix_p, xp, mix_s, xs, w_out[0], w_cq[0], g2, tm=TM_RESIDENT)
    att_s = _attn_sample(q_s, cache_mem_k, cache_mem_v)
    h2_p, xn3_p, h2_s, xn3_s = _attn_co(q_p, mk, mv, h1_p, att_s, h1_s, w_co[0], g3,
                                        tm=TM_RESIDENT, tiles_per_batch=seq // TM_RESIDENT)

    hid_p, hid_s = _ff1(xn3_p, xn3_s, w_ff1[0], tm=TM_MLP, tn=TN_MLP)
    h3_p, part_s = _ff2(hid_p, hid_s, w_ff2[0], h2_p, tm=TM_MLP, tn=TN_MLP, tk=TK_MLP)
    y_p = _final_norm(h3_p, gf, tm=TM_NORM)
    y_s = _add_norm_sample(part_s, h2_s, gf)

    return (y_p.reshape(batch, seq, d),
            y_s,
            mk5,
            mv5,
            state_p[None],
            state_s[None],
            v_rows[None])
```

```python
import functools
import math

import jax
import jax.numpy as jnp
import numpy as np
from jax import lax
from jax.experimental import pallas as pl
from jax.experimental.pallas import tpu as pltpu

F32 = jnp.float32
BF16 = jnp.bfloat16

D_MODEL = 2048
H_A = 8
CH_A = 128
W_A = H_A * CH_A
CHUNK = 128
H_B = 8
DK_B = 128
DV_B = 128
W_B = H_B * DV_B
ROPE_THETA = 10000.0
N_MEM = 256
H_X = 4
DH_X = D_MODEL // H_X
EPS = 1e-6
PAST_LEN = 16384

SEC = 1024
N_SEC = 6
LOG_G = tuple(math.log(1.0 - 2.0 ** (-5.0 - h)) for h in range(H_B))

V7X_VMEM_BYTES = 64 * 1024 * 1024
COMPILER_SCRATCH_BYTES = 4 * 1024 * 1024
VMEM_LIMIT = V7X_VMEM_BYTES - COMPILER_SCRATCH_BYTES
ROW_CHUNK = 256

TM_IN_PROJ = 1024
TM_RESIDENT = 512
TM_MLP, TN_MLP, TK_MLP = 2048, 1024, 1024
TM_NORM = 1024


def _params(sem):
    return pltpu.CompilerParams(dimension_semantics=sem, vmem_limit_bytes=VMEM_LIMIT)


def _resident(shape):
    zeros = (0,) * len(shape)
    return pl.BlockSpec(shape, lambda *_: zeros, pipeline_mode=pl.Buffered(1))


def _whole_out(shape):
    zeros = (0,) * len(shape)
    return pl.BlockSpec(shape, lambda *_: zeros)


def _row_loop(n_rows, fn):
    rc = min(n_rows, ROW_CHUNK)
    assert n_rows % rc == 0

    def body(c, carry):
        fn(pl.ds(pl.multiple_of(c * rc, rc), rc))
        return carry

    lax.fori_loop(0, n_rows // rc, body, 0)


def _rms(h, g_ref):
    ms = jnp.mean(h * h, axis=-1, keepdims=True)
    return h * lax.rsqrt(ms + EPS) * g_ref[...]


def _rms_rows(x_ref, g_ref, xn_ref, n_rows):
    def fn(rows):
        xn_ref[rows, :] = _rms(x_ref[rows, :], g_ref).astype(BF16)

    _row_loop(n_rows, fn)


def _cast_rows(w_ref, wb_ref):
    def fn(rows):
        wb_ref[rows, :] = w_ref[rows, :].astype(BF16)

    _row_loop(w_ref.shape[0], fn)


def _inproj_u_kernel(x_ref, g_ref, w_ref, xs_ref, u_ref, xn_ref, us_ref, xsn_ref, wb_ref, *, tm, ts):
    @pl.when(pl.program_id(0) == 0)
    def _():
        _cast_rows(w_ref, wb_ref)
        _rms_rows(xs_ref, g_ref, xsn_ref, ts)
        us_ref[...] = jax.nn.gelu(jnp.dot(xsn_ref[...], wb_ref[...], preferred_element_type=F32))

    _rms_rows(x_ref, g_ref, xn_ref, tm)
    u_ref[...] = jax.nn.gelu(jnp.dot(xn_ref[...], wb_ref[...], preferred_element_type=F32)).astype(BF16)


def _in_proj_u(x, xs, g, w, *, tm):
    t, d = x.shape
    ts = xs.shape[0]
    return pl.pallas_call(
        functools.partial(_inproj_u_kernel, tm=tm, ts=ts),
        grid=(t // tm,),
        in_specs=[pl.BlockSpec((tm, d), lambda i: (i, 0)), _resident((1, d)),
                  pl.BlockSpec((d, SEC), lambda i: (0, 0), pipeline_mode=pl.Buffered(1)),
                  _resident((ts, d))],
        out_specs=[pl.BlockSpec((tm, SEC), lambda i: (i, 0)), pl.BlockSpec((tm, d), lambda i: (i, 0)),
                   _whole_out((ts, SEC)), _whole_out((ts, d))],
        out_shape=[jax.ShapeDtypeStruct((t, SEC), BF16), jax.ShapeDtypeStruct((t, d), BF16),
                   jax.ShapeDtypeStruct((ts, SEC), F32), jax.ShapeDtypeStruct((ts, d), BF16)],
        scratch_shapes=[pltpu.VMEM((d, SEC), BF16)],
        compiler_params=_params(("arbitrary",)),
        name="in_proj_u",
    )(x, g, w, xs)


def _inproj_rest_kernel(xn_ref, w_ref, cc_ref, ss_ref, lng_ref, xsn_ref, ccs_ref, sss_ref,
                        o_ref, os_ref, vrows_ref, wb_ref):
    j = pl.program_id(0)
    i = pl.program_id(1)

    @pl.when(i == 0)
    def _():
        _cast_rows(w_ref, wb_ref)

    def layernorm(acc, cc, ss):
        z = jax.nn.gelu(acc)
        mu = jnp.mean(z, axis=-1, keepdims=True)
        zc = z - mu
        var = jnp.mean(zc * zc, axis=-1, keepdims=True)
        return zc * lax.rsqrt(var + EPS) * lng_ref[...]

    def rotary(acc, cc, ss):
        scale = jnp.where(j == 2, DK_B ** -0.5, 1.0).astype(F32)
        heads = []
        for h in range(H_B):
            blk = acc[:, h * DK_B:(h + 1) * DK_B]
            rot = pltpu.roll(blk, DK_B // 2, 1)
            heads.append((blk * cc + rot * ss) * scale)
        return jnp.concatenate(heads, axis=-1)

    def section(pred, epilogue):
        @pl.when(pred)
        def _():
            acc = jnp.dot(xn_ref[...], wb_ref[...], preferred_element_type=F32)
            o_ref[...] = epilogue(acc, cc_ref[...], ss_ref[...]).astype(o_ref.dtype)

            @pl.when(i == 0)
            def _():
                acc_s = jnp.dot(xsn_ref[...], wb_ref[...], preferred_element_type=F32)
                os_ref[j] = epilogue(acc_s, ccs_ref[...], sss_ref[...])

    section(j == 0, layernorm)
    section((j == 1) | (j == 2), rotary)
    section(j == 3, lambda acc, cc, ss: acc)
    section(j == 4, lambda acc, cc, ss: jax.nn.silu(acc))

    @pl.when((j == 0) & (i == 0))
    def _():
        for h in range(H_A):
            vrows_ref[:, 0, h, :] = os_ref[0, :, h * CH_A:(h + 1) * CH_A]


def _in_proj_rest(xn, xsn, w, rope_p, rope_s, lng, *, tm):
    t, d = xn.shape
    ts = xsn.shape[0]
    n_rest = N_SEC - 1
    rope_blocks = rope_p[0].shape[0] // tm
    rope_spec = pl.BlockSpec((tm, DK_B), lambda j, i: (i % rope_blocks, 0))
    return pl.pallas_call(
        _inproj_rest_kernel,
        grid=(n_rest, t // tm),
        in_specs=[pl.BlockSpec((tm, d), lambda j, i: (i, 0)),
                  pl.BlockSpec((d, SEC), lambda j, i: (0, j + 1)),
                  rope_spec, rope_spec, _resident((1, SEC)),
                  _resident((ts, d)), _resident((ts, DK_B)), _resident((ts, DK_B))],
        out_specs=[pl.BlockSpec((tm, SEC), lambda j, i: (i, j)), _whole_out((n_rest, ts, SEC)),
                   _whole_out((ts, 1, H_A, CH_A))],
        out_shape=[jax.ShapeDtypeStruct((t, n_rest * SEC), BF16),
                   jax.ShapeDtypeStruct((n_rest, ts, SEC), F32),
                   jax.ShapeDtypeStruct((ts, 1, H_A, CH_A), F32)],
        scratch_shapes=[pltpu.VMEM((d, SEC), BF16)],
        compiler_params=_params(("arbitrary", "arbitrary")),
        name="in_proj_rest",
    )(xn, w, rope_p[0], rope_p[1], lng, xsn, rope_s[0], rope_s[1])


def _relu2(acc):
    return jnp.square(jnp.maximum(acc, 0.0))


def _ff1_kernel(x_ref, w_ref, xs_ref, o_ref, os_ref):
    o_ref[...] = _relu2(jnp.dot(x_ref[...], w_ref[...].astype(BF16), preferred_element_type=F32)).astype(BF16)

    @pl.when(pl.program_id(0) == 0)
    def _():
        os_ref[pl.program_id(1)] = _relu2(jnp.dot(xs_ref[...], w_ref[...].astype(BF16),
                                                  preferred_element_type=F32)).astype(BF16)


def _ff1(x, xs, w, *, tm, tn):
    t, k = x.shape
    ts = xs.shape[0]
    n = w.shape[1]
    return pl.pallas_call(
        _ff1_kernel,
        grid=(t // tm, n // tn),
        in_specs=[pl.BlockSpec((tm, k), lambda i, j: (i, 0)),
                  pl.BlockSpec((k, tn), lambda i, j: (0, j)),
                  _resident((ts, k))],
        out_specs=[pl.BlockSpec((tm, tn), lambda i, j: (i, j)), _whole_out((n // tn, ts, tn))],
        out_shape=[jax.ShapeDtypeStruct((t, n), BF16), jax.ShapeDtypeStruct((n // tn, ts, tn), BF16)],
        compiler_params=_params(("arbitrary", "arbitrary")),
        name="ff1",
    )(x, w, xs)


def _ff2_kernel(x_ref, w_ref, r_ref, xs_ref, o_ref, os_ref):
    i, j, k = pl.program_id(0), pl.program_id(1), pl.program_id(2)

    @pl.when(k == 0)
    def _():
        o_ref[...] = r_ref[...]

    o_ref[...] += jnp.dot(x_ref[...], w_ref[...].astype(BF16), preferred_element_type=F32)

    @pl.when(i == 0)
    def _():
        part = jnp.dot(xs_ref[k], w_ref[...].astype(BF16), preferred_element_type=F32)

        @pl.when(k == 0)
        def _():
            os_ref[j] = part

        @pl.when(k > 0)
        def _():
            os_ref[j] += part


def _ff2(x, xs, w, r, *, tm, tn, tk):
    t, k = x.shape
    ts = xs.shape[1]
    n = w.shape[1]
    assert xs.shape == (k // tk, ts, tk), "sample tiles must be stacked in contraction slices of tk"
    return pl.pallas_call(
        _ff2_kernel,
        grid=(t // tm, n // tn, k // tk),
        in_specs=[pl.BlockSpec((tm, tk), lambda i, j, kk: (i, kk)),
                  pl.BlockSpec((tk, tn), lambda i, j, kk: (kk, j)),
                  pl.BlockSpec((tm, tn), lambda i, j, kk: (i, j)),
                  _resident((k // tk, ts, tk))],
        out_specs=[pl.BlockSpec((tm, tn), lambda i, j, kk: (i, j)), _whole_out((n // tn, ts, tn))],
        out_shape=[jax.ShapeDtypeStruct((t, n), F32), jax.ShapeDtypeStruct((n // tn, ts, tn), F32)],
        compiler_params=_params(("arbitrary", "arbitrary", "arbitrary")),
        name="ff2",
    )(x, w, r, xs)


def _final_norm_kernel(x_ref, g_ref, o_ref, *, tm):
    def fn(rows):
        o_ref[rows, :] = _rms(x_ref[rows, :], g_ref)

    _row_loop(tm, fn)


def _final_norm(x, g, *, tm):
    t, d = x.shape
    row = pl.BlockSpec((tm, d), lambda i: (i, 0))
    return pl.pallas_call(
        functools.partial(_final_norm_kernel, tm=tm),
        grid=(t // tm,),
        in_specs=[row, _resident((1, d))],
        out_specs=row,
        out_shape=jax.ShapeDtypeStruct((t, d), F32),
        compiler_params=_params(("arbitrary",)),
        name="final_norm",
    )(x, g)


def _add_norm_sample_kernel(x_ref, r_ref, g_ref, o_ref):
    y = r_ref[...] + jnp.concatenate([x_ref[j] for j in range(x_ref.shape[0])], axis=-1)
    o_ref[:, 0, :] = _rms(y, g_ref)


def _add_norm_sample(x, r, g):
    ts, d = r.shape
    return pl.pallas_call(
        _add_norm_sample_kernel,
        grid=(1,),
        in_specs=[_resident(x.shape), _resident((ts, d)), _resident((1, d))],
        out_specs=_whole_out((ts, 1, d)),
        out_shape=jax.ShapeDtypeStruct((ts, 1, d), F32),
        compiler_params=_params(("arbitrary",)),
        name="final_norm_sample",
    )(x, r, g)


W_CHUNK = 256
N_LOAD = D_MODEL // W_CHUNK


def _load_weight_chunk(s, pairs):
    rows = pl.ds(pl.multiple_of(s * W_CHUNK, W_CHUNK), W_CHUNK)
    for src, dst in pairs:
        dst[rows, :] = src[...].astype(BF16)


def _w_chunk_spec():
    return pl.BlockSpec((W_CHUNK, D_MODEL), lambda s: (jnp.minimum(s, N_LOAD - 1), 0))


def _tile_spec(tm, width=D_MODEL):
    return pl.BlockSpec((tm, width), lambda s: (jnp.maximum(s - N_LOAD, 0), 0))


def _outq_kernel(mix_ref, x_ref, wo_ref, wq_ref, g2_ref, mixs_ref, xs_ref,
                 h1_ref, q_ref, h1s_ref, qs_ref, wo_bf, wq_bf):
    s = pl.program_id(0)

    def block(mix_r, x_r, h1_r, q_r):
        h1 = x_r[...] + jnp.dot(mix_r[...].astype(BF16), wo_bf[...], preferred_element_type=F32)
        h1_r[...] = h1
        q_r[...] = jnp.dot(_rms(h1, g2_ref).astype(BF16), wq_bf[...],
                           preferred_element_type=F32).astype(q_r.dtype)

    @pl.when(s < N_LOAD)
    def _():
        _load_weight_chunk(s, ((wo_ref, wo_bf), (wq_ref, wq_bf)))

    @pl.when(s >= N_LOAD)
    def _():
        block(mix_ref, x_ref, h1_ref, q_ref)

    @pl.when(s == N_LOAD)
    def _():
        block(mixs_ref, xs_ref, h1s_ref, qs_ref)


def _out_q(mix, x, mix_s, xs, w_out, w_cq, g2, *, tm):
    t, d = x.shape
    ts = xs.shape[0]
    return pl.pallas_call(
        _outq_kernel,
        grid=(N_LOAD + t // tm,),
        in_specs=[_tile_spec(tm), _tile_spec(tm), _w_chunk_spec(), _w_chunk_spec(), _resident((1, d)),
                  _resident((ts, d)), _resident((ts, d))],
        out_specs=[_tile_spec(tm), _tile_spec(tm), _whole_out((ts, d)), _whole_out((ts, d))],
        out_shape=[jax.ShapeDtypeStruct((t, d), F32), jax.ShapeDtypeStruct((t, d), BF16),
                   jax.ShapeDtypeStruct((ts, d), F32), jax.ShapeDtypeStruct((ts, d), F32)],
        scratch_shapes=[pltpu.VMEM((d, d), BF16), pltpu.VMEM((d, d), BF16)],
        compiler_params=_params(("arbitrary",)),
        name="out_q",
    )(mix, x, w_out, w_cq, g2, mix_s, xs)


def _softmax_rows(s):
    m = jnp.max(s, axis=-1, keepdims=True)
    e = jnp.exp(s - m)
    return e / jnp.sum(e, axis=-1, keepdims=True)


def _attn_co_kernel(q_ref, mk_ref, mv_ref, h1_ref, wc_ref, g3_ref, atts_ref, h1s_ref,
                    h2_ref, xn_ref, h2s_ref, xns_ref, wc_bf, att_ref):
    s = pl.program_id(0)

    def co_block(att_r, h1_r, h2_r, xn_r):
        h2 = h1_r[...] + jnp.dot(att_r[...].astype(BF16), wc_bf[...], preferred_element_type=F32)
        h2_r[...] = h2
        xn_r[...] = _rms(h2, g3_ref).astype(BF16)

    @pl.when(s < N_LOAD)
    def _():
        _load_weight_chunk(s, ((wc_ref, wc_bf),))

    @pl.when(s >= N_LOAD)
    def _():
        for h in range(H_X):
            cols = slice(h * DH_X, (h + 1) * DH_X)
            sc = lax.dot_general(q_ref[:, cols], mk_ref[:, cols], (((1,), (1,)), ((), ())),
                                 preferred_element_type=F32) * (DH_X ** -0.5)
            p = _softmax_rows(sc)
            att_ref[:, cols] = jnp.dot(p.astype(BF16), mv_ref[:, cols],
                                       preferred_element_type=F32).astype(BF16)
        co_block(att_ref, h1_ref, h2_ref, xn_ref)

    @pl.when(s == N_LOAD)
    def _():
        co_block(atts_ref, h1s_ref, h2s_ref, xns_ref)


def _attn_co(q, mk, mv, h1, att_s, h1_s, w_co, g3, *, tm, tiles_per_batch):
    t, d = h1.shape
    ts = h1_s.shape[0]
    mem_spec = pl.BlockSpec((N_MEM, d), lambda s: (jnp.maximum(s - N_LOAD, 0) // tiles_per_batch, 0))
    return pl.pallas_call(
        _attn_co_kernel,
        grid=(N_LOAD + t // tm,),
        in_specs=[_tile_spec(tm), mem_spec, mem_spec, _tile_spec(tm), _w_chunk_spec(), _resident((1, d)),
                  _resident((ts, d)), _resident((ts, d))],
        out_specs=[_tile_spec(tm), _tile_spec(tm), _whole_out((ts, d)), _whole_out((ts, d))],
        out_shape=[jax.ShapeDtypeStruct((t, d), F32), jax.ShapeDtypeStruct((t, d), BF16),
                   jax.ShapeDtypeStruct((ts, d), F32), jax.ShapeDtypeStruct((ts, d), BF16)],
        scratch_shapes=[pltpu.VMEM((d, d), BF16), pltpu.VMEM((tm, d), BF16)],
        compiler_params=_params(("arbitrary",)),
        name="attn_co",
    )(q, mk, mv, h1, w_co, g3, att_s, h1_s)


def _mix_prompt_kernel(u_ref, v_ref, q_ref, k_ref, vb_ref, g_ref, ws_ref, bt_ref, gn_ref,
                       us_ref, vs_ref, qs_ref, ks_ref, vbs_ref, gs_ref, w0_ref, b0_ref, ss_ref,
                       o_ref, so_ref, os_ref, sso_ref,
                       st_ref, wt_ref, bias_ref, dm_ref, qd_ref, kd_ref, *, batch, n_chunks):
    n = pl.program_id(0)
    _mix_sample_rows(us_ref, vs_ref, qs_ref, ks_ref, vbs_ref, gs_ref, w0_ref, b0_ref, gn_ref, ss_ref,
                     os_ref, sso_ref)

    @pl.when(n == 0)
    def _():
        ii = lax.broadcasted_iota(jnp.int32, (CHUNK, CHUNK), 0)
        jj = lax.broadcasted_iota(jnp.int32, (CHUNK, CHUNK), 1)
        causal = ii >= jj
        diff = jnp.maximum((ii - jj).astype(F32), 0.0)
        ridx = ii.astype(F32)
        for h in range(H_A):
            lg = LOG_G[h]
            wt_ref[h] = (ws_ref[h] * causal.astype(F32)).astype(BF16)
            bias_ref[h] = jnp.broadcast_to(bt_ref[:, h:h + 1], (CHUNK, CH_A))
            dm_ref[h] = jnp.where(causal, jnp.exp(lg * diff), 0.0)
            qd_ref[h] = jnp.exp(lg * (ridx + 1.0))
            kd_ref[h] = jnp.exp(lg * (CHUNK - 1.0 - ridx))
        st_ref[...] = jnp.zeros_like(st_ref)

    for b in range(batch):
        for h in range(H_A):
            cols = slice(h * CH_A, (h + 1) * CH_A)
            mixed = jnp.dot(wt_ref[h], v_ref[b, :, cols], preferred_element_type=F32) + bias_ref[h]
            o_ref[b, :, cols] = (u_ref[b, :, cols].astype(F32) * mixed).astype(BF16)

            kh = k_ref[b, :, cols]
            vh = vb_ref[b, :, cols]
            state = st_ref[b, h]
            state_bf = state.astype(BF16)
            qh = q_ref[b, :, cols]
            scores = lax.dot_general(qh, kh, (((1,), (1,)), ((), ())), preferred_element_type=F32) * dm_ref[h]
            intra = jnp.dot(scores.astype(BF16), vh, preferred_element_type=F32)
            cross = jnp.dot(qh, state_bf, preferred_element_type=F32) * qd_ref[h]
            ret = intra + cross
            mu = jnp.mean(ret, axis=-1, keepdims=True)
            rc = ret - mu
            var = jnp.mean(rc * rc, axis=-1, keepdims=True)
            normed = rc * lax.rsqrt(var + EPS) * gn_ref[:, cols]
            o_ref[b, :, W_A + h * DV_B:W_A + (h + 1) * DV_B] = (
                g_ref[b, :, cols].astype(F32) * normed).astype(BF16)
            kd = (kh.astype(F32) * kd_ref[h]).astype(BF16)
            st_ref[b, h] = state * math.exp(LOG_G[h] * CHUNK) + lax.dot_general(
                kd, vh, (((0,), (0,)), ((), ())), preferred_element_type=F32)

    @pl.when(n == n_chunks - 1)
    def _():
        so_ref[...] = st_ref[...]


def _mixers(u, rest, u_s, rest_s, ws, bt, gn, w0, b0, state_s, *, n_chunks):
    batch, seq, _ = u.shape
    ts = u_s.shape[0]
    rb = ts // n_chunks
    assert rb * n_chunks == ts and rb % 8 == 0

    def sec(s):
        return pl.BlockSpec((batch, CHUNK, SEC), lambda n, s=s: (0, n, s))

    def sec_s(s):
        return pl.BlockSpec((None, rb, SEC), lambda n, s=s: (s, n, 0))

    state_spec = pl.BlockSpec((rb, H_B, DK_B, DV_B), lambda n: (n, 0, 0, 0))
    table = pltpu.VMEM((H_B, CHUNK, CHUNK), F32)
    return pl.pallas_call(
        functools.partial(_mix_prompt_kernel, batch=batch, n_chunks=n_chunks),
        grid=(n_chunks,),
        in_specs=[sec(0), sec(0), sec(1), sec(2), sec(3), sec(4),
                  _resident((H_A, CHUNK, CHUNK)), _resident((CHUNK, H_A)), _resident((1, W_B)),
                  pl.BlockSpec((rb, SEC), lambda n: (n, 0)), sec_s(0), sec_s(1), sec_s(2), sec_s(3), sec_s(4),
                  _resident((1, W_A)), _resident((1, W_A)), state_spec],
        out_specs=[pl.BlockSpec((batch, CHUNK, W_A + W_B), lambda n: (0, n, 0)),
                   _whole_out((batch, H_B, DK_B, DV_B)),
                   pl.BlockSpec((rb, W_A + W_B), lambda n: (n, 0)), state_spec],
        out_shape=[jax.ShapeDtypeStruct((batch, seq, W_A + W_B), BF16),
                   jax.ShapeDtypeStruct((batch, H_B, DK_B, DV_B), F32),
                   jax.ShapeDtypeStruct((ts, W_A + W_B), F32),
                   jax.ShapeDtypeStruct(state_s.shape, F32)],
        scratch_shapes=[pltpu.VMEM((batch, H_B, DK_B, DV_B), F32), pltpu.VMEM((H_A, CHUNK, CHUNK), BF16),
                        table, table, table, table],
        compiler_params=_params(("arbitrary",)),
        name="mixers",
    )(u, rest, rest, rest, rest, rest, ws, bt, gn,
      u_s, rest_s, rest_s, rest_s, rest_s, rest_s, w0, b0, state_s)


def _mix_sample_rows(u_ref, v_ref, q_ref, k_ref, vb_ref, g_ref, w0_ref, b0_ref, gn_ref, s_ref,
                     o_ref, so_ref):
    n_rows = u_ref.shape[0]
    rid = lax.broadcasted_iota(jnp.int32, (n_rows, 1), 0)
    o_ref[:, :W_A] = u_ref[...] * (w0_ref[...] * v_ref[...] + b0_ref[...])

    for h in range(H_B):
        cols = slice(h * DK_B, (h + 1) * DK_B)
        g_h = math.exp(LOG_G[h])
        q_blk = q_ref[:, cols]
        k_blk = k_ref[:, cols]
        v_blk = vb_ref[:, cols]
        v_bf = v_blk.astype(BF16)
        intra = jnp.sum(q_blk * k_blk, axis=-1, keepdims=True) * v_blk
        cross = jnp.zeros((n_rows, DV_B), F32)
        for r in range(n_rows):
            state = s_ref[r, h]
            q_only_r = jnp.where(rid == r, q_blk, 0.0).astype(BF16)
            cross = cross + jnp.dot(q_only_r, state.astype(BF16), preferred_element_type=F32)
            k_only_r = jnp.where(rid == r, k_blk, 0.0).astype(BF16)
            outer = lax.dot_general(k_only_r, v_bf, (((0,), (0,)), ((), ())), preferred_element_type=F32)
            so_ref[r, h] = state * g_h + outer
        ret = intra + cross * g_h
        mu = jnp.mean(ret, axis=-1, keepdims=True)
        rc = ret - mu
        var = jnp.mean(rc * rc, axis=-1, keepdims=True)
        normed = rc * lax.rsqrt(var + EPS) * gn_ref[:, cols]
        o_ref[:, W_A + h * DV_B:W_A + (h + 1) * DV_B] = g_ref[:, cols] * normed


ATTN_SAMPLE_ROWS = 2
SUBLANES = 8
MEM_CHUNK = 256
MEM_LOAD = D_MODEL // MEM_CHUNK


def _attn_mem_kernel(q_ref, ck_ref, cv_ref, x_ref, g_ref, wk_ref, wv_ref,
                     o_ref, k5_ref, k2_ref, v5_ref, v2_ref, wb_ref, *, batch):
    i = pl.program_id(0)
    k_proj, v_load, v_proj = MEM_LOAD, MEM_LOAD + batch, 2 * MEM_LOAD + batch

    def load(w_ref, c):
        wb_ref[pl.ds(pl.multiple_of(c * MEM_CHUNK, MEM_CHUNK), MEM_CHUNK), :] = w_ref[...].astype(BF16)

    def project(o5_ref, o2_ref):
        res = jnp.dot(_rms(x_ref[...], g_ref).astype(BF16), wb_ref[...], preferred_element_type=F32)
        o2_ref[...] = res.astype(BF16)
        for h in range(H_X):
            o5_ref[0, 0, :, h, :] = res[:, h * DH_X:(h + 1) * DH_X]

    pl.when(i < k_proj)(lambda: load(wk_ref, i))
    pl.when((i >= k_proj) & (i < v_load))(lambda: project(k5_ref, k2_ref))
    pl.when((i >= v_load) & (i < v_proj))(lambda: load(wv_ref, i - v_load))
    pl.when((i >= v_proj) & (i < v_proj + batch))(lambda: project(v5_ref, v2_ref))

    pairs = N_MEM * H_X
    fold = lambda t: pltpu.roll(t, H_X, 1)
    for r in range(ATTN_SAMPLE_ROWS):
        row = pl.ds(i * ATTN_SAMPLE_ROWS + r, 1)
        q = q_ref[row, :]
        q8 = jnp.concatenate([q[:, (s % H_X) * DH_X:(s % H_X + 1) * DH_X] for s in range(SUBLANES)], axis=0)
        k3 = ck_ref[0, r].reshape(pairs // SUBLANES, SUBLANES, DH_X)
        v3 = cv_ref[0, r].reshape(pairs // SUBLANES, SUBLANES, DH_X)
        s = jnp.sum(k3 * q8[None], axis=-1, keepdims=True) * (DH_X ** -0.5)
        m = jnp.max(s, axis=0, keepdims=True)
        m = jnp.maximum(m, fold(m))
        e = jnp.exp(s - m)
        l = jnp.sum(e, axis=0, keepdims=True)
        acc = jnp.sum(e * v3, axis=0, keepdims=True)
        out = ((acc + fold(acc)) / (l + fold(l)))[0]
        for h in range(H_X):
            o_ref[row, h * DH_X:(h + 1) * DH_X] = out[h:h + 1, :]


def _attn_mem(q, ck, cv, mem, g, w_k, w_v, *, batch):
    t, d = q.shape
    rb = ATTN_SAMPLE_ROWS
    steps = t // rb
    k_proj, v_load, v_proj = MEM_LOAD, MEM_LOAD + batch, 2 * MEM_LOAD + batch
    assert v_proj + batch <= steps

    def prompt(i):
        return jnp.where(i < v_load, jnp.clip(i - k_proj, 0, batch - 1), jnp.clip(i - v_proj, 0, batch - 1))

    def k_idx(i):
        return jnp.clip(i - k_proj, 0, batch - 1)

    def v_idx(i):
        return jnp.clip(i - v_proj, 0, batch - 1)

    cache_spec = pl.BlockSpec((1, rb, N_MEM, H_X, DH_X), lambda i: (0, i, 0, 0, 0))
    out5 = jax.ShapeDtypeStruct((1, batch, N_MEM, H_X, DH_X), F32)
    out2 = jax.ShapeDtypeStruct((batch * N_MEM, d), BF16)
    return pl.pallas_call(
        functools.partial(_attn_mem_kernel, batch=batch),
        grid=(steps,),
        in_specs=[_resident((t, d)), cache_spec, cache_spec,
                  pl.BlockSpec((N_MEM, d), lambda i: (prompt(i), 0)), _resident((1, d)),
                  pl.BlockSpec((MEM_CHUNK, d), lambda i: (jnp.clip(i, 0, MEM_LOAD - 1), 0)),
                  pl.BlockSpec((MEM_CHUNK, d), lambda i: (jnp.clip(i - v_load, 0, MEM_LOAD - 1), 0))],
        out_specs=[_whole_out((t, d)),
                   pl.BlockSpec((1, 1, N_MEM, H_X, DH_X), lambda i: (0, k_idx(i), 0, 0, 0)),
                   pl.BlockSpec((N_MEM, d), lambda i: (k_idx(i), 0)),
                   pl.BlockSpec((1, 1, N_MEM, H_X, DH_X), lambda i: (0, v_idx(i), 0, 0, 0)),
                   pl.BlockSpec((N_MEM, d), lambda i: (v_idx(i), 0))],
        out_shape=[jax.ShapeDtypeStruct((t, d), F32), out5, out2, out5, out2],
        scratch_shapes=[pltpu.VMEM((d, d), BF16)],
        compiler_params=_params(("arbitrary",)),
        name="attn_mem",
    )(q, ck, cv, mem, g, w_k, w_v)


def _rope_tables(pos):
    half = DK_B // 2
    freqs = np.float64(ROPE_THETA) ** (-np.arange(half, dtype=np.float64) / half)
    ang = np.asarray(pos, np.float64)[:, None] * freqs[None, :]
    cos = np.cos(ang)
    sin = np.sin(ang)
    return (jnp.asarray(np.concatenate([cos, cos], axis=-1), F32),
            jnp.asarray(np.concatenate([-sin, sin], axis=-1), F32))


def kernel(x_prompt, x_sample, mem_prompt, cache_mem_k, cache_mem_v, state_ret, norm1_g, w_in, sgu_norm_g, sgu_w_s, sgu_b, ret_gn_g, w_out, norm2_g, mem_norm_g, w_cq, w_ck, w_cv, w_co, norm3_g, w_ff1, w_ff2, final_norm_g):
    batch, seq, d = x_prompt.shape
    ts = x_sample.shape[0]
    n_chunks = seq // CHUNK
    tp = batch * seq

    g1 = norm1_g[0][None, :]
    g2 = norm2_g[0][None, :]
    g3 = norm3_g[0][None, :]
    gm = mem_norm_g[0][None, :]
    gf = final_norm_g[None, :]
    lng = sgu_norm_g[0][None, :]
    gn = ret_gn_g[0][None, :]
    ws = sgu_w_s[0]
    sb = sgu_b[0]
    xp = x_prompt.reshape(tp, d)
    xs = x_sample.reshape(ts, d)

    rope_p = _rope_tables(np.arange(seq))
    rope_s = _rope_tables(np.full((ts,), PAST_LEN))
    u_p, xn_p, u_s, xn_s = _in_proj_u(xp, xs, g1, w_in[0], tm=TM_IN_PROJ)
    rest_p, rest_s, v_rows = _in_proj_rest(xn_p, xn_s, w_in[0], rope_p, rope_s, lng, tm=TM_IN_PROJ)
    w0 = jnp.repeat(ws[:, 0, 0], CH_A)[None, :]
    b0 = jnp.repeat(sb[:, 0], CH_A)[None, :]
    mix_p, state_p, mix_s, state_s = _mixers(
        u_p.reshape(batch, seq, SEC), rest_p.reshape(batch, seq, (N_SEC - 1) * SEC), u_s, rest_s,
        ws, sb.T, gn, w0, b0, state_ret[0], n_chunks=n_chunks)
    mix_p = mix_p.reshape(tp, W_A + W_B)

    mem = mem_prompt.reshape(batch * N_MEM, d)
    h1_p, q_p, h1_s, q_s = _out_q(mix_p, xp, mix_s, xs, w_out[0], w_cq[0], g2, tm=TM_RESIDENT)
    att_s, mk5, mk, mv5, mv = _attn_mem(q_s, cache_mem_k, cache_mem_v, mem, gm, w_ck[0], w_cv[0], batch=batch)
    h2_p, xn3_p, h2_s, xn3_s = _attn_co(q_p, mk, mv, h1_p, att_s, h1_s, w_co[0], g3,
                                        tm=TM_RESIDENT, tiles_per_batch=seq // TM_RESIDENT)

    hid_p, hid_s = _ff1(xn3_p, xn3_s, w_ff1[0], tm=TM_MLP, tn=TN_MLP)
    h3_p, part_s = _ff2(hid_p, hid_s, w_ff2[0], h2_p, tm=TM_MLP, tn=TN_MLP, tk=TK_MLP)
    y_p = _final_norm(h3_p, gf, tm=TM_NORM)
    y_s = _add_norm_sample(part_s, h2_s, gf)

    return (y_p.reshape(batch, seq, d),
            y_s,
            mk5,
            mv5,
            state_p[None],
            state_s[None],
            v_rows[None])
```

```python
import functools
import math

import jax
import jax.numpy as jnp
import numpy as np
from jax import lax
from jax.experimental import pallas as pl
from jax.experimental.pallas import tpu as pltpu

F32 = jnp.float32
BF16 = jnp.bfloat16

D_MODEL = 2048
H_A = 8
CH_A = 128
W_A = H_A * CH_A
CHUNK = 128
H_B = 8
DK_B = 128
DV_B = 128
W_B = H_B * DV_B
ROPE_THETA = 10000.0
N_MEM = 256
H_X = 4
DH_X = D_MODEL // H_X
EPS = 1e-6
PAST_LEN = 16384

SEC = 1024
N_SEC = 6
LOG_G = tuple(math.log(1.0 - 2.0 ** (-5.0 - h)) for h in range(H_B))

V7X_VMEM_BYTES = 64 * 1024 * 1024
COMPILER_SCRATCH_BYTES = 4 * 1024 * 1024
VMEM_LIMIT = V7X_VMEM_BYTES - COMPILER_SCRATCH_BYTES
ROW_CHUNK = 256

TM_IN_PROJ = 1024
TM_RESIDENT = 512
TM_MLP, TN_MLP, TK_MLP = 2048, 1024, 1024
TM_NORM = 1024


def _params(sem):
    return pltpu.CompilerParams(dimension_semantics=sem, vmem_limit_bytes=VMEM_LIMIT)


def _resident(shape):
    zeros = (0,) * len(shape)
    return pl.BlockSpec(shape, lambda *_: zeros, pipeline_mode=pl.Buffered(1))


def _whole_out(shape):
    zeros = (0,) * len(shape)
    return pl.BlockSpec(shape, lambda *_: zeros)


def _row_loop(n_rows, fn):
    rc = min(n_rows, ROW_CHUNK)
    assert n_rows % rc == 0

    def body(c, carry):
        fn(pl.ds(pl.multiple_of(c * rc, rc), rc))
        return carry

    lax.fori_loop(0, n_rows // rc, body, 0)


def _rms(h, g_ref):
    ms = jnp.mean(h * h, axis=-1, keepdims=True)
    return h * lax.rsqrt(ms + EPS) * g_ref[...]


def _rms_rows(x_ref, g_ref, xn_ref, n_rows):
    def fn(rows):
        xn_ref[rows, :] = _rms(x_ref[rows, :], g_ref).astype(BF16)

    _row_loop(n_rows, fn)


def _cast_rows(w_ref, wb_ref):
    def fn(rows):
        wb_ref[rows, :] = w_ref[rows, :].astype(BF16)

    _row_loop(w_ref.shape[0], fn)


def _inproj_u_kernel(x_ref, g_ref, w_ref, xs_ref, u_ref, xn_ref, us_ref, xsn_ref, wb_ref, *, tm, ts):
    @pl.when(pl.program_id(0) == 0)
    def _():
        _cast_rows(w_ref, wb_ref)
        _rms_rows(xs_ref, g_ref, xsn_ref, ts)
        us_ref[...] = jax.nn.gelu(jnp.dot(xsn_ref[...], wb_ref[...], preferred_element_type=F32))

    _rms_rows(x_ref, g_ref, xn_ref, tm)
    u_ref[...] = jax.nn.gelu(jnp.dot(xn_ref[...], wb_ref[...], preferred_element_type=F32)).astype(BF16)


def _in_proj_u(x, xs, g, w, *, tm):
    t, d = x.shape
    ts = xs.shape[0]
    return pl.pallas_call(
        functools.partial(_inproj_u_kernel, tm=tm, ts=ts),
        grid=(t // tm,),
        in_specs=[pl.BlockSpec((tm, d), lambda i: (i, 0)), _resident((1, d)),
                  pl.BlockSpec((d, SEC), lambda i: (0, 0), pipeline_mode=pl.Buffered(1)),
                  _resident((ts, d))],
        out_specs=[pl.BlockSpec((tm, SEC), lambda i: (i, 0)), pl.BlockSpec((tm, d), lambda i: (i, 0)),
                   _whole_out((ts, SEC)), _whole_out((ts, d))],
        out_shape=[jax.ShapeDtypeStruct((t, SEC), BF16), jax.ShapeDtypeStruct((t, d), BF16),
                   jax.ShapeDtypeStruct((ts, SEC), F32), jax.ShapeDtypeStruct((ts, d), BF16)],
        scratch_shapes=[pltpu.VMEM((d, SEC), BF16)],
        compiler_params=_params(("arbitrary",)),
        name="in_proj_u",
    )(x, g, w, xs)


def _inproj_rest_kernel(xn_ref, w_ref, cc_ref, ss_ref, lng_ref, xsn_ref, ccs_ref, sss_ref,
                        o_ref, os_ref, vrows_ref, wb_ref):
    j = pl.program_id(0)
    i = pl.program_id(1)

    @pl.when(i == 0)
    def _():
        _cast_rows(w_ref, wb_ref)

    def layernorm(acc, cc, ss):
        z = jax.nn.gelu(acc)
        mu = jnp.mean(z, axis=-1, keepdims=True)
        zc = z - mu
        var = jnp.mean(zc * zc, axis=-1, keepdims=True)
        return zc * lax.rsqrt(var + EPS) * lng_ref[...]

    def rotary(acc, cc, ss):
        scale = jnp.where(j == 2, DK_B ** -0.5, 1.0).astype(F32)
        heads = []
        for h in range(H_B):
            blk = acc[:, h * DK_B:(h + 1) * DK_B]
            rot = pltpu.roll(blk, DK_B // 2, 1)
            heads.append((blk * cc + rot * ss) * scale)
        return jnp.concatenate(heads, axis=-1)

    def section(pred, epilogue):
        @pl.when(pred)
        def _():
            acc = jnp.dot(xn_ref[...], wb_ref[...], preferred_element_type=F32)
            o_ref[...] = epilogue(acc, cc_ref[...], ss_ref[...]).astype(o_ref.dtype)

            @pl.when(i == 0)
            def _():
                acc_s = jnp.dot(xsn_ref[...], wb_ref[...], preferred_element_type=F32)
                os_ref[j] = epilogue(acc_s, ccs_ref[...], sss_ref[...])

    section(j == 0, layernorm)
    section((j == 1) | (j == 2), rotary)
    section(j == 3, lambda acc, cc, ss: acc)
    section(j == 4, lambda acc, cc, ss: jax.nn.silu(acc))

    @pl.when((j == 0) & (i == 0))
    def _():
        for h in range(H_A):
            vrows_ref[:, 0, h, :] = os_ref[0, :, h * CH_A:(h + 1) * CH_A]


def _in_proj_rest(xn, xsn, w, rope_p, rope_s, lng, *, tm):
    t, d = xn.shape
    ts = xsn.shape[0]
    n_rest = N_SEC - 1
    rope_blocks = rope_p[0].shape[0] // tm
    rope_spec = pl.BlockSpec((tm, DK_B), lambda j, i: (i % rope_blocks, 0))
    return pl.pallas_call(
        _inproj_rest_kernel,
        grid=(n_rest, t // tm),
        in_specs=[pl.BlockSpec((tm, d), lambda j, i: (i, 0)),
                  pl.BlockSpec((d, SEC), lambda j, i: (0, j + 1)),
                  rope_spec, rope_spec, _resident((1, SEC)),
                  _resident((ts, d)), _resident((ts, DK_B)), _resident((ts, DK_B))],
        out_specs=[pl.BlockSpec((tm, SEC), lambda j, i: (i, j)), _whole_out((n_rest, ts, SEC)),
                   _whole_out((ts, 1, H_A, CH_A))],
        out_shape=[jax.ShapeDtypeStruct((t, n_rest * SEC), BF16),
                   jax.ShapeDtypeStruct((n_rest, ts, SEC), F32),
                   jax.ShapeDtypeStruct((ts, 1, H_A, CH_A), F32)],
        scratch_shapes=[pltpu.VMEM((d, SEC), BF16)],
        compiler_params=_params(("arbitrary", "arbitrary")),
        name="in_proj_rest",
    )(xn, w, rope_p[0], rope_p[1], lng, xsn, rope_s[0], rope_s[1])


def _relu2(acc):
    return jnp.square(jnp.maximum(acc, 0.0))


def _ff1_kernel(x_ref, w_ref, xs_ref, o_ref, os_ref):
    o_ref[...] = _relu2(jnp.dot(x_ref[...], w_ref[...].astype(BF16), preferred_element_type=F32)).astype(BF16)

    @pl.when(pl.program_id(0) == 0)
    def _():
        os_ref[pl.program_id(1)] = _relu2(jnp.dot(xs_ref[...], w_ref[...].astype(BF16),
                                                  preferred_element_type=F32)).astype(BF16)


def _ff1(x, xs, w, *, tm, tn):
    t, k = x.shape
    ts = xs.shape[0]
    n = w.shape[1]
    return pl.pallas_call(
        _ff1_kernel,
        grid=(t // tm, n // tn),
        in_specs=[pl.BlockSpec((tm, k), lambda i, j: (i, 0)),
                  pl.BlockSpec((k, tn), lambda i, j: (0, j)),
                  _resident((ts, k))],
        out_specs=[pl.BlockSpec((tm, tn), lambda i, j: (i, j)), _whole_out((n // tn, ts, tn))],
        out_shape=[jax.ShapeDtypeStruct((t, n), BF16), jax.ShapeDtypeStruct((n // tn, ts, tn), BF16)],
        compiler_params=_params(("arbitrary", "arbitrary")),
        name="ff1",
    )(x, w, xs)


def _ff2_kernel(x_ref, w_ref, r_ref, xs_ref, o_ref, os_ref):
    i, j, k = pl.program_id(0), pl.program_id(1), pl.program_id(2)

    @pl.when(k == 0)
    def _():
        o_ref[...] = r_ref[...]

    o_ref[...] += jnp.dot(x_ref[...], w_ref[...].astype(BF16), preferred_element_type=F32)

    @pl.when(i == 0)
    def _():
        part = jnp.dot(xs_ref[k], w_ref[...].astype(BF16), preferred_element_type=F32)

        @pl.when(k == 0)
        def _():
            os_ref[j] = part

        @pl.when(k > 0)
        def _():
            os_ref[j] += part


def _ff2(x, xs, w, r, *, tm, tn, tk):
    t, k = x.shape
    ts = xs.shape[1]
    n = w.shape[1]
    assert xs.shape == (k // tk, ts, tk), "sample tiles must be stacked in contraction slices of tk"
    return pl.pallas_call(
        _ff2_kernel,
        grid=(t // tm, n // tn, k // tk),
        in_specs=[pl.BlockSpec((tm, tk), lambda i, j, kk: (i, kk)),
                  pl.BlockSpec((tk, tn), lambda i, j, kk: (kk, j)),
                  pl.BlockSpec((tm, tn), lambda i, j, kk: (i, j)),
                  _resident((k // tk, ts, tk))],
        out_specs=[pl.BlockSpec((tm, tn), lambda i, j, kk: (i, j)), _whole_out((n // tn, ts, tn))],
        out_shape=[jax.ShapeDtypeStruct((t, n), F32), jax.ShapeDtypeStruct((n // tn, ts, tn), F32)],
        compiler_params=_params(("arbitrary", "arbitrary", "arbitrary")),
        name="ff2",
    )(x, w, r, xs)


def _final_norm_kernel(x_ref, g_ref, o_ref, *, tm):
    def fn(rows):
        o_ref[rows, :] = _rms(x_ref[rows, :], g_ref)

    _row_loop(tm, fn)


def _final_norm(x, g, *, tm):
    t, d = x.shape
    row = pl.BlockSpec((tm, d), lambda i: (i, 0))
    return pl.pallas_call(
        functools.partial(_final_norm_kernel, tm=tm),
        grid=(t // tm,),
        in_specs=[row, _resident((1, d))],
        out_specs=row,
        out_shape=jax.ShapeDtypeStruct((t, d), F32),
        compiler_params=_params(("arbitrary",)),
        name="final_norm",
    )(x, g)


def _add_norm_sample_kernel(x_ref, r_ref, g_ref, o_ref):
    y = r_ref[...] + jnp.concatenate([x_ref[j] for j in range(x_ref.shape[0])], axis=-1)
    o_ref[:, 0, :] = _rms(y, g_ref)


def _add_norm_sample(x, r, g):
    ts, d = r.shape
    return pl.pallas_call(
        _add_norm_sample_kernel,
        grid=(1,),
        in_specs=[_resident(x.shape), _resident((ts, d)), _resident((1, d))],
        out_specs=_whole_out((ts, 1, d)),
        out_shape=jax.ShapeDtypeStruct((ts, 1, d), F32),
        compiler_params=_params(("arbitrary",)),
        name="final_norm_sample",
    )(x, r, g)


W_CHUNK = 256
N_LOAD = D_MODEL // W_CHUNK


def _load_weight_chunk(s, pairs):
    rows = pl.ds(pl.multiple_of(s * W_CHUNK, W_CHUNK), W_CHUNK)
    for src, dst in pairs:
        dst[rows, :] = src[...].astype(BF16)


def _w_chunk_spec():
    return pl.BlockSpec((W_CHUNK, D_MODEL), lambda s: (jnp.minimum(s, N_LOAD - 1), 0))


def _tile_spec(tm, width=D_MODEL):
    return pl.BlockSpec((tm, width), lambda s: (jnp.maximum(s - N_LOAD, 0), 0))


def _outq_kernel(mix_ref, x_ref, wo_ref, wq_ref, g2_ref, mixs_ref, xs_ref,
                 h1_ref, q_ref, h1s_ref, qs_ref, wo_bf, wq_bf):
    s = pl.program_id(0)

    def block(mix_r, x_r, h1_r, q_r):
        h1 = x_r[...] + jnp.dot(mix_r[...].astype(BF16), wo_bf[...], preferred_element_type=F32)
        h1_r[...] = h1
        q_r[...] = jnp.dot(_rms(h1, g2_ref).astype(BF16), wq_bf[...],
                           preferred_element_type=F32).astype(q_r.dtype)

    @pl.when(s < N_LOAD)
    def _():
        _load_weight_chunk(s, ((wo_ref, wo_bf), (wq_ref, wq_bf)))

    @pl.when(s >= N_LOAD)
    def _():
        block(mix_ref, x_ref, h1_ref, q_ref)

    @pl.when(s == N_LOAD)
    def _():
        block(mixs_ref, xs_ref, h1s_ref, qs_ref)


def _out_q(mix, x, mix_s, xs, w_out, w_cq, g2, *, tm):
    t, d = x.shape
    ts = xs.shape[0]
    return pl.pallas_call(
        _outq_kernel,
        grid=(N_LOAD + t // tm,),
        in_specs=[_tile_spec(tm), _tile_spec(tm), _w_chunk_spec(), _w_chunk_spec(), _resident((1, d)),
                  _resident((ts, d)), _resident((ts, d))],
        out_specs=[_tile_spec(tm), _tile_spec(tm), _whole_out((ts, d)), _whole_out((ts, d))],
        out_shape=[jax.ShapeDtypeStruct((t, d), F32), jax.ShapeDtypeStruct((t, d), BF16),
                   jax.ShapeDtypeStruct((ts, d), F32), jax.ShapeDtypeStruct((ts, d), F32)],
        scratch_shapes=[pltpu.VMEM((d, d), BF16), pltpu.VMEM((d, d), BF16)],
        compiler_params=_params(("arbitrary",)),
        name="out_q",
    )(mix, x, w_out, w_cq, g2, mix_s, xs)


def _softmax_rows(s):
    m = jnp.max(s, axis=-1, keepdims=True)
    e = jnp.exp(s - m)
    return e / jnp.sum(e, axis=-1, keepdims=True)


def _attn_co_kernel(q_ref, mk_ref, mv_ref, h1_ref, wc_ref, g3_ref, atts_ref, h1s_ref, qs_ref, ck_ref, cv_ref,
                    h2_ref, xn_ref, h2s_ref, xns_ref, wc_bf, att_ref, tail_ref, *, n_tiles):
    s = pl.program_id(0)
    n_head = atts_ref.shape[0]

    def co_block(att, h1_r, h2_r, xn_r):
        h2 = h1_r[...] + jnp.dot(att.astype(BF16), wc_bf[...], preferred_element_type=F32)
        h2_r[...] = h2
        xn_r[...] = _rms(h2, g3_ref).astype(BF16)

    @pl.when(s < N_LOAD)
    def _():
        _load_weight_chunk(s, ((wc_ref, wc_bf),))

    @pl.when(s >= N_LOAD)
    def _():
        tile = s - N_LOAD
        _cache_attn_row(qs_ref[pl.ds(n_head + tile, 1), :], ck_ref[0, 0], cv_ref[0, 0], tail_ref, pl.ds(tile, 1))
        for h in range(H_X):
            cols = slice(h * DH_X, (h + 1) * DH_X)
            sc = lax.dot_general(q_ref[:, cols], mk_ref[:, cols], (((1,), (1,)), ((), ())),
                                 preferred_element_type=F32) * (DH_X ** -0.5)
            p = _softmax_rows(sc)
            att_ref[:, cols] = jnp.dot(p.astype(BF16), mv_ref[:, cols],
                                       preferred_element_type=F32).astype(BF16)
        co_block(att_ref[...], h1_ref, h2_ref, xn_ref)

    @pl.when(s == N_LOAD + n_tiles - 1)
    def _():
        co_block(jnp.concatenate([atts_ref[...], tail_ref[...]], axis=0), h1s_ref, h2s_ref, xns_ref)


def _attn_co(q, mk, mv, h1, att_head, h1_s, q_s, ck, cv, w_co, g3, *, tm, tiles_per_batch):
    t, d = h1.shape
    ts = h1_s.shape[0]
    n_tiles = t // tm
    n_head = att_head.shape[0]
    assert n_head + n_tiles == ts
    mem_spec = pl.BlockSpec((N_MEM, d), lambda s: (jnp.maximum(s - N_LOAD, 0) // tiles_per_batch, 0))
    cache_spec = pl.BlockSpec((1, 1, N_MEM, H_X, DH_X),
                              lambda s: (0, n_head + jnp.maximum(s - N_LOAD, 0), 0, 0, 0))
    return pl.pallas_call(
        functools.partial(_attn_co_kernel, n_tiles=n_tiles),
        grid=(N_LOAD + n_tiles,),
        in_specs=[_tile_spec(tm), mem_spec, mem_spec, _tile_spec(tm), _w_chunk_spec(), _resident((1, d)),
                  _resident((n_head, d)), _resident((ts, d)), _resident((ts, d)), cache_spec, cache_spec],
        out_specs=[_tile_spec(tm), _tile_spec(tm), _whole_out((ts, d)), _whole_out((ts, d))],
        out_shape=[jax.ShapeDtypeStruct((t, d), F32), jax.ShapeDtypeStruct((t, d), BF16),
                   jax.ShapeDtypeStruct((ts, d), F32), jax.ShapeDtypeStruct((ts, d), BF16)],
        scratch_shapes=[pltpu.VMEM((d, d), BF16), pltpu.VMEM((tm, d), BF16), pltpu.VMEM((n_tiles, d), F32)],
        compiler_params=_params(("arbitrary",)),
        name="attn_co",
    )(q, mk, mv, h1, w_co, g3, att_head, h1_s, q_s, ck, cv)


def _mix_prompt_kernel(u_ref, v_ref, q_ref, k_ref, vb_ref, g_ref, ws_ref, bt_ref, gn_ref,
                       us_ref, vs_ref, qs_ref, ks_ref, vbs_ref, gs_ref, w0_ref, b0_ref, ss_ref,
                       o_ref, so_ref, os_ref, sso_ref,
                       st_ref, wt_ref, bias_ref, dm_ref, qd_ref, kd_ref, *, batch, n_chunks):
    n = pl.program_id(0)
    _mix_sample_rows(us_ref, vs_ref, qs_ref, ks_ref, vbs_ref, gs_ref, w0_ref, b0_ref, gn_ref, ss_ref,
                     os_ref, sso_ref)

    @pl.when(n == 0)
    def _():
        ii = lax.broadcasted_iota(jnp.int32, (CHUNK, CHUNK), 0)
        jj = lax.broadcasted_iota(jnp.int32, (CHUNK, CHUNK), 1)
        causal = ii >= jj
        diff = jnp.maximum((ii - jj).astype(F32), 0.0)
        ridx = ii.astype(F32)
        for h in range(H_A):
            lg = LOG_G[h]
            wt_ref[h] = (ws_ref[h] * causal.astype(F32)).astype(BF16)
            bias_ref[h] = jnp.broadcast_to(bt_ref[:, h:h + 1], (CHUNK, CH_A))
            dm_ref[h] = jnp.where(causal, jnp.exp(lg * diff), 0.0)
            qd_ref[h] = jnp.exp(lg * (ridx + 1.0))
            kd_ref[h] = jnp.exp(lg * (CHUNK - 1.0 - ridx))
        st_ref[...] = jnp.zeros_like(st_ref)

    for b in range(batch):
        for h in range(H_A):
            cols = slice(h * CH_A, (h + 1) * CH_A)
            mixed = jnp.dot(wt_ref[h], v_ref[b, :, cols], preferred_element_type=F32) + bias_ref[h]
            o_ref[b, :, cols] = (u_ref[b, :, cols].astype(F32) * mixed).astype(BF16)

            kh = k_ref[b, :, cols]
            vh = vb_ref[b, :, cols]
            state = st_ref[b, h]
            state_bf = state.astype(BF16)
            qh = q_ref[b, :, cols]
            scores = lax.dot_general(qh, kh, (((1,), (1,)), ((), ())), preferred_element_type=F32) * dm_ref[h]
            intra = jnp.dot(scores.astype(BF16), vh, preferred_element_type=F32)
            cross = jnp.dot(qh, state_bf, preferred_element_type=F32) * qd_ref[h]
            ret = intra + cross
            mu = jnp.mean(ret, axis=-1, keepdims=True)
            rc = ret - mu
            var = jnp.mean(rc * rc, axis=-1, keepdims=True)
            normed = rc * lax.rsqrt(var + EPS) * gn_ref[:, cols]
            o_ref[b, :, W_A + h * DV_B:W_A + (h + 1) * DV_B] = (
                g_ref[b, :, cols].astype(F32) * normed).astype(BF16)
            kd = (kh.astype(F32) * kd_ref[h]).astype(BF16)
            st_ref[b, h] = state * math.exp(LOG_G[h] * CHUNK) + lax.dot_general(
                kd, vh, (((0,), (0,)), ((), ())), preferred_element_type=F32)

    @pl.when(n == n_chunks - 1)
    def _():
        so_ref[...] = st_ref[...]


def _mixers(u, rest, u_s, rest_s, ws, bt, gn, w0, b0, state_s, *, n_chunks):
    batch, seq, _ = u.shape
    ts = u_s.shape[0]
    rb = ts // n_chunks
    assert rb * n_chunks == ts and rb % 8 == 0

    def sec(s):
        return pl.BlockSpec((batch, CHUNK, SEC), lambda n, s=s: (0, n, s))

    def sec_s(s):
        return pl.BlockSpec((None, rb, SEC), lambda n, s=s: (s, n, 0))

    state_spec = pl.BlockSpec((rb, H_B, DK_B, DV_B), lambda n: (n, 0, 0, 0))
    table = pltpu.VMEM((H_B, CHUNK, CHUNK), F32)
    return pl.pallas_call(
        functools.partial(_mix_prompt_kernel, batch=batch, n_chunks=n_chunks),
        grid=(n_chunks,),
        in_specs=[sec(0), sec(0), sec(1), sec(2), sec(3), sec(4),
                  _resident((H_A, CHUNK, CHUNK)), _resident((CHUNK, H_A)), _resident((1, W_B)),
                  pl.BlockSpec((rb, SEC), lambda n: (n, 0)), sec_s(0), sec_s(1), sec_s(2), sec_s(3), sec_s(4),
                  _resident((1, W_A)), _resident((1, W_A)), state_spec],
        out_specs=[pl.BlockSpec((batch, CHUNK, W_A + W_B), lambda n: (0, n, 0)),
                   _whole_out((batch, H_B, DK_B, DV_B)),
                   pl.BlockSpec((rb, W_A + W_B), lambda n: (n, 0)), state_spec],
        out_shape=[jax.ShapeDtypeStruct((batch, seq, W_A + W_B), BF16),
                   jax.ShapeDtypeStruct((batch, H_B, DK_B, DV_B), F32),
                   jax.ShapeDtypeStruct((ts, W_A + W_B), F32),
                   jax.ShapeDtypeStruct(state_s.shape, F32)],
        scratch_shapes=[pltpu.VMEM((batch, H_B, DK_B, DV_B), F32), pltpu.VMEM((H_A, CHUNK, CHUNK), BF16),
                        table, table, table, table],
        compiler_params=_params(("arbitrary",)),
        name="mixers",
    )(u, rest, rest, rest, rest, rest, ws, bt, gn,
      u_s, rest_s, rest_s, rest_s, rest_s, rest_s, w0, b0, state_s)


def _mix_sample_rows(u_ref, v_ref, q_ref, k_ref, vb_ref, g_ref, w0_ref, b0_ref, gn_ref, s_ref,
                     o_ref, so_ref):
    n_rows = u_ref.shape[0]
    rid = lax.broadcasted_iota(jnp.int32, (n_rows, 1), 0)
    o_ref[:, :W_A] = u_ref[...] * (w0_ref[...] * v_ref[...] + b0_ref[...])

    for h in range(H_B):
        cols = slice(h * DK_B, (h + 1) * DK_B)
        g_h = math.exp(LOG_G[h])
        q_blk = q_ref[:, cols]
        k_blk = k_ref[:, cols]
        v_blk = vb_ref[:, cols]
        v_bf = v_blk.astype(BF16)
        intra = jnp.sum(q_blk * k_blk, axis=-1, keepdims=True) * v_blk
        cross = jnp.zeros((n_rows, DV_B), F32)
        for r in range(n_rows):
            state = s_ref[r, h]
            q_only_r = jnp.where(rid == r, q_blk, 0.0).astype(BF16)
            cross = cross + jnp.dot(q_only_r, state.astype(BF16), preferred_element_type=F32)
            k_only_r = jnp.where(rid == r, k_blk, 0.0).astype(BF16)
            outer = lax.dot_general(k_only_r, v_bf, (((0,), (0,)), ((), ())), preferred_element_type=F32)
            so_ref[r, h] = state * g_h + outer
        ret = intra + cross * g_h
        mu = jnp.mean(ret, axis=-1, keepdims=True)
        rc = ret - mu
        var = jnp.mean(rc * rc, axis=-1, keepdims=True)
        normed = rc * lax.rsqrt(var + EPS) * gn_ref[:, cols]
        o_ref[:, W_A + h * DV_B:W_A + (h + 1) * DV_B] = g_ref[:, cols] * normed


ATTN_SAMPLE_ROWS = 2
SUBLANES = 8
MEM_CHUNK = 256
MEM_LOAD = D_MODEL // MEM_CHUNK


def _attn_mem_kernel(q_ref, ck_ref, cv_ref, x_ref, g_ref, wk_ref, wv_ref,
                     o_ref, k5_ref, k2_ref, v5_ref, v2_ref, wb_ref, *, batch):
    i = pl.program_id(0)
    k_proj, v_load, v_proj = MEM_LOAD, MEM_LOAD + batch, 2 * MEM_LOAD + batch

    def load(w_ref, c):
        wb_ref[pl.ds(pl.multiple_of(c * MEM_CHUNK, MEM_CHUNK), MEM_CHUNK), :] = w_ref[...].astype(BF16)

    def project(o5_ref, o2_ref):
        res = jnp.dot(_rms(x_ref[...], g_ref).astype(BF16), wb_ref[...], preferred_element_type=F32)
        o2_ref[...] = res.astype(BF16)
        for h in range(H_X):
            o5_ref[0, 0, :, h, :] = res[:, h * DH_X:(h + 1) * DH_X]

    pl.when(i < k_proj)(lambda: load(wk_ref, i))
    pl.when((i >= k_proj) & (i < v_load))(lambda: project(k5_ref, k2_ref))
    pl.when((i >= v_load) & (i < v_proj))(lambda: load(wv_ref, i - v_load))
    pl.when((i >= v_proj) & (i < v_proj + batch))(lambda: project(v5_ref, v2_ref))

    for r in range(ATTN_SAMPLE_ROWS):
        row = pl.ds(i * ATTN_SAMPLE_ROWS + r, 1)
        _cache_attn_row(q_ref[row, :], ck_ref[0, r], cv_ref[0, r], o_ref, row)


def _cache_attn_row(q, k_blk, v_blk, o_ref, out_row):
    pairs = N_MEM * H_X
    fold = lambda t: pltpu.roll(t, H_X, 1)
    q8 = jnp.concatenate([q[:, (s % H_X) * DH_X:(s % H_X + 1) * DH_X] for s in range(SUBLANES)], axis=0)
    k3 = k_blk.reshape(pairs // SUBLANES, SUBLANES, DH_X)
    v3 = v_blk.reshape(pairs // SUBLANES, SUBLANES, DH_X)
    s = jnp.sum(k3 * q8[None], axis=-1, keepdims=True) * (DH_X ** -0.5)
    m = jnp.max(s, axis=0, keepdims=True)
    m = jnp.maximum(m, fold(m))
    e = jnp.exp(s - m)
    l = jnp.sum(e, axis=0, keepdims=True)
    acc = jnp.sum(e * v3, axis=0, keepdims=True)
    out = ((acc + fold(acc)) / (l + fold(l)))[0]
    for h in range(H_X):
        o_ref[out_row, h * DH_X:(h + 1) * DH_X] = out[h:h + 1, :]


def _attn_mem(q, ck, cv, mem, g, w_k, w_v, *, batch, n_rows):
    t, d = q.shape
    rb = ATTN_SAMPLE_ROWS
    steps = n_rows // rb
    assert steps * rb == n_rows
    k_proj, v_load, v_proj = MEM_LOAD, MEM_LOAD + batch, 2 * MEM_LOAD + batch
    assert v_proj + batch <= steps

    def prompt(i):
        return jnp.where(i < v_load, jnp.clip(i - k_proj, 0, batch - 1), jnp.clip(i - v_proj, 0, batch - 1))

    def k_idx(i):
        return jnp.clip(i - k_proj, 0, batch - 1)

    def v_idx(i):
        return jnp.clip(i - v_proj, 0, batch - 1)

    cache_spec = pl.BlockSpec((1, rb, N_MEM, H_X, DH_X), lambda i: (0, i, 0, 0, 0))
    out5 = jax.ShapeDtypeStruct((1, batch, N_MEM, H_X, DH_X), F32)
    out2 = jax.ShapeDtypeStruct((batch * N_MEM, d), BF16)
    return pl.pallas_call(
        functools.partial(_attn_mem_kernel, batch=batch),
        grid=(steps,),
        in_specs=[_resident((t, d)), cache_spec, cache_spec,
                  pl.BlockSpec((N_MEM, d), lambda i: (prompt(i), 0)), _resident((1, d)),
                  pl.BlockSpec((MEM_CHUNK, d), lambda i: (jnp.clip(i, 0, MEM_LOAD - 1), 0)),
                  pl.BlockSpec((MEM_CHUNK, d), lambda i: (jnp.clip(i - v_load, 0, MEM_LOAD - 1), 0))],
        out_specs=[_whole_out((n_rows, d)),
                   pl.BlockSpec((1, 1, N_MEM, H_X, DH_X), lambda i: (0, k_idx(i), 0, 0, 0)),
                   pl.BlockSpec((N_MEM, d), lambda i: (k_idx(i), 0)),
                   pl.BlockSpec((1, 1, N_MEM, H_X, DH_X), lambda i: (0, v_idx(i), 0, 0, 0)),
                   pl.BlockSpec((N_MEM, d), lambda i: (v_idx(i), 0))],
        out_shape=[jax.ShapeDtypeStruct((n_rows, d), F32), out5, out2, out5, out2],
        scratch_shapes=[pltpu.VMEM((d, d), BF16)],
        compiler_params=_params(("arbitrary",)),
        name="attn_mem",
    )(q, ck, cv, mem, g, w_k, w_v)


def _rope_tables(pos):
    half = DK_B // 2
    freqs = np.float64(ROPE_THETA) ** (-np.arange(half, dtype=np.float64) / half)
    ang = np.asarray(pos, np.float64)[:, None] * freqs[None, :]
    cos = np.cos(ang)
    sin = np.sin(ang)
    return (jnp.asarray(np.concatenate([cos, cos], axis=-1), F32),
            jnp.asarray(np.concatenate([-sin, sin], axis=-1), F32))


def kernel(x_prompt, x_sample, mem_prompt, cache_mem_k, cache_mem_v, state_ret, norm1_g, w_in, sgu_norm_g, sgu_w_s, sgu_b, ret_gn_g, w_out, norm2_g, mem_norm_g, w_cq, w_ck, w_cv, w_co, norm3_g, w_ff1, w_ff2, final_norm_g):
    batch, seq, d = x_prompt.shape
    ts = x_sample.shape[0]
    n_chunks = seq // CHUNK
    tp = batch * seq

    g1 = norm1_g[0][None, :]
    g2 = norm2_g[0][None, :]
    g3 = norm3_g[0][None, :]
    gm = mem_norm_g[0][None, :]
    gf = final_norm_g[None, :]
    lng = sgu_norm_g[0][None, :]
    gn = ret_gn_g[0][None, :]
    ws = sgu_w_s[0]
    sb = sgu_b[0]
    xp = x_prompt.reshape(tp, d)
    xs = x_sample.reshape(ts, d)

    rope_p = _rope_tables(np.arange(seq))
    rope_s = _rope_tables(np.full((ts,), PAST_LEN))
    u_p, xn_p, u_s, xn_s = _in_proj_u(xp, xs, g1, w_in[0], tm=TM_IN_PROJ)
    rest_p, rest_s, v_rows = _in_proj_rest(xn_p, xn_s, w_in[0], rope_p, rope_s, lng, tm=TM_IN_PROJ)
    w0 = jnp.repeat(ws[:, 0, 0], CH_A)[None, :]
    b0 = jnp.repeat(sb[:, 0], CH_A)[None, :]
    mix_p, state_p, mix_s, state_s = _mixers(
        u_p.reshape(batch, seq, SEC), rest_p.reshape(batch, seq, (N_SEC - 1) * SEC), u_s, rest_s,
        ws, sb.T, gn, w0, b0, state_ret[0], n_chunks=n_chunks)
    mix_p = mix_p.reshape(tp, W_A + W_B)

    mem = mem_prompt.reshape(batch * N_MEM, d)
    h1_p, q_p, h1_s, q_s = _out_q(mix_p, xp, mix_s, xs, w_out[0], w_cq[0], g2, tm=TM_RESIDENT)
    att_head, mk5, mk, mv5, mv = _attn_mem(q_s, cache_mem_k, cache_mem_v, mem, gm, w_ck[0], w_cv[0],
                                           batch=batch, n_rows=ts - tp // TM_RESIDENT)
    h2_p, xn3_p, h2_s, xn3_s = _attn_co(q_p, mk, mv, h1_p, att_head, h1_s, q_s, cache_mem_k, cache_mem_v,
                                        w_co[0], g3, tm=TM_RESIDENT, tiles_per_batch=seq // TM_RESIDENT)

    hid_p, hid_s = _ff1(xn3_p, xn3_s, w_ff1[0], tm=TM_MLP, tn=TN_MLP)
    h3_p, part_s = _ff2(hid_p, hid_s, w_ff2[0], h2_p, tm=TM_MLP, tn=TN_MLP, tk=TK_MLP)
    y_p = _final_norm(h3_p, gf, tm=TM_NORM)
    y_s = _add_norm_sample(part_s, h2_s, gf)

    return (y_p.reshape(batch, seq, d),
            y_s,
            mk5,
            mv5,
            state_p[None],
            state_s[None],
            v_rows[None])
```

```python
import functools
import math

import jax
import jax.numpy as jnp
import numpy as np
from jax import lax
from jax.experimental import pallas as pl
from jax.experimental.pallas import tpu as pltpu

F32 = jnp.float32
BF16 = jnp.bfloat16

D_MODEL = 2048
H_A = 8
CH_A = 128
W_A = H_A * CH_A
CHUNK = 128
H_B = 8
DK_B = 128
DV_B = 128
W_B = H_B * DV_B
ROPE_THETA = 10000.0
N_MEM = 256
H_X = 4
DH_X = D_MODEL // H_X
EPS = 1e-6
PAST_LEN = 16384

SEC = 1024
N_SEC = 6
LOG_G = tuple(math.log(1.0 - 2.0 ** (-5.0 - h)) for h in range(H_B))

V7X_VMEM_BYTES = 64 * 1024 * 1024
COMPILER_SCRATCH_BYTES = 4 * 1024 * 1024
VMEM_LIMIT = V7X_VMEM_BYTES - COMPILER_SCRATCH_BYTES
ROW_CHUNK = 256

TM_IN_PROJ = 1024
TM_RESIDENT = 512
TM_MLP, TN_MLP, TK_MLP = 2048, 1024, 1024
TM_NORM = 1024


def _params(sem):
    return pltpu.CompilerParams(dimension_semantics=sem, vmem_limit_bytes=VMEM_LIMIT)


def _resident(shape):
    zeros = (0,) * len(shape)
    return pl.BlockSpec(shape, lambda *_: zeros, pipeline_mode=pl.Buffered(1))


def _whole_out(shape):
    zeros = (0,) * len(shape)
    return pl.BlockSpec(shape, lambda *_: zeros)


def _row_loop(n_rows, fn):
    rc = min(n_rows, ROW_CHUNK)
    assert n_rows % rc == 0

    def body(c, carry):
        fn(pl.ds(pl.multiple_of(c * rc, rc), rc))
        return carry

    lax.fori_loop(0, n_rows // rc, body, 0)


def _rms(h, g_ref):
    ms = jnp.mean(h * h, axis=-1, keepdims=True)
    return h * lax.rsqrt(ms + EPS) * g_ref[...]


def _rms_rows(x_ref, g_ref, xn_ref, n_rows):
    def fn(rows):
        xn_ref[rows, :] = _rms(x_ref[rows, :], g_ref).astype(BF16)

    _row_loop(n_rows, fn)


def _cast_rows(w_ref, wb_ref):
    def fn(rows):
        wb_ref[rows, :] = w_ref[rows, :].astype(BF16)

    _row_loop(w_ref.shape[0], fn)


def _inproj_u_kernel(x_ref, g_ref, w_ref, xs_ref, u_ref, xn_ref, us_ref, xsn_ref, wb_ref, *, tm, ts):
    @pl.when(pl.program_id(0) == 0)
    def _():
        _cast_rows(w_ref, wb_ref)
        _rms_rows(xs_ref, g_ref, xsn_ref, ts)
        us_ref[...] = jax.nn.gelu(jnp.dot(xsn_ref[...], wb_ref[...], preferred_element_type=F32))

    _rms_rows(x_ref, g_ref, xn_ref, tm)
    u_ref[...] = jax.nn.gelu(jnp.dot(xn_ref[...], wb_ref[...], preferred_element_type=F32)).astype(BF16)


def _in_proj_u(x, xs, g, w, *, tm):
    t, d = x.shape
    ts = xs.shape[0]
    return pl.pallas_call(
        functools.partial(_inproj_u_kernel, tm=tm, ts=ts),
        grid=(t // tm,),
        in_specs=[pl.BlockSpec((tm, d), lambda i: (i, 0)), _resident((1, d)),
                  pl.BlockSpec((d, SEC), lambda i: (0, 0), pipeline_mode=pl.Buffered(1)),
                  _resident((ts, d))],
        out_specs=[pl.BlockSpec((tm, SEC), lambda i: (i, 0)), pl.BlockSpec((tm, d), lambda i: (i, 0)),
                   _whole_out((ts, SEC)), _whole_out((ts, d))],
        out_shape=[jax.ShapeDtypeStruct((t, SEC), BF16), jax.ShapeDtypeStruct((t, d), BF16),
                   jax.ShapeDtypeStruct((ts, SEC), F32), jax.ShapeDtypeStruct((ts, d), BF16)],
        scratch_shapes=[pltpu.VMEM((d, SEC), BF16)],
        compiler_params=_params(("arbitrary",)),
        name="in_proj_u",
    )(x, g, w, xs)


def _inproj_rest_kernel(xn_ref, w_ref, cc_ref, ss_ref, lng_ref, xsn_ref, ccs_ref, sss_ref,
                        o_ref, os_ref, vrows_ref, wb_ref):
    j = pl.program_id(0)
    i = pl.program_id(1)

    @pl.when(i == 0)
    def _():
        _cast_rows(w_ref, wb_ref)

    def layernorm(acc, cc, ss):
        z = jax.nn.gelu(acc)
        mu = jnp.mean(z, axis=-1, keepdims=True)
        zc = z - mu
        var = jnp.mean(zc * zc, axis=-1, keepdims=True)
        return zc * lax.rsqrt(var + EPS) * lng_ref[...]

    def rotary(acc, cc, ss):
        scale = jnp.where(j == 2, DK_B ** -0.5, 1.0).astype(F32)
        heads = []
        for h in range(H_B):
            blk = acc[:, h * DK_B:(h + 1) * DK_B]
            rot = pltpu.roll(blk, DK_B // 2, 1)
            heads.append((blk * cc + rot * ss) * scale)
        return jnp.concatenate(heads, axis=-1)

    def section(pred, epilogue):
        @pl.when(pred)
        def _():
            acc = jnp.dot(xn_ref[...], wb_ref[...], preferred_element_type=F32)
            o_ref[...] = epilogue(acc, cc_ref[...], ss_ref[...]).astype(o_ref.dtype)

            @pl.when(i == 0)
            def _():
                acc_s = jnp.dot(xsn_ref[...], wb_ref[...], preferred_element_type=F32)
                os_ref[j] = epilogue(acc_s, ccs_ref[...], sss_ref[...])

    section(j == 0, layernorm)
    section((j == 1) | (j == 2), rotary)
    section(j == 3, lambda acc, cc, ss: acc)
    section(j == 4, lambda acc, cc, ss: jax.nn.silu(acc))

    @pl.when((j == 0) & (i == 0))
    def _():
        for h in range(H_A):
            vrows_ref[:, 0, h, :] = os_ref[0, :, h * CH_A:(h + 1) * CH_A]


def _in_proj_rest(xn, xsn, w, rope_p, rope_s, lng, *, tm):
    t, d = xn.shape
    ts = xsn.shape[0]
    n_rest = N_SEC - 1
    rope_blocks = rope_p[0].shape[0] // tm
    rope_spec = pl.BlockSpec((tm, DK_B), lambda j, i: (i % rope_blocks, 0))
    return pl.pallas_call(
        _inproj_rest_kernel,
        grid=(n_rest, t // tm),
        in_specs=[pl.BlockSpec((tm, d), lambda j, i: (i, 0)),
                  pl.BlockSpec((d, SEC), lambda j, i: (0, j + 1)),
                  rope_spec, rope_spec, _resident((1, SEC)),
                  _resident((ts, d)), _resident((ts, DK_B)), _resident((ts, DK_B))],
        out_specs=[pl.BlockSpec((tm, SEC), lambda j, i: (i, j)), _whole_out((n_rest, ts, SEC)),
                   _whole_out((ts, 1, H_A, CH_A))],
        out_shape=[jax.ShapeDtypeStruct((t, n_rest * SEC), BF16),
                   jax.ShapeDtypeStruct((n_rest, ts, SEC), F32),
                   jax.ShapeDtypeStruct((ts, 1, H_A, CH_A), F32)],
        scratch_shapes=[pltpu.VMEM((d, SEC), BF16)],
        compiler_params=_params(("arbitrary", "arbitrary")),
        name="in_proj_rest",
    )(xn, w, rope_p[0], rope_p[1], lng, xsn, rope_s[0], rope_s[1])


def _relu2(acc):
    return jnp.square(jnp.maximum(acc, 0.0))


def _ff1_kernel(x_ref, w_ref, xs_ref, o_ref, os_ref):
    o_ref[...] = _relu2(jnp.dot(x_ref[...], w_ref[...].astype(BF16), preferred_element_type=F32)).astype(BF16)

    @pl.when(pl.program_id(0) == 0)
    def _():
        os_ref[pl.program_id(1)] = _relu2(jnp.dot(xs_ref[...], w_ref[...].astype(BF16),
                                                  preferred_element_type=F32)).astype(BF16)


def _ff1(x, xs, w, *, tm, tn):
    t, k = x.shape
    ts = xs.shape[0]
    n = w.shape[1]
    return pl.pallas_call(
        _ff1_kernel,
        grid=(t // tm, n // tn),
        in_specs=[pl.BlockSpec((tm, k), lambda i, j: (i, 0)),
                  pl.BlockSpec((k, tn), lambda i, j: (0, j)),
                  _resident((ts, k))],
        out_specs=[pl.BlockSpec((tm, tn), lambda i, j: (i, j)), _whole_out((n // tn, ts, tn))],
        out_shape=[jax.ShapeDtypeStruct((t, n), BF16), jax.ShapeDtypeStruct((n // tn, ts, tn), BF16)],
        compiler_params=_params(("arbitrary", "arbitrary")),
        name="ff1",
    )(x, w, xs)


def _ff2_kernel(x_ref, w_ref, r_ref, xs_ref, o_ref, os_ref):
    i, j, k = pl.program_id(0), pl.program_id(1), pl.program_id(2)

    @pl.when(k == 0)
    def _():
        o_ref[...] = r_ref[...]

    o_ref[...] += jnp.dot(x_ref[...], w_ref[...].astype(BF16), preferred_element_type=F32)

    @pl.when(i == 0)
    def _():
        part = jnp.dot(xs_ref[k], w_ref[...].astype(BF16), preferred_element_type=F32)

        @pl.when(k == 0)
        def _():
            os_ref[j] = part

        @pl.when(k > 0)
        def _():
            os_ref[j] += part


def _ff2(x, xs, w, r, *, tm, tn, tk):
    t, k = x.shape
    ts = xs.shape[1]
    n = w.shape[1]
    assert xs.shape == (k // tk, ts, tk), "sample tiles must be stacked in contraction slices of tk"
    return pl.pallas_call(
        _ff2_kernel,
        grid=(t // tm, n // tn, k // tk),
        in_specs=[pl.BlockSpec((tm, tk), lambda i, j, kk: (i, kk)),
                  pl.BlockSpec((tk, tn), lambda i, j, kk: (kk, j)),
                  pl.BlockSpec((tm, tn), lambda i, j, kk: (i, j)),
                  _resident((k // tk, ts, tk))],
        out_specs=[pl.BlockSpec((tm, tn), lambda i, j, kk: (i, j)), _whole_out((n // tn, ts, tn))],
        out_shape=[jax.ShapeDtypeStruct((t, n), F32), jax.ShapeDtypeStruct((n // tn, ts, tn), F32)],
        compiler_params=_params(("arbitrary", "arbitrary", "arbitrary")),
        name="ff2",
    )(x, w, r, xs)


def _final_norm_kernel(x_ref, g_ref, o_ref, *, tm):
    def fn(rows):
        o_ref[rows, :] = _rms(x_ref[rows, :], g_ref)

    _row_loop(tm, fn)


def _final_norm(x, g, *, tm):
    t, d = x.shape
    row = pl.BlockSpec((tm, d), lambda i: (i, 0))
    return pl.pallas_call(
        functools.partial(_final_norm_kernel, tm=tm),
        grid=(t // tm,),
        in_specs=[row, _resident((1, d))],
        out_specs=row,
        out_shape=jax.ShapeDtypeStruct((t, d), F32),
        compiler_params=_params(("arbitrary",)),
        name="final_norm",
    )(x, g)


def _add_norm_sample_kernel(x_ref, r_ref, g_ref, o_ref):
    y = r_ref[...] + jnp.concatenate([x_ref[j] for j in range(x_ref.shape[0])], axis=-1)
    o_ref[:, 0, :] = _rms(y, g_ref)


def _add_norm_sample(x, r, g):
    ts, d = r.shape
    return pl.pallas_call(
        _add_norm_sample_kernel,
        grid=(1,),
        in_specs=[_resident(x.shape), _resident((ts, d)), _resident((1, d))],
        out_specs=_whole_out((ts, 1, d)),
        out_shape=jax.ShapeDtypeStruct((ts, 1, d), F32),
        compiler_params=_params(("arbitrary",)),
        name="final_norm_sample",
    )(x, r, g)


W_CHUNK = 128
N_LOAD = D_MODEL // W_CHUNK


def _load_weight_chunk(s, pairs):
    rows = pl.ds(pl.multiple_of(s * W_CHUNK, W_CHUNK), W_CHUNK)
    for src, dst in pairs:
        dst[rows, :] = src[...].astype(BF16)


def _w_chunk_spec():
    return pl.BlockSpec((W_CHUNK, D_MODEL), lambda s: (jnp.minimum(s, N_LOAD - 1), 0))


def _tile_spec(tm, width=D_MODEL):
    return pl.BlockSpec((tm, width), lambda s: (jnp.maximum(s - N_LOAD, 0), 0))


def _outq_kernel(mix_ref, x_ref, wo_ref, wq_ref, g2_ref, mixs_ref, xs_ref, ck_ref, cv_ref,
                 h1_ref, q_ref, h1s_ref, qs_ref, atts_ref, wo_bf, wq_bf, *, first_row):
    s = pl.program_id(0)

    def block(mix_r, x_r, h1_r, q_r):
        h1 = x_r[...] + jnp.dot(mix_r[...].astype(BF16), wo_bf[...], preferred_element_type=F32)
        h1_r[...] = h1
        q_r[...] = jnp.dot(_rms(h1, g2_ref).astype(BF16), wq_bf[...],
                           preferred_element_type=F32).astype(q_r.dtype)

    @pl.when(s < N_LOAD)
    def _():
        _load_weight_chunk(s, ((wo_ref, wo_bf), (wq_ref, wq_bf)))

    @pl.when(s == N_LOAD)
    def _():
        block(mixs_ref, xs_ref, h1s_ref, qs_ref)

    @pl.when(s >= N_LOAD)
    def _():
        tile = s - N_LOAD
        _cache_attn_row(qs_ref[pl.ds(first_row + tile, 1), :], ck_ref[0, 0], cv_ref[0, 0], atts_ref, pl.ds(tile, 1))
        block(mix_ref, x_ref, h1_ref, q_ref)


def _out_q(mix, x, mix_s, xs, ck, cv, w_out, w_cq, g2, *, tm, first_row):
    t, d = x.shape
    ts = xs.shape[0]
    n_tiles = t // tm
    cache_spec = pl.BlockSpec((1, 1, N_MEM, H_X, DH_X),
                              lambda s: (0, first_row + jnp.maximum(s - N_LOAD, 0), 0, 0, 0))
    return pl.pallas_call(
        functools.partial(_outq_kernel, first_row=first_row),
        grid=(N_LOAD + n_tiles,),
        in_specs=[_tile_spec(tm), _tile_spec(tm), _w_chunk_spec(), _w_chunk_spec(), _resident((1, d)),
                  _resident((ts, d)), _resident((ts, d)), cache_spec, cache_spec],
        out_specs=[_tile_spec(tm), _tile_spec(tm), _whole_out((ts, d)), _whole_out((ts, d)),
                   _whole_out((n_tiles, d))],
        out_shape=[jax.ShapeDtypeStruct((t, d), F32), jax.ShapeDtypeStruct((t, d), BF16),
                   jax.ShapeDtypeStruct((ts, d), F32), jax.ShapeDtypeStruct((ts, d), F32),
                   jax.ShapeDtypeStruct((n_tiles, d), F32)],
        scratch_shapes=[pltpu.VMEM((d, d), BF16), pltpu.VMEM((d, d), BF16)],
        compiler_params=_params(("arbitrary",)),
        name="out_q",
    )(mix, x, w_out, w_cq, g2, mix_s, xs, ck, cv)


def _softmax_rows(s):
    m = jnp.max(s, axis=-1, keepdims=True)
    e = jnp.exp(s - m)
    return e / jnp.sum(e, axis=-1, keepdims=True)


def _attn_co_kernel(q_ref, mk_ref, mv_ref, h1_ref, wc_ref, g3_ref, atts_ref, h1s_ref, qs_ref, ck_ref, cv_ref,
                    h2_ref, xn_ref, h2s_ref, xns_ref, wc_bf, att_ref, tail_ref, *, n_tiles):
    s = pl.program_id(0)
    n_head = atts_ref.shape[0]

    def co_block(att, h1_r, h2_r, xn_r):
        h2 = h1_r[...] + jnp.dot(att.astype(BF16), wc_bf[...], preferred_element_type=F32)
        h2_r[...] = h2
        xn_r[...] = _rms(h2, g3_ref).astype(BF16)

    @pl.when(s < N_LOAD)
    def _():
        _load_weight_chunk(s, ((wc_ref, wc_bf),))

    @pl.when(s >= N_LOAD)
    def _():
        tile = s - N_LOAD
        _cache_attn_row(qs_ref[pl.ds(n_head + tile, 1), :], ck_ref[0, 0], cv_ref[0, 0], tail_ref, pl.ds(tile, 1))
        for h in range(H_X):
            cols = slice(h * DH_X, (h + 1) * DH_X)
            sc = lax.dot_general(q_ref[:, cols], mk_ref[:, cols], (((1,), (1,)), ((), ())),
                                 preferred_element_type=F32) * (DH_X ** -0.5)
            p = _softmax_rows(sc)
            att_ref[:, cols] = jnp.dot(p.astype(BF16), mv_ref[:, cols],
                                       preferred_element_type=F32).astype(BF16)
        co_block(att_ref[...], h1_ref, h2_ref, xn_ref)

    @pl.when(s == N_LOAD + n_tiles - 1)
    def _():
        co_block(jnp.concatenate([atts_ref[...], tail_ref[...]], axis=0), h1s_ref, h2s_ref, xns_ref)


def _attn_co(q, mk, mv, h1, att_head, h1_s, q_s, ck, cv, w_co, g3, *, tm, tiles_per_batch):
    t, d = h1.shape
    ts = h1_s.shape[0]
    n_tiles = t // tm
    n_head = att_head.shape[0]
    assert n_head + n_tiles == ts
    mem_spec = pl.BlockSpec((N_MEM, d), lambda s: (jnp.maximum(s - N_LOAD, 0) // tiles_per_batch, 0))
    cache_spec = pl.BlockSpec((1, 1, N_MEM, H_X, DH_X),
                              lambda s: (0, n_head + jnp.maximum(s - N_LOAD, 0), 0, 0, 0))
    return pl.pallas_call(
        functools.partial(_attn_co_kernel, n_tiles=n_tiles),
        grid=(N_LOAD + n_tiles,),
        in_specs=[_tile_spec(tm), mem_spec, mem_spec, _tile_spec(tm), _w_chunk_spec(), _resident((1, d)),
                  _resident((n_head, d)), _resident((ts, d)), _resident((ts, d)), cache_spec, cache_spec],
        out_specs=[_tile_spec(tm), _tile_spec(tm), _whole_out((ts, d)), _whole_out((ts, d))],
        out_shape=[jax.ShapeDtypeStruct((t, d), F32), jax.ShapeDtypeStruct((t, d), BF16),
                   jax.ShapeDtypeStruct((ts, d), F32), jax.ShapeDtypeStruct((ts, d), BF16)],
        scratch_shapes=[pltpu.VMEM((d, d), BF16), pltpu.VMEM((tm, d), BF16), pltpu.VMEM((n_tiles, d), F32)],
        compiler_params=_params(("arbitrary",)),
        name="attn_co",
    )(q, mk, mv, h1, w_co, g3, att_head, h1_s, q_s, ck, cv)


def _mix_prompt_kernel(u_ref, v_ref, q_ref, k_ref, vb_ref, g_ref, ws_ref, bt_ref, gn_ref,
                       us_ref, vs_ref, qs_ref, ks_ref, vbs_ref, gs_ref, w0_ref, b0_ref, ss_ref,
                       o_ref, so_ref, os_ref, sso_ref,
                       st_ref, wt_ref, bias_ref, dm_ref, qd_ref, kd_ref, *, batch, n_chunks):
    n = pl.program_id(0)
    _mix_sample_rows(us_ref, vs_ref, qs_ref, ks_ref, vbs_ref, gs_ref, w0_ref, b0_ref, gn_ref, ss_ref,
                     os_ref, sso_ref)

    @pl.when(n == 0)
    def _():
        ii = lax.broadcasted_iota(jnp.int32, (CHUNK, CHUNK), 0)
        jj = lax.broadcasted_iota(jnp.int32, (CHUNK, CHUNK), 1)
        causal = ii >= jj
        diff = jnp.maximum((ii - jj).astype(F32), 0.0)
        ridx = ii.astype(F32)
        for h in range(H_A):
            lg = LOG_G[h]
            wt_ref[h] = (ws_ref[h] * causal.astype(F32)).astype(BF16)
            bias_ref[h] = jnp.broadcast_to(bt_ref[:, h:h + 1], (CHUNK, CH_A))
            dm_ref[h] = jnp.where(causal, jnp.exp(lg * diff), 0.0)
            qd_ref[h] = jnp.exp(lg * (ridx + 1.0))
            kd_ref[h] = jnp.exp(lg * (CHUNK - 1.0 - ridx))
        st_ref[...] = jnp.zeros_like(st_ref)

    for b in range(batch):
        for h in range(H_A):
            cols = slice(h * CH_A, (h + 1) * CH_A)
            mixed = jnp.dot(wt_ref[h], v_ref[b, :, cols], preferred_element_type=F32) + bias_ref[h]
            o_ref[b, :, cols] = (u_ref[b, :, cols].astype(F32) * mixed).astype(BF16)

            kh = k_ref[b, :, cols]
            vh = vb_ref[b, :, cols]
            state = st_ref[b, h]
            state_bf = state.astype(BF16)
            qh = q_ref[b, :, cols]
            scores = lax.dot_general(qh, kh, (((1,), (1,)), ((), ())), preferred_element_type=F32) * dm_ref[h]
            intra = jnp.dot(scores.astype(BF16), vh, preferred_element_type=F32)
            cross = jnp.dot(qh, state_bf, preferred_element_type=F32) * qd_ref[h]
            ret = intra + cross
            mu = jnp.mean(ret, axis=-1, keepdims=True)
            rc = ret - mu
            var = jnp.mean(rc * rc, axis=-1, keepdims=True)
            normed = rc * lax.rsqrt(var + EPS) * gn_ref[:, cols]
            o_ref[b, :, W_A + h * DV_B:W_A + (h + 1) * DV_B] = (
                g_ref[b, :, cols].astype(F32) * normed).astype(BF16)
            kd = (kh.astype(F32) * kd_ref[h]).astype(BF16)
            st_ref[b, h] = state * math.exp(LOG_G[h] * CHUNK) + lax.dot_general(
                kd, vh, (((0,), (0,)), ((), ())), preferred_element_type=F32)

    @pl.when(n == n_chunks - 1)
    def _():
        so_ref[...] = st_ref[...]


def _mixers(u, rest, u_s, rest_s, ws, bt, gn, w0, b0, state_s, *, n_chunks):
    batch, seq, _ = u.shape
    ts = u_s.shape[0]
    rb = ts // n_chunks
    assert rb * n_chunks == ts and rb % 8 == 0

    def sec(s):
        return pl.BlockSpec((batch, CHUNK, SEC), lambda n, s=s: (0, n, s))

    def sec_s(s):
        return pl.BlockSpec((None, rb, SEC), lambda n, s=s: (s, n, 0))

    state_spec = pl.BlockSpec((rb, H_B, DK_B, DV_B), lambda n: (n, 0, 0, 0))
    table = pltpu.VMEM((H_B, CHUNK, CHUNK), F32)
    return pl.pallas_call(
        functools.partial(_mix_prompt_kernel, batch=batch, n_chunks=n_chunks),
        grid=(n_chunks,),
        in_specs=[sec(0), sec(0), sec(1), sec(2), sec(3), sec(4),
                  _resident((H_A, CHUNK, CHUNK)), _resident((CHUNK, H_A)), _resident((1, W_B)),
                  pl.BlockSpec((rb, SEC), lambda n: (n, 0)), sec_s(0), sec_s(1), sec_s(2), sec_s(3), sec_s(4),
                  _resident((1, W_A)), _resident((1, W_A)), state_spec],
        out_specs=[pl.BlockSpec((batch, CHUNK, W_A + W_B), lambda n: (0, n, 0)),
                   _whole_out((batch, H_B, DK_B, DV_B)),
                   pl.BlockSpec((rb, W_A + W_B), lambda n: (n, 0)), state_spec],
        out_shape=[jax.ShapeDtypeStruct((batch, seq, W_A + W_B), BF16),
                   jax.ShapeDtypeStruct((batch, H_B, DK_B, DV_B), F32),
                   jax.ShapeDtypeStruct((ts, W_A + W_B), F32),
                   jax.ShapeDtypeStruct(state_s.shape, F32)],
        scratch_shapes=[pltpu.VMEM((batch, H_B, DK_B, DV_B), F32), pltpu.VMEM((H_A, CHUNK, CHUNK), BF16),
                        table, table, table, table],
        compiler_params=_params(("arbitrary",)),
        name="mixers",
    )(u, rest, rest, rest, rest, rest, ws, bt, gn,
      u_s, rest_s, rest_s, rest_s, rest_s, rest_s, w0, b0, state_s)


def _mix_sample_rows(u_ref, v_ref, q_ref, k_ref, vb_ref, g_ref, w0_ref, b0_ref, gn_ref, s_ref,
                     o_ref, so_ref):
    n_rows = u_ref.shape[0]
    rid = lax.broadcasted_iota(jnp.int32, (n_rows, 1), 0)
    o_ref[:, :W_A] = u_ref[...] * (w0_ref[...] * v_ref[...] + b0_ref[...])

    for h in range(H_B):
        cols = slice(h * DK_B, (h + 1) * DK_B)
        g_h = math.exp(LOG_G[h])
        q_blk = q_ref[:, cols]
        k_blk = k_ref[:, cols]
        v_blk = vb_ref[:, cols]
        v_bf = v_blk.astype(BF16)
        intra = jnp.sum(q_blk * k_blk, axis=-1, keepdims=True) * v_blk
        cross = jnp.zeros((n_rows, DV_B), F32)
        for r in range(n_rows):
            state = s_ref[r, h]
            q_only_r = jnp.where(rid == r, q_blk, 0.0).astype(BF16)
            cross = cross + jnp.dot(q_only_r, state.astype(BF16), preferred_element_type=F32)
            k_only_r = jnp.where(rid == r, k_blk, 0.0).astype(BF16)
            outer = lax.dot_general(k_only_r, v_bf, (((0,), (0,)), ((), ())), preferred_element_type=F32)
            so_ref[r, h] = state * g_h + outer
        ret = intra + cross * g_h
        mu = jnp.mean(ret, axis=-1, keepdims=True)
        rc = ret - mu
        var = jnp.mean(rc * rc, axis=-1, keepdims=True)
        normed = rc * lax.rsqrt(var + EPS) * gn_ref[:, cols]
        o_ref[:, W_A + h * DV_B:W_A + (h + 1) * DV_B] = g_ref[:, cols] * normed


ATTN_SAMPLE_ROWS = 2
SUBLANES = 8
MEM_CHUNK = 256
MEM_LOAD = D_MODEL // MEM_CHUNK


def _attn_mem_kernel(q_ref, ck_ref, cv_ref, x_ref, g_ref, wk_ref, wv_ref,
                     o_ref, k5_ref, k2_ref, v5_ref, v2_ref, wb_ref, *, batch):
    i = pl.program_id(0)
    k_proj, v_load, v_proj = MEM_LOAD, MEM_LOAD + batch, 2 * MEM_LOAD + batch

    def load(w_ref, c):
        wb_ref[pl.ds(pl.multiple_of(c * MEM_CHUNK, MEM_CHUNK), MEM_CHUNK), :] = w_ref[...].astype(BF16)

    def project(o5_ref, o2_ref):
        res = jnp.dot(_rms(x_ref[...], g_ref).astype(BF16), wb_ref[...], preferred_element_type=F32)
        o2_ref[...] = res.astype(BF16)
        for h in range(H_X):
            o5_ref[0, 0, :, h, :] = res[:, h * DH_X:(h + 1) * DH_X]

    pl.when(i < k_proj)(lambda: load(wk_ref, i))
    pl.when((i >= k_proj) & (i < v_load))(lambda: project(k5_ref, k2_ref))
    pl.when((i >= v_load) & (i < v_proj))(lambda: load(wv_ref, i - v_load))
    pl.when((i >= v_proj) & (i < v_proj + batch))(lambda: project(v5_ref, v2_ref))

    for r in range(ATTN_SAMPLE_ROWS):
        row = pl.ds(i * ATTN_SAMPLE_ROWS + r, 1)
        _cache_attn_row(q_ref[row, :], ck_ref[0, r], cv_ref[0, r], o_ref, row)


def _cache_attn_row(q, k_blk, v_blk, o_ref, out_row):
    pairs = N_MEM * H_X
    fold = lambda t: pltpu.roll(t, H_X, 1)
    q8 = jnp.concatenate([q[:, (s % H_X) * DH_X:(s % H_X + 1) * DH_X] for s in range(SUBLANES)], axis=0)
    k3 = k_blk.reshape(pairs // SUBLANES, SUBLANES, DH_X)
    v3 = v_blk.reshape(pairs // SUBLANES, SUBLANES, DH_X)
    s = jnp.sum(k3 * q8[None], axis=-1, keepdims=True) * (DH_X ** -0.5)
    m = jnp.max(s, axis=0, keepdims=True)
    m = jnp.maximum(m, fold(m))
    e = jnp.exp(s - m)
    l = jnp.sum(e, axis=0, keepdims=True)
    acc = jnp.sum(e * v3, axis=0, keepdims=True)
    out = ((acc + fold(acc)) / (l + fold(l)))[0]
    for h in range(H_X):
        o_ref[out_row, h * DH_X:(h + 1) * DH_X] = out[h:h + 1, :]


def _attn_mem(q, ck, cv, mem, g, w_k, w_v, *, batch, n_rows):
    t, d = q.shape
    rb = ATTN_SAMPLE_ROWS
    steps = n_rows // rb
    assert steps * rb == n_rows
    k_proj, v_load, v_proj = MEM_LOAD, MEM_LOAD + batch, 2 * MEM_LOAD + batch
    assert v_proj + batch <= steps

    def prompt(i):
        return jnp.where(i < v_load, jnp.clip(i - k_proj, 0, batch - 1), jnp.clip(i - v_proj, 0, batch - 1))

    def k_idx(i):
        return jnp.clip(i - k_proj, 0, batch - 1)

    def v_idx(i):
        return jnp.clip(i - v_proj, 0, batch - 1)

    cache_spec = pl.BlockSpec((1, rb, N_MEM, H_X, DH_X), lambda i: (0, i, 0, 0, 0))
    out5 = jax.ShapeDtypeStruct((1, batch, N_MEM, H_X, DH_X), F32)
    out2 = jax.ShapeDtypeStruct((batch * N_MEM, d), BF16)
    return pl.pallas_call(
        functools.partial(_attn_mem_kernel, batch=batch),
        grid=(steps,),
        in_specs=[_resident((t, d)), cache_spec, cache_spec,
                  pl.BlockSpec((N_MEM, d), lambda i: (prompt(i), 0)), _resident((1, d)),
                  pl.BlockSpec((MEM_CHUNK, d), lambda i: (jnp.clip(i, 0, MEM_LOAD - 1), 0)),
                  pl.BlockSpec((MEM_CHUNK, d), lambda i: (jnp.clip(i - v_load, 0, MEM_LOAD - 1), 0))],
        out_specs=[_whole_out((n_rows, d)),
                   pl.BlockSpec((1, 1, N_MEM, H_X, DH_X), lambda i: (0, k_idx(i), 0, 0, 0)),
                   pl.BlockSpec((N_MEM, d), lambda i: (k_idx(i), 0)),
                   pl.BlockSpec((1, 1, N_MEM, H_X, DH_X), lambda i: (0, v_idx(i), 0, 0, 0)),
                   pl.BlockSpec((N_MEM, d), lambda i: (v_idx(i), 0))],
        out_shape=[jax.ShapeDtypeStruct((n_rows, d), F32), out5, out2, out5, out2],
        scratch_shapes=[pltpu.VMEM((d, d), BF16)],
        compiler_params=_params(("arbitrary",)),
        name="attn_mem",
    )(q, ck, cv, mem, g, w_k, w_v)


def _rope_tables(pos):
    half = DK_B // 2
    freqs = np.float64(ROPE_THETA) ** (-np.arange(half, dtype=np.float64) / half)
    ang = np.asarray(pos, np.float64)[:, None] * freqs[None, :]
    cos = np.cos(ang)
    sin = np.sin(ang)
    return (jnp.asarray(np.concatenate([cos, cos], axis=-1), F32),
            jnp.asarray(np.concatenate([-sin, sin], axis=-1), F32))


def kernel(x_prompt, x_sample, mem_prompt, cache_mem_k, cache_mem_v, state_ret, norm1_g, w_in, sgu_norm_g, sgu_w_s, sgu_b, ret_gn_g, w_out, norm2_g, mem_norm_g, w_cq, w_ck, w_cv, w_co, norm3_g, w_ff1, w_ff2, final_norm_g):
    batch, seq, d = x_prompt.shape
    ts = x_sample.shape[0]
    n_chunks = seq // CHUNK
    tp = batch * seq

    g1 = norm1_g[0][None, :]
    g2 = norm2_g[0][None, :]
    g3 = norm3_g[0][None, :]
    gm = mem_norm_g[0][None, :]
    gf = final_norm_g[None, :]
    lng = sgu_norm_g[0][None, :]
    gn = ret_gn_g[0][None, :]
    ws = sgu_w_s[0]
    sb = sgu_b[0]
    xp = x_prompt.reshape(tp, d)
    xs = x_sample.reshape(ts, d)

    rope_p = _rope_tables(np.arange(seq))
    rope_s = _rope_tables(np.full((ts,), PAST_LEN))
    u_p, xn_p, u_s, xn_s = _in_proj_u(xp, xs, g1, w_in[0], tm=TM_IN_PROJ)
    rest_p, rest_s, v_rows = _in_proj_rest(xn_p, xn_s, w_in[0], rope_p, rope_s, lng, tm=TM_IN_PROJ)
    w0 = jnp.repeat(ws[:, 0, 0], CH_A)[None, :]
    b0 = jnp.repeat(sb[:, 0], CH_A)[None, :]
    mix_p, state_p, mix_s, state_s = _mixers(
        u_p.reshape(batch, seq, SEC), rest_p.reshape(batch, seq, (N_SEC - 1) * SEC), u_s, rest_s,
        ws, sb.T, gn, w0, b0, state_ret[0], n_chunks=n_chunks)
    mix_p = mix_p.reshape(tp, W_A + W_B)

    mem = mem_prompt.reshape(batch * N_MEM, d)
    n_mem_rows = ts - 2 * (tp // TM_RESIDENT)
    h1_p, q_p, h1_s, q_s, att_mid = _out_q(mix_p, xp, mix_s, xs, cache_mem_k, cache_mem_v, w_out[0], w_cq[0], g2,
                                           tm=TM_RESIDENT, first_row=n_mem_rows)
    att_head, mk5, mk, mv5, mv = _attn_mem(q_s, cache_mem_k, cache_mem_v, mem, gm, w_ck[0], w_cv[0],
                                           batch=batch, n_rows=n_mem_rows)
    att_head = jnp.concatenate([att_head, att_mid], axis=0)
    h2_p, xn3_p, h2_s, xn3_s = _attn_co(q_p, mk, mv, h1_p, att_head, h1_s, q_s, cache_mem_k, cache_mem_v,
                                        w_co[0], g3, tm=TM_RESIDENT, tiles_per_batch=seq // TM_RESIDENT)

    hid_p, hid_s = _ff1(xn3_p, xn3_s, w_ff1[0], tm=TM_MLP, tn=TN_MLP)
    h3_p, part_s = _ff2(hid_p, hid_s, w_ff2[0], h2_p, tm=TM_MLP, tn=TN_MLP, tk=TK_MLP)
    y_p = _final_norm(h3_p, gf, tm=TM_NORM)
    y_s = _add_norm_sample(part_s, h2_s, gf)

    return (y_p.reshape(batch, seq, d),
            y_s,
            mk5,
            mv5,
            state_p[None],
            state_s[None],
            v_rows[None])
```

```python
import functools
import math

import jax
import jax.numpy as jnp
import numpy as np
from jax import lax
from jax.experimental import pallas as pl
from jax.experimental.pallas import tpu as pltpu

F32 = jnp.float32
BF16 = jnp.bfloat16

D_MODEL = 2048
H_A = 8
CH_A = 128
W_A = H_A * CH_A
CHUNK = 128
H_B = 8
DK_B = 128
DV_B = 128
W_B = H_B * DV_B
ROPE_THETA = 10000.0
N_MEM = 256
H_X = 4
DH_X = D_MODEL // H_X
EPS = 1e-6
PAST_LEN = 16384

SEC = 1024
N_SEC = 6
LOG_G = tuple(math.log(1.0 - 2.0 ** (-5.0 - h)) for h in range(H_B))

V7X_VMEM_BYTES = 64 * 1024 * 1024
COMPILER_SCRATCH_BYTES = 4 * 1024 * 1024
VMEM_LIMIT = V7X_VMEM_BYTES - COMPILER_SCRATCH_BYTES
ROW_CHUNK = 256

TM_IN_PROJ = 1024
TM_RESIDENT = 512
TM_MLP, TN_MLP, TK_MLP = 2048, 1024, 1024
TM_NORM = 1024


def _params(sem):
    return pltpu.CompilerParams(dimension_semantics=sem, vmem_limit_bytes=VMEM_LIMIT)


def _resident(shape):
    zeros = (0,) * len(shape)
    return pl.BlockSpec(shape, lambda *_: zeros, pipeline_mode=pl.Buffered(1))


def _whole_out(shape):
    zeros = (0,) * len(shape)
    return pl.BlockSpec(shape, lambda *_: zeros)


def _row_loop(n_rows, fn):
    rc = min(n_rows, ROW_CHUNK)
    assert n_rows % rc == 0

    def body(c, carry):
        fn(pl.ds(pl.multiple_of(c * rc, rc), rc))
        return carry

    lax.fori_loop(0, n_rows // rc, body, 0)


def _rms(h, g_ref):
    ms = jnp.mean(h * h, axis=-1, keepdims=True)
    return h * lax.rsqrt(ms + EPS) * g_ref[...]


def _rms_rows(x_ref, g_ref, xn_ref, n_rows):
    def fn(rows):
        xn_ref[rows, :] = _rms(x_ref[rows, :], g_ref).astype(BF16)

    _row_loop(n_rows, fn)


def _cast_rows(w_ref, wb_ref):
    def fn(rows):
        wb_ref[rows, :] = w_ref[rows, :].astype(BF16)

    _row_loop(w_ref.shape[0], fn)


def _inproj_u_kernel(x_ref, g_ref, w_ref, xs_ref, u_ref, xn_ref, us_ref, xsn_ref, wb_ref, *, tm, ts):
    @pl.when(pl.program_id(0) == 0)
    def _():
        _cast_rows(w_ref, wb_ref)
        _rms_rows(xs_ref, g_ref, xsn_ref, ts)
        us_ref[...] = jax.nn.gelu(jnp.dot(xsn_ref[...], wb_ref[...], preferred_element_type=F32))

    _rms_rows(x_ref, g_ref, xn_ref, tm)
    u_ref[...] = jax.nn.gelu(jnp.dot(xn_ref[...], wb_ref[...], preferred_element_type=F32)).astype(BF16)


def _in_proj_u(x, xs, g, w, *, tm):
    t, d = x.shape
    ts = xs.shape[0]
    return pl.pallas_call(
        functools.partial(_inproj_u_kernel, tm=tm, ts=ts),
        grid=(t // tm,),
        in_specs=[pl.BlockSpec((tm, d), lambda i: (i, 0)), _resident((1, d)),
                  pl.BlockSpec((d, SEC), lambda i: (0, 0), pipeline_mode=pl.Buffered(1)),
                  _resident((ts, d))],
        out_specs=[pl.BlockSpec((tm, SEC), lambda i: (i, 0)), pl.BlockSpec((tm, d), lambda i: (i, 0)),
                   _whole_out((ts, SEC)), _whole_out((ts, d))],
        out_shape=[jax.ShapeDtypeStruct((t, SEC), BF16), jax.ShapeDtypeStruct((t, d), BF16),
                   jax.ShapeDtypeStruct((ts, SEC), F32), jax.ShapeDtypeStruct((ts, d), BF16)],
        scratch_shapes=[pltpu.VMEM((d, SEC), BF16)],
        compiler_params=_params(("arbitrary",)),
        name="in_proj_u",
    )(x, g, w, xs)


def _inproj_rest_kernel(xn_ref, w_ref, cc_ref, ss_ref, lng_ref, xsn_ref, ccs_ref, sss_ref,
                        o_ref, os_ref, vrows_ref, wb_ref):
    j = pl.program_id(0)
    i = pl.program_id(1)

    @pl.when(i == 0)
    def _():
        _cast_rows(w_ref, wb_ref)

    def layernorm(acc, cc, ss):
        z = jax.nn.gelu(acc)
        mu = jnp.mean(z, axis=-1, keepdims=True)
        zc = z - mu
        var = jnp.mean(zc * zc, axis=-1, keepdims=True)
        return zc * lax.rsqrt(var + EPS) * lng_ref[...]

    def rotary(acc, cc, ss):
        scale = jnp.where(j == 2, DK_B ** -0.5, 1.0).astype(F32)
        heads = []
        for h in range(H_B):
            blk = acc[:, h * DK_B:(h + 1) * DK_B]
            rot = pltpu.roll(blk, DK_B // 2, 1)
            heads.append((blk * cc + rot * ss) * scale)
        return jnp.concatenate(heads, axis=-1)

    def section(pred, epilogue):
        @pl.when(pred)
        def _():
            acc = jnp.dot(xn_ref[...], wb_ref[...], preferred_element_type=F32)
            o_ref[...] = epilogue(acc, cc_ref[...], ss_ref[...]).astype(o_ref.dtype)

            @pl.when(i == 0)
            def _():
                acc_s = jnp.dot(xsn_ref[...], wb_ref[...], preferred_element_type=F32)
                os_ref[j] = epilogue(acc_s, ccs_ref[...], sss_ref[...])

    section(j == 0, layernorm)
    section((j == 1) | (j == 2), rotary)
    section(j == 3, lambda acc, cc, ss: acc)
    section(j == 4, lambda acc, cc, ss: jax.nn.silu(acc))

    @pl.when((j == 0) & (i == 0))
    def _():
        for h in range(H_A):
            vrows_ref[:, 0, h, :] = os_ref[0, :, h * CH_A:(h + 1) * CH_A]


def _in_proj_rest(xn, xsn, w, rope_p, rope_s, lng, *, tm):
    t, d = xn.shape
    ts = xsn.shape[0]
    n_rest = N_SEC - 1
    rope_blocks = rope_p[0].shape[0] // tm
    rope_spec = pl.BlockSpec((tm, DK_B), lambda j, i: (i % rope_blocks, 0))
    return pl.pallas_call(
        _inproj_rest_kernel,
        grid=(n_rest, t // tm),
        in_specs=[pl.BlockSpec((tm, d), lambda j, i: (i, 0)),
                  pl.BlockSpec((d, SEC), lambda j, i: (0, j + 1)),
                  rope_spec, rope_spec, _resident((1, SEC)),
                  _resident((ts, d)), _resident((ts, DK_B)), _resident((ts, DK_B))],
        out_specs=[pl.BlockSpec((tm, SEC), lambda j, i: (i, j)), _whole_out((n_rest, ts, SEC)),
                   _whole_out((ts, 1, H_A, CH_A))],
        out_shape=[jax.ShapeDtypeStruct((t, n_rest * SEC), BF16),
                   jax.ShapeDtypeStruct((n_rest, ts, SEC), F32),
                   jax.ShapeDtypeStruct((ts, 1, H_A, CH_A), F32)],
        scratch_shapes=[pltpu.VMEM((d, SEC), BF16)],
        compiler_params=_params(("arbitrary", "arbitrary")),
        name="in_proj_rest",
    )(xn, w, rope_p[0], rope_p[1], lng, xsn, rope_s[0], rope_s[1])


def _relu2(acc):
    return jnp.square(jnp.maximum(acc, 0.0))


def _ff1_kernel(x_ref, w_ref, xs_ref, o_ref, os_ref):
    o_ref[...] = _relu2(jnp.dot(x_ref[...], w_ref[...].astype(BF16), preferred_element_type=F32)).astype(BF16)

    @pl.when(pl.program_id(0) == 0)
    def _():
        os_ref[pl.program_id(1)] = _relu2(jnp.dot(xs_ref[...], w_ref[...].astype(BF16),
                                                  preferred_element_type=F32)).astype(BF16)


def _ff1(x, xs, w, *, tm, tn):
    t, k = x.shape
    ts = xs.shape[0]
    n = w.shape[1]
    return pl.pallas_call(
        _ff1_kernel,
        grid=(t // tm, n // tn),
        in_specs=[pl.BlockSpec((tm, k), lambda i, j: (i, 0)),
                  pl.BlockSpec((k, tn), lambda i, j: (0, j)),
                  _resident((ts, k))],
        out_specs=[pl.BlockSpec((tm, tn), lambda i, j: (i, j)), _whole_out((n // tn, ts, tn))],
        out_shape=[jax.ShapeDtypeStruct((t, n), BF16), jax.ShapeDtypeStruct((n // tn, ts, tn), BF16)],
        compiler_params=_params(("arbitrary", "arbitrary")),
        name="ff1",
    )(x, w, xs)


def _ff2_kernel(x_ref, w_ref, r_ref, xs_ref, o_ref, os_ref):
    i, j, k = pl.program_id(0), pl.program_id(1), pl.program_id(2)

    @pl.when(k == 0)
    def _():
        o_ref[...] = r_ref[...]

    o_ref[...] += jnp.dot(x_ref[...], w_ref[...].astype(BF16), preferred_element_type=F32)

    @pl.when(i == 0)
    def _():
        part = jnp.dot(xs_ref[k], w_ref[...].astype(BF16), preferred_element_type=F32)

        @pl.when(k == 0)
        def _():
            os_ref[j] = part

        @pl.when(k > 0)
        def _():
            os_ref[j] += part


def _ff2(x, xs, w, r, *, tm, tn, tk):
    t, k = x.shape
    ts = xs.shape[1]
    n = w.shape[1]
    assert xs.shape == (k // tk, ts, tk), "sample tiles must be stacked in contraction slices of tk"
    return pl.pallas_call(
        _ff2_kernel,
        grid=(t // tm, n // tn, k // tk),
        in_specs=[pl.BlockSpec((tm, tk), lambda i, j, kk: (i, kk)),
                  pl.BlockSpec((tk, tn), lambda i, j, kk: (kk, j)),
                  pl.BlockSpec((tm, tn), lambda i, j, kk: (i, j)),
                  _resident((k // tk, ts, tk))],
        out_specs=[pl.BlockSpec((tm, tn), lambda i, j, kk: (i, j)), _whole_out((n // tn, ts, tn))],
        out_shape=[jax.ShapeDtypeStruct((t, n), F32), jax.ShapeDtypeStruct((n // tn, ts, tn), F32)],
        compiler_params=_params(("arbitrary", "arbitrary", "arbitrary")),
        name="ff2",
    )(x, w, r, xs)


def _final_norm_kernel(x_ref, g_ref, o_ref, *, tm):
    def fn(rows):
        o_ref[rows, :] = _rms(x_ref[rows, :], g_ref)

    _row_loop(tm, fn)


def _final_norm(x, g, *, tm):
    t, d = x.shape
    row = pl.BlockSpec((tm, d), lambda i: (i, 0))
    return pl.pallas_call(
        functools.partial(_final_norm_kernel, tm=tm),
        grid=(t // tm,),
        in_specs=[row, _resident((1, d))],
        out_specs=row,
        out_shape=jax.ShapeDtypeStruct((t, d), F32),
        compiler_params=_params(("arbitrary",)),
        name="final_norm",
    )(x, g)


def _add_norm_sample_kernel(x_ref, r_ref, g_ref, o_ref):
    y = r_ref[...] + jnp.concatenate([x_ref[j] for j in range(x_ref.shape[0])], axis=-1)
    o_ref[:, 0, :] = _rms(y, g_ref)


def _add_norm_sample(x, r, g):
    ts, d = r.shape
    return pl.pallas_call(
        _add_norm_sample_kernel,
        grid=(1,),
        in_specs=[_resident(x.shape), _resident((ts, d)), _resident((1, d))],
        out_specs=_whole_out((ts, 1, d)),
        out_shape=jax.ShapeDtypeStruct((ts, 1, d), F32),
        compiler_params=_params(("arbitrary",)),
        name="final_norm_sample",
    )(x, r, g)


W_CHUNK = 64
N_LOAD = D_MODEL // W_CHUNK


def _load_weight_chunk(s, pairs):
    rows = pl.ds(pl.multiple_of(s * W_CHUNK, W_CHUNK), W_CHUNK)
    for src, dst in pairs:
        dst[rows, :] = src[...].astype(BF16)


def _w_chunk_spec():
    return pl.BlockSpec((W_CHUNK, D_MODEL), lambda s: (jnp.minimum(s, N_LOAD - 1), 0))


def _tile_spec(tm, width=D_MODEL):
    return pl.BlockSpec((tm, width), lambda s: (jnp.maximum(s - N_LOAD, 0), 0))


def _outq_kernel(mix_ref, x_ref, wo_ref, wq_ref, g2_ref, mixs_ref, xs_ref, ck_ref, cv_ref, wn_ref,
                 h1_ref, q_ref, h1s_ref, qs_ref, atts_ref, wnb_ref, wo_bf, wq_bf, *, first_row):
    s = pl.program_id(0)

    def block(mix_r, x_r, h1_r, q_r):
        h1 = x_r[...] + jnp.dot(mix_r[...].astype(BF16), wo_bf[...], preferred_element_type=F32)
        h1_r[...] = h1
        q_r[...] = jnp.dot(_rms(h1, g2_ref).astype(BF16), wq_bf[...],
                           preferred_element_type=F32).astype(q_r.dtype)

    @pl.when(s < N_LOAD)
    def _():
        _load_weight_chunk(s, ((wo_ref, wo_bf), (wq_ref, wq_bf)))

    @pl.when(s == N_LOAD)
    def _():
        block(mixs_ref, xs_ref, h1s_ref, qs_ref)

    @pl.when(s >= N_LOAD)
    def _():
        tile = s - N_LOAD
        _cache_attn_row(qs_ref[pl.ds(first_row + tile, 1), :], ck_ref[0, 0], cv_ref[0, 0], atts_ref, pl.ds(tile, 1))
        wnb_ref[...] = wn_ref[...].astype(BF16)
        block(mix_ref, x_ref, h1_ref, q_ref)


def _out_q(mix, x, mix_s, xs, ck, cv, w_out, w_cq, g2, w_next, *, tm, first_row):
    t, d = x.shape
    ts = xs.shape[0]
    n_tiles = t // tm
    next_rows = w_next.shape[0] // n_tiles
    assert n_tiles * next_rows == w_next.shape[0]
    cache_spec = pl.BlockSpec((1, 1, N_MEM, H_X, DH_X),
                              lambda s: (0, first_row + jnp.maximum(s - N_LOAD, 0), 0, 0, 0))
    return pl.pallas_call(
        functools.partial(_outq_kernel, first_row=first_row),
        grid=(N_LOAD + n_tiles,),
        in_specs=[_tile_spec(tm), _tile_spec(tm), _w_chunk_spec(), _w_chunk_spec(), _resident((1, d)),
                  _resident((ts, d)), _resident((ts, d)), cache_spec, cache_spec, _tile_spec(next_rows)],
        out_specs=[_tile_spec(tm), _tile_spec(tm), _whole_out((ts, d)), _whole_out((ts, d)),
                   _whole_out((n_tiles, d)), _tile_spec(next_rows)],
        out_shape=[jax.ShapeDtypeStruct((t, d), F32), jax.ShapeDtypeStruct((t, d), BF16),
                   jax.ShapeDtypeStruct((ts, d), F32), jax.ShapeDtypeStruct((ts, d), F32),
                   jax.ShapeDtypeStruct((n_tiles, d), F32), jax.ShapeDtypeStruct(w_next.shape, BF16)],
        scratch_shapes=[pltpu.VMEM((d, d), BF16), pltpu.VMEM((d, d), BF16)],
        compiler_params=_params(("arbitrary",)),
        name="out_q",
    )(mix, x, w_out, w_cq, g2, mix_s, xs, ck, cv, w_next)


def _softmax_rows(s):
    m = jnp.max(s, axis=-1, keepdims=True)
    e = jnp.exp(s - m)
    return e / jnp.sum(e, axis=-1, keepdims=True)


def _attn_co_kernel(q_ref, mk_ref, mv_ref, h1_ref, wc_ref, g3_ref, atts_ref, h1s_ref, qs_ref, ck_ref, cv_ref,
                    h2_ref, xn_ref, h2s_ref, xns_ref, att_ref, tail_ref, *, n_tiles):
    s = pl.program_id(0)
    n_head = atts_ref.shape[0]

    def co_block(att, h1_r, h2_r, xn_r):
        h2 = h1_r[...] + jnp.dot(att.astype(BF16), wc_ref[...], preferred_element_type=F32)
        h2_r[...] = h2
        xn_r[...] = _rms(h2, g3_ref).astype(BF16)

    _cache_attn_row(qs_ref[pl.ds(n_head + s, 1), :], ck_ref[0, 0], cv_ref[0, 0], tail_ref, pl.ds(s, 1))
    for h in range(H_X):
        cols = slice(h * DH_X, (h + 1) * DH_X)
        sc = lax.dot_general(q_ref[:, cols], mk_ref[:, cols], (((1,), (1,)), ((), ())),
                             preferred_element_type=F32) * (DH_X ** -0.5)
        p = _softmax_rows(sc)
        att_ref[:, cols] = jnp.dot(p.astype(BF16), mv_ref[:, cols],
                                   preferred_element_type=F32).astype(BF16)
    co_block(att_ref[...], h1_ref, h2_ref, xn_ref)

    @pl.when(s == n_tiles - 1)
    def _():
        co_block(jnp.concatenate([atts_ref[...], tail_ref[...]], axis=0), h1s_ref, h2s_ref, xns_ref)


def _attn_co(q, mk, mv, h1, att_head, h1_s, q_s, ck, cv, w_co, g3, *, tm, tiles_per_batch):
    t, d = h1.shape
    ts = h1_s.shape[0]
    n_tiles = t // tm
    n_head = att_head.shape[0]
    assert n_head + n_tiles == ts
    assert w_co.dtype == BF16
    tile = pl.BlockSpec((tm, d), lambda s: (s, 0))
    mem_spec = pl.BlockSpec((N_MEM, d), lambda s: (s // tiles_per_batch, 0))
    cache_spec = pl.BlockSpec((1, 1, N_MEM, H_X, DH_X), lambda s: (0, n_head + s, 0, 0, 0))
    return pl.pallas_call(
        functools.partial(_attn_co_kernel, n_tiles=n_tiles),
        grid=(n_tiles,),
        in_specs=[tile, mem_spec, mem_spec, tile, _resident((d, d)), _resident((1, d)),
                  _resident((n_head, d)), _resident((ts, d)), _resident((ts, d)), cache_spec, cache_spec],
        out_specs=[tile, tile, _whole_out((ts, d)), _whole_out((ts, d))],
        out_shape=[jax.ShapeDtypeStruct((t, d), F32), jax.ShapeDtypeStruct((t, d), BF16),
                   jax.ShapeDtypeStruct((ts, d), F32), jax.ShapeDtypeStruct((ts, d), BF16)],
        scratch_shapes=[pltpu.VMEM((tm, d), BF16), pltpu.VMEM((n_tiles, d), F32)],
        compiler_params=_params(("arbitrary",)),
        name="attn_co",
    )(q, mk, mv, h1, w_co, g3, att_head, h1_s, q_s, ck, cv)


def _mix_prompt_kernel(u_ref, v_ref, q_ref, k_ref, vb_ref, g_ref, ws_ref, bt_ref, gn_ref,
                       us_ref, vs_ref, qs_ref, ks_ref, vbs_ref, gs_ref, w0_ref, b0_ref, ss_ref,
                       o_ref, so_ref, os_ref, sso_ref,
                       st_ref, wt_ref, bias_ref, dm_ref, qd_ref, kd_ref, *, batch, n_chunks):
    n = pl.program_id(0)
    _mix_sample_rows(us_ref, vs_ref, qs_ref, ks_ref, vbs_ref, gs_ref, w0_ref, b0_ref, gn_ref, ss_ref,
                     os_ref, sso_ref)

    @pl.when(n == 0)
    def _():
        ii = lax.broadcasted_iota(jnp.int32, (CHUNK, CHUNK), 0)
        jj = lax.broadcasted_iota(jnp.int32, (CHUNK, CHUNK), 1)
        causal = ii >= jj
        diff = jnp.maximum((ii - jj).astype(F32), 0.0)
        ridx = ii.astype(F32)
        for h in range(H_A):
            lg = LOG_G[h]
            wt_ref[h] = (ws_ref[h] * causal.astype(F32)).astype(BF16)
            bias_ref[h] = jnp.broadcast_to(bt_ref[:, h:h + 1], (CHUNK, CH_A))
            dm_ref[h] = jnp.where(causal, jnp.exp(lg * diff), 0.0)
            qd_ref[h] = jnp.exp(lg * (ridx + 1.0))
            kd_ref[h] = jnp.exp(lg * (CHUNK - 1.0 - ridx))
        st_ref[...] = jnp.zeros_like(st_ref)

    for b in range(batch):
        for h in range(H_A):
            cols = slice(h * CH_A, (h + 1) * CH_A)
            mixed = jnp.dot(wt_ref[h], v_ref[b, :, cols], preferred_element_type=F32) + bias_ref[h]
            o_ref[b, :, cols] = (u_ref[b, :, cols].astype(F32) * mixed).astype(BF16)

            kh = k_ref[b, :, cols]
            vh = vb_ref[b, :, cols]
            state = st_ref[b, h]
            state_bf = state.astype(BF16)
            qh = q_ref[b, :, cols]
            scores = lax.dot_general(qh, kh, (((1,), (1,)), ((), ())), preferred_element_type=F32) * dm_ref[h]
            intra = jnp.dot(scores.astype(BF16), vh, preferred_element_type=F32)
            cross = jnp.dot(qh, state_bf, preferred_element_type=F32) * qd_ref[h]
            ret = intra + cross
            mu = jnp.mean(ret, axis=-1, keepdims=True)
            rc = ret - mu
            var = jnp.mean(rc * rc, axis=-1, keepdims=True)
            normed = rc * lax.rsqrt(var + EPS) * gn_ref[:, cols]
            o_ref[b, :, W_A + h * DV_B:W_A + (h + 1) * DV_B] = (
                g_ref[b, :, cols].astype(F32) * normed).astype(BF16)
            kd = (kh.astype(F32) * kd_ref[h]).astype(BF16)
            st_ref[b, h] = state * math.exp(LOG_G[h] * CHUNK) + lax.dot_general(
                kd, vh, (((0,), (0,)), ((), ())), preferred_element_type=F32)

    @pl.when(n == n_chunks - 1)
    def _():
        so_ref[...] = st_ref[...]


def _mixers(u, rest, u_s, rest_s, ws, bt, gn, w0, b0, state_s, *, n_chunks):
    batch, seq, _ = u.shape
    ts = u_s.shape[0]
    rb = ts // n_chunks
    assert rb * n_chunks == ts and rb % 8 == 0

    def sec(s):
        return pl.BlockSpec((batch, CHUNK, SEC), lambda n, s=s: (0, n, s))

    def sec_s(s):
        return pl.BlockSpec((None, rb, SEC), lambda n, s=s: (s, n, 0))

    state_spec = pl.BlockSpec((rb, H_B, DK_B, DV_B), lambda n: (n, 0, 0, 0))
    table = pltpu.VMEM((H_B, CHUNK, CHUNK), F32)
    return pl.pallas_call(
        functools.partial(_mix_prompt_kernel, batch=batch, n_chunks=n_chunks),
        grid=(n_chunks,),
        in_specs=[sec(0), sec(0), sec(1), sec(2), sec(3), sec(4),
                  _resident((H_A, CHUNK, CHUNK)), _resident((CHUNK, H_A)), _resident((1, W_B)),
                  pl.BlockSpec((rb, SEC), lambda n: (n, 0)), sec_s(0), sec_s(1), sec_s(2), sec_s(3), sec_s(4),
                  _resident((1, W_A)), _resident((1, W_A)), state_spec],
        out_specs=[pl.BlockSpec((batch, CHUNK, W_A + W_B), lambda n: (0, n, 0)),
                   _whole_out((batch, H_B, DK_B, DV_B)),
                   pl.BlockSpec((rb, W_A + W_B), lambda n: (n, 0)), state_spec],
        out_shape=[jax.ShapeDtypeStruct((batch, seq, W_A + W_B), BF16),
                   jax.ShapeDtypeStruct((batch, H_B, DK_B, DV_B), F32),
                   jax.ShapeDtypeStruct((ts, W_A + W_B), F32),
                   jax.ShapeDtypeStruct(state_s.shape, F32)],
        scratch_shapes=[pltpu.VMEM((batch, H_B, DK_B, DV_B), F32), pltpu.VMEM((H_A, CHUNK, CHUNK), BF16),
                        table, table, table, table],
        compiler_params=_params(("arbitrary",)),
        name="mixers",
    )(u, rest, rest, rest, rest, rest, ws, bt, gn,
      u_s, rest_s, rest_s, rest_s, rest_s, rest_s, w0, b0, state_s)


def _mix_sample_rows(u_ref, v_ref, q_ref, k_ref, vb_ref, g_ref, w0_ref, b0_ref, gn_ref, s_ref,
                     o_ref, so_ref):
    n_rows = u_ref.shape[0]
    rid = lax.broadcasted_iota(jnp.int32, (n_rows, 1), 0)
    o_ref[:, :W_A] = u_ref[...] * (w0_ref[...] * v_ref[...] + b0_ref[...])

    for h in range(H_B):
        cols = slice(h * DK_B, (h + 1) * DK_B)
        g_h = math.exp(LOG_G[h])
        q_blk = q_ref[:, cols]
        k_blk = k_ref[:, cols]
        v_blk = vb_ref[:, cols]
        v_bf = v_blk.astype(BF16)
        intra = jnp.sum(q_blk * k_blk, axis=-1, keepdims=True) * v_blk
        cross = jnp.zeros((n_rows, DV_B), F32)
        for r in range(n_rows):
            state = s_ref[r, h]
            q_only_r = jnp.where(rid == r, q_blk, 0.0).astype(BF16)
            cross = cross + jnp.dot(q_only_r, state.astype(BF16), preferred_element_type=F32)
            k_only_r = jnp.where(rid == r, k_blk, 0.0).astype(BF16)
            outer = lax.dot_general(k_only_r, v_bf, (((0,), (0,)), ((), ())), preferred_element_type=F32)
            so_ref[r, h] = state * g_h + outer
        ret = intra + cross * g_h
        mu = jnp.mean(ret, axis=-1, keepdims=True)
        rc = ret - mu
        var = jnp.mean(rc * rc, axis=-1, keepdims=True)
        normed = rc * lax.rsqrt(var + EPS) * gn_ref[:, cols]
        o_ref[:, W_A + h * DV_B:W_A + (h + 1) * DV_B] = g_ref[:, cols] * normed


ATTN_SAMPLE_ROWS = 2
SUBLANES = 8
MEM_CHUNK = 256
MEM_LOAD = D_MODEL // MEM_CHUNK


def _attn_mem_kernel(q_ref, ck_ref, cv_ref, x_ref, g_ref, wk_ref, wv_ref,
                     o_ref, k5_ref, k2_ref, v5_ref, v2_ref, wb_ref, *, batch):
    i = pl.program_id(0)
    k_proj, v_load, v_proj = MEM_LOAD, MEM_LOAD + batch, 2 * MEM_LOAD + batch

    def load(w_ref, c):
        wb_ref[pl.ds(pl.multiple_of(c * MEM_CHUNK, MEM_CHUNK), MEM_CHUNK), :] = w_ref[...].astype(BF16)

    def project(o5_ref, o2_ref):
        res = jnp.dot(_rms(x_ref[...], g_ref).astype(BF16), wb_ref[...], preferred_element_type=F32)
        o2_ref[...] = res.astype(BF16)
        for h in range(H_X):
            o5_ref[0, 0, :, h, :] = res[:, h * DH_X:(h + 1) * DH_X]

    pl.when(i < k_proj)(lambda: load(wk_ref, i))
    pl.when((i >= k_proj) & (i < v_load))(lambda: project(k5_ref, k2_ref))
    pl.when((i >= v_load) & (i < v_proj))(lambda: load(wv_ref, i - v_load))
    pl.when((i >= v_proj) & (i < v_proj + batch))(lambda: project(v5_ref, v2_ref))

    for r in range(ATTN_SAMPLE_ROWS):
        row = pl.ds(i * ATTN_SAMPLE_ROWS + r, 1)
        _cache_attn_row(q_ref[row, :], ck_ref[0, r], cv_ref[0, r], o_ref, row)


def _cache_attn_row(q, k_blk, v_blk, o_ref, out_row):
    pairs = N_MEM * H_X
    fold = lambda t: pltpu.roll(t, H_X, 1)
    q8 = jnp.concatenate([q[:, (s % H_X) * DH_X:(s % H_X + 1) * DH_X] for s in range(SUBLANES)], axis=0)
    k3 = k_blk.reshape(pairs // SUBLANES, SUBLANES, DH_X)
    v3 = v_blk.reshape(pairs // SUBLANES, SUBLANES, DH_X)
    s = jnp.sum(k3 * q8[None], axis=-1, keepdims=True) * (DH_X ** -0.5)
    m = jnp.max(s, axis=0, keepdims=True)
    m = jnp.maximum(m, fold(m))
    e = jnp.exp(s - m)
    l = jnp.sum(e, axis=0, keepdims=True)
    acc = jnp.sum(e * v3, axis=0, keepdims=True)
    out = ((acc + fold(acc)) / (l + fold(l)))[0]
    for h in range(H_X):
        o_ref[out_row, h * DH_X:(h + 1) * DH_X] = out[h:h + 1, :]


def _attn_mem(q, ck, cv, mem, g, w_k, w_v, *, batch, n_rows):
    t, d = q.shape
    rb = ATTN_SAMPLE_ROWS
    steps = n_rows // rb
    assert steps * rb == n_rows
    k_proj, v_load, v_proj = MEM_LOAD, MEM_LOAD + batch, 2 * MEM_LOAD + batch
    assert v_proj + batch <= steps

    def prompt(i):
        return jnp.where(i < v_load, jnp.clip(i - k_proj, 0, batch - 1), jnp.clip(i - v_proj, 0, batch - 1))

    def k_idx(i):
        return jnp.clip(i - k_proj, 0, batch - 1)

    def v_idx(i):
        return jnp.clip(i - v_proj, 0, batch - 1)

    cache_spec = pl.BlockSpec((1, rb, N_MEM, H_X, DH_X), lambda i: (0, i, 0, 0, 0))
    out5 = jax.ShapeDtypeStruct((1, batch, N_MEM, H_X, DH_X), F32)
    out2 = jax.ShapeDtypeStruct((batch * N_MEM, d), BF16)
    return pl.pallas_call(
        functools.partial(_attn_mem_kernel, batch=batch),
        grid=(steps,),
        in_specs=[_resident((t, d)), cache_spec, cache_spec,
                  pl.BlockSpec((N_MEM, d), lambda i: (prompt(i), 0)), _resident((1, d)),
                  pl.BlockSpec((MEM_CHUNK, d), lambda i: (jnp.clip(i, 0, MEM_LOAD - 1), 0)),
                  pl.BlockSpec((MEM_CHUNK, d), lambda i: (jnp.clip(i - v_load, 0, MEM_LOAD - 1), 0))],
        out_specs=[_whole_out((n_rows, d)),
                   pl.BlockSpec((1, 1, N_MEM, H_X, DH_X), lambda i: (0, k_idx(i), 0, 0, 0)),
                   pl.BlockSpec((N_MEM, d), lambda i: (k_idx(i), 0)),
                   pl.BlockSpec((1, 1, N_MEM, H_X, DH_X), lambda i: (0, v_idx(i), 0, 0, 0)),
                   pl.BlockSpec((N_MEM, d), lambda i: (v_idx(i), 0))],
        out_shape=[jax.ShapeDtypeStruct((n_rows, d), F32), out5, out2, out5, out2],
        scratch_shapes=[pltpu.VMEM((d, d), BF16)],
        compiler_params=_params(("arbitrary",)),
        name="attn_mem",
    )(q, ck, cv, mem, g, w_k, w_v)


def _rope_tables(pos):
    half = DK_B // 2
    freqs = np.float64(ROPE_THETA) ** (-np.arange(half, dtype=np.float64) / half)
    ang = np.asarray(pos, np.float64)[:, None] * freqs[None, :]
    cos = np.cos(ang)
    sin = np.sin(ang)
    return (jnp.asarray(np.concatenate([cos, cos], axis=-1), F32),
            jnp.asarray(np.concatenate([-sin, sin], axis=-1), F32))


def kernel(x_prompt, x_sample, mem_prompt, cache_mem_k, cache_mem_v, state_ret, norm1_g, w_in, sgu_norm_g, sgu_w_s, sgu_b, ret_gn_g, w_out, norm2_g, mem_norm_g, w_cq, w_ck, w_cv, w_co, norm3_g, w_ff1, w_ff2, final_norm_g):
    batch, seq, d = x_prompt.shape
    ts = x_sample.shape[0]
    n_chunks = seq // CHUNK
    tp = batch * seq

    g1 = norm1_g[0][None, :]
    g2 = norm2_g[0][None, :]
    g3 = norm3_g[0][None, :]
    gm = mem_norm_g[0][None, :]
    gf = final_norm_g[None, :]
    lng = sgu_norm_g[0][None, :]
    gn = ret_gn_g[0][None, :]
    ws = sgu_w_s[0]
    sb = sgu_b[0]
    xp = x_prompt.reshape(tp, d)
    xs = x_sample.reshape(ts, d)

    rope_p = _rope_tables(np.arange(seq))
    rope_s = _rope_tables(np.full((ts,), PAST_LEN))
    u_p, xn_p, u_s, xn_s = _in_proj_u(xp, xs, g1, w_in[0], tm=TM_IN_PROJ)
    rest_p, rest_s, v_rows = _in_proj_rest(xn_p, xn_s, w_in[0], rope_p, rope_s, lng, tm=TM_IN_PROJ)
    w0 = jnp.repeat(ws[:, 0, 0], CH_A)[None, :]
    b0 = jnp.repeat(sb[:, 0], CH_A)[None, :]
    mix_p, state_p, mix_s, state_s = _mixers(
        u_p.reshape(batch, seq, SEC), rest_p.reshape(batch, seq, (N_SEC - 1) * SEC), u_s, rest_s,
        ws, sb.T, gn, w0, b0, state_ret[0], n_chunks=n_chunks)
    mix_p = mix_p.reshape(tp, W_A + W_B)

    mem = mem_prompt.reshape(batch * N_MEM, d)
    n_mem_rows = ts - 2 * (tp // TM_RESIDENT)
    h1_p, q_p, h1_s, q_s, att_mid, w_co_bf = _out_q(mix_p, xp, mix_s, xs, cache_mem_k, cache_mem_v,
                                                    w_out[0], w_cq[0], g2, w_co[0],
                                                    tm=TM_RESIDENT, first_row=n_mem_rows)
    att_head, mk5, mk, mv5, mv = _attn_mem(q_s, cache_mem_k, cache_mem_v, mem, gm, w_ck[0], w_cv[0],
                                           batch=batch, n_rows=n_mem_rows)
    att_head = jnp.concatenate([att_head, att_mid], axis=0)
    h2_p, xn3_p, h2_s, xn3_s = _attn_co(q_p, mk, mv, h1_p, att_head, h1_s, q_s, cache_mem_k, cache_mem_v,
                                        w_co_bf, g3, tm=TM_RESIDENT, tiles_per_batch=seq // TM_RESIDENT)

    hid_p, hid_s = _ff1(xn3_p, xn3_s, w_ff1[0], tm=TM_MLP, tn=TN_MLP)
    h3_p, part_s = _ff2(hid_p, hid_s, w_ff2[0], h2_p, tm=TM_MLP, tn=TN_MLP, tk=TK_MLP)
    y_p = _final_norm(h3_p, gf, tm=TM_NORM)
    y_s = _add_norm_sample(part_s, h2_s, gf)

    return (y_p.reshape(batch, seq, d),
            y_s,
            mk5,
            mv5,
            state_p[None],
            state_s[None],
            v_rows[None])
```
